```python
import math
import jax
import jax.numpy as jnp
from jax import lax
import numpy as np

D_MODEL = 2048
BATCH = 4
SEQ = 2048
DEPTH = 1
DEC_BATCH = 128
DEC_SEQ = 1
PAST_LEN = 16384
PAGE_SIZE = 128

HGRN_WIDTH = D_MODEL // 2
HGRN_HEAD_DIM = 128
HGRN_HEADS = HGRN_WIDTH // HGRN_HEAD_DIM
HGRN_CHUNK = 16
S5_WIDTH = D_MODEL - HGRN_WIDTH
S5_GROUP = 16
S5_GROUPS = S5_WIDTH // S5_GROUP
S5_STATE = 64
FFN_HIDDEN = -(-8 * D_MODEL // (3 * 256)) * 256
IN_PROJ_WIDTH = 4 * HGRN_WIDTH + S5_WIDTH + 2 * D_MODEL
RMS_EPS = 1e-6

kernel_name = 'hgrn2_s5_gated_hybrid_step'


def rmsnorm(x, g):
    xf = x.astype(jnp.float32)
    xf = xf * lax.rsqrt(jnp.mean(xf * xf, axis=-1, keepdims=True) + RMS_EPS)
    return (xf * g.astype(jnp.float32)).astype(x.dtype)


def hgrn_lower_bounds(lb_logits):
    return jnp.cumsum(jax.nn.softmax(lb_logits.astype(jnp.float32), axis=0), axis=0)


def hgrn2_recurrence(q, k, v, g, S0):
    Bsz, T, H, _ = q.shape
    C = min(HGRN_CHUNK, T)
    n_chunks = -(-T // C)
    pad = n_chunks * C - T

    def prep(a):
        a = jnp.pad(a, ((0, 0), (0, pad), (0, 0), (0, 0)))
        return a.reshape(Bsz, n_chunks, C, H, a.shape[-1]).transpose(1, 0, 3, 2, 4)

    qc, kc, vc, gc = prep(q), prep(k), prep(v), prep(g)
    b = jnp.cumsum(gc, axis=3)
    mid = C // 2
    b_mid = b[:, :, :, mid:mid + 1]
    b_last = b[:, :, :, -1:]
    scores = jnp.einsum('nbhtk,nbhsk->nbhts', qc * jnp.exp(b - b_mid), kc * jnp.exp(b_mid - b))
    causal = jnp.tril(jnp.ones((C, C), dtype=bool))
    scores = jnp.where(causal, scores, 0.0)
    o_intra = jnp.einsum('nbhts,nbhsv->nbhtv', scores, vc)
    q_in = qc * jnp.exp(b)
    k_upd = kc * jnp.exp(b_last - b)
    decay = jnp.exp(b_last[:, :, :, 0, :])

    def step(S, inp):
        q_i, k_u, v_i, dec = inp
        o_inter = jnp.einsum('bhtk,bhkv->bhtv', q_i, S)
        S = dec[..., None] * S + jnp.einsum('bhsk,bhsv->bhkv', k_u, v_i)
        return S, o_inter

    S_final, o_inter = lax.scan(step, S0, (q_in, k_upd, vc, decay))
    o = (o_intra + o_inter).transpose(1, 0, 3, 2, 4).reshape(Bsz, n_chunks * C, H, vc.shape[-1])[:, :T]
    return o, S_final


def s5_scan(u, lam_re, lam_im, log_dt, B_re, B_im, C_re, C_im, d_skip, h0_re, h0_im):
    dt = jnp.exp(log_dt)[:, None]
    mag = jnp.exp(lam_re * dt)
    ab_re = mag * jnp.cos(lam_im * dt)
    ab_im = mag * jnp.sin(lam_im * dt)
    den = lam_re * lam_re + lam_im * lam_im
    nr = ab_re - 1.0
    ni = ab_im
    z_re = (nr * lam_re + ni * lam_im) / den
    z_im = (ni * lam_re - nr * lam_im) / den
    bb_re = z_re[:, :, None] * B_re - z_im[:, :, None] * B_im
    bb_im = z_re[:, :, None] * B_im + z_im[:, :, None] * B_re
    bu_re = jnp.einsum('gnp,btgp->btgn', bb_re, u)
    bu_im = jnp.einsum('gnp,btgp->btgn', bb_im, u)
    bu_re = bu_re.at[:, 0].add(ab_re * h0_re - ab_im * h0_im)
    bu_im = bu_im.at[:, 0].add(ab_re * h0_im + ab_im * h0_re)
    T = u.shape[1]
    a_re = jnp.broadcast_to(ab_re, (1, T) + ab_re.shape)
    a_im = jnp.broadcast_to(ab_im, (1, T) + ab_im.shape)

    def combine(e1, e2):
        a1r, a1i, b1r, b1i = e1
        a2r, a2i, b2r, b2i = e2
        return (a2r * a1r - a2i * a1i,
                a2r * a1i + a2i * a1r,
                a2r * b1r - a2i * b1i + b2r,
                a2r * b1i + a2i * b1r + b2i)

    _, _, h_re, h_im = lax.associative_scan(combine, (a_re, a_im, bu_re, bu_im), axis=1)
    y = (jnp.einsum('gpn,btgn->btgp', C_re, h_re) - jnp.einsum('gpn,btgn->btgp', C_im, h_im)
         + d_skip.reshape(S5_GROUPS, S5_GROUP) * u)
    return y, h_re[:, -1], h_im[:, -1]


def decoder_layer(x, S0, h0_re, h0_im, l, w):
    Bsz, T, _ = x.shape
    f32 = jnp.float32
    h = rmsnorm(x, w['norm1_g'][l])
    proj = h @ w['w_in'][l]
    hw = HGRN_WIDTH
    splits = [hw, 2 * hw, 3 * hw, 4 * hw, 4 * hw + S5_WIDTH, 4 * hw + S5_WIDTH + D_MODEL]
    q, f_logit, i_in, og, u_in, gate_a, gate_b = jnp.split(proj, splits, axis=-1)

    lb = hgrn_lower_bounds(w['lb_logits'])[l]
    f = lb + (1.0 - lb) * jax.nn.sigmoid(f_logit.astype(f32))

    def heads(a):
        return a.astype(f32).reshape(Bsz, T, HGRN_HEADS, HGRN_HEAD_DIM)

    o_h, S_new = hgrn2_recurrence(heads(q), heads(1.0 - f), heads(i_in), heads(jnp.log(f)), S0.astype(f32))
    o_h = o_h * lax.rsqrt(jnp.mean(o_h * o_h, axis=-1, keepdims=True) + RMS_EPS)
    o_h = o_h * w['hgrn_norm_g'][l].astype(f32).reshape(HGRN_HEADS, HGRN_HEAD_DIM)
    o_a = (o_h.reshape(Bsz, T, hw) * jax.nn.silu(og.astype(f32))).astype(x.dtype)

    u = u_in.astype(f32).reshape(Bsz, T, S5_GROUPS, S5_GROUP)
    y5, h_re, h_im = s5_scan(u, w['s5_lam_re'][l].astype(f32), w['s5_lam_im'][l].astype(f32),
                             w['s5_log_dt'][l].astype(f32), w['s5_B_re'][l].astype(f32),
                             w['s5_B_im'][l].astype(f32), w['s5_C_re'][l].astype(f32),
                             w['s5_C_im'][l].astype(f32), w['s5_D'][l].astype(f32),
                             h0_re.astype(f32), h0_im.astype(f32))
    z = jax.nn.gelu(y5.reshape(Bsz, T, S5_WIDTH), approximate=False).astype(x.dtype)
    zg = z @ w['w_s5_glu'][l]
    o_b = zg[..., :S5_WIDTH] * jax.nn.sigmoid(zg[..., S5_WIDTH:])

    merged = (jax.nn.sigmoid(gate_a) * (o_a @ w['w_proj_a'][l])
              + jax.nn.sigmoid(gate_b) * (o_b @ w['w_proj_b'][l]))
    x = x + merged @ w['w_out'][l]

    h2 = rmsnorm(x, w['norm2_g'][l])
    up = h2 @ w['w_ffn_up'][l]
    x = x + (jax.nn.silu(up[..., :FFN_HIDDEN]) * up[..., FFN_HIDDEN:]) @ w['w_ffn_down'][l]
    return x, S_new, h_re, h_im


def setup_inputs(seed: int = 0) -> dict:
    key = jax.random.key(seed)
    ks = jax.random.split(key, 26)
    nrm = jax.random.normal
    L, D, hw, sw = DEPTH, D_MODEL, HGRN_WIDTH, S5_WIDTH
    G, N, P = S5_GROUPS, S5_STATE, S5_GROUP
    inp = {}
    inp['x_prompt'] = nrm(ks[0], (BATCH, SEQ, D), jnp.float32)
    inp['x_sample'] = nrm(ks[1], (DEC_BATCH, DEC_SEQ, D), jnp.float32)
    inp['state_hgrn'] = 0.3 * nrm(ks[2], (L, DEC_BATCH, HGRN_HEADS, HGRN_HEAD_DIM, HGRN_HEAD_DIM), jnp.float32)
    inp['state_s5_re'] = 0.1 * nrm(ks[3], (L, DEC_BATCH, G, N), jnp.float32)
    inp['state_s5_im'] = 0.1 * nrm(ks[4], (L, DEC_BATCH, G, N), jnp.float32)
    inp['lb_logits'] = 0.1 * nrm(ks[5], (L + 1, hw), jnp.float32)
    inp['norm1_g'] = 1.0 + 0.02 * nrm(ks[6], (L, D), jnp.float32)
    inp['w_in'] = nrm(ks[7], (L, D, IN_PROJ_WIDTH), jnp.float32) * D ** -0.5
    inp['hgrn_norm_g'] = 1.0 + 0.02 * nrm(ks[8], (L, hw), jnp.float32)
    inp['s5_lam_re'] = -0.5 + 0.01 * nrm(ks[9], (L, G, N), jnp.float32)
    inp['s5_lam_im'] = (math.pi * jnp.arange(N, dtype=jnp.float32))[None, None, :] + 0.01 * nrm(ks[10], (L, G, N), jnp.float32)
    inp['s5_log_dt'] = jax.random.uniform(ks[11], (L, G), jnp.float32, math.log(1e-3), math.log(1e-1))
    inp['s5_B_re'] = nrm(ks[12], (L, G, N, P), jnp.float32) * (2 * P) ** -0.5
    inp['s5_B_im'] = nrm(ks[13], (L, G, N, P), jnp.float32) * (2 * P) ** -0.5
    inp['s5_C_re'] = nrm(ks[14], (L, G, P, N), jnp.float32) * N ** -0.5
    inp['s5_C_im'] = nrm(ks[15], (L, G, P, N), jnp.float32) * N ** -0.5
    inp['s5_D'] = nrm(ks[16], (L, sw), jnp.float32)
    inp['w_s5_glu'] = nrm(ks[17], (L, sw, 2 * sw), jnp.float32) * sw ** -0.5
    inp['w_proj_a'] = nrm(ks[18], (L, hw, D), jnp.float32) * hw ** -0.5
    inp['w_proj_b'] = nrm(ks[19], (L, sw, D), jnp.float32) * sw ** -0.5
    inp['w_out'] = nrm(ks[20], (L, D, D), jnp.float32) * D ** -0.5
    inp['norm2_g'] = 1.0 + 0.02 * nrm(ks[21], (L, D), jnp.float32)
    inp['w_ffn_up'] = nrm(ks[22], (L, D, 2 * FFN_HIDDEN), jnp.float32) * D ** -0.5
    inp['w_ffn_down'] = nrm(ks[23], (L, FFN_HIDDEN, D), jnp.float32) * FFN_HIDDEN ** -0.5
    inp['final_norm_g'] = 1.0 + 0.02 * nrm(ks[24], (D,), jnp.float32)
    return inp


def reference(x_prompt, x_sample, state_hgrn, state_s5_re, state_s5_im, lb_logits, norm1_g, w_in,
              hgrn_norm_g, s5_lam_re, s5_lam_im, s5_log_dt, s5_B_re, s5_B_im, s5_C_re, s5_C_im, s5_D,
              w_s5_glu, w_proj_a, w_proj_b, w_out, norm2_g, w_ffn_up, w_ffn_down, final_norm_g):
    w = {'lb_logits': lb_logits, 'norm1_g': norm1_g, 'w_in': w_in, 'hgrn_norm_g': hgrn_norm_g,
         's5_lam_re': s5_lam_re, 's5_lam_im': s5_lam_im, 's5_log_dt': s5_log_dt,
         's5_B_re': s5_B_re, 's5_B_im': s5_B_im, 's5_C_re': s5_C_re, 's5_C_im': s5_C_im, 's5_D': s5_D,
         'w_s5_glu': w_s5_glu, 'w_proj_a': w_proj_a, 'w_proj_b': w_proj_b, 'w_out': w_out,
         'norm2_g': norm2_g, 'w_ffn_up': w_ffn_up, 'w_ffn_down': w_ffn_down}
    bp = x_prompt.shape[0]
    xp, xs = x_prompt, x_sample
    p_h, p_re, p_im, s_h, s_re, s_im = [], [], [], [], [], []
    for l in range(DEPTH):
        zero_h = jnp.zeros((bp, HGRN_HEADS, HGRN_HEAD_DIM, HGRN_HEAD_DIM), jnp.float32)
        zero_s5 = jnp.zeros((bp, S5_GROUPS, S5_STATE), jnp.float32)
        xp, sh, sr, si = decoder_layer(xp, zero_h, zero_s5, zero_s5, l, w)
        p_h.append(sh); p_re.append(sr); p_im.append(si)
        xs, sh, sr, si = decoder_layer(xs, state_hgrn[l], state_s5_re[l], state_s5_im[l], l, w)
        s_h.append(sh); s_re.append(sr); s_im.append(si)
    y_prompt = rmsnorm(xp, final_norm_g)
    y_sample = rmsnorm(xs, final_norm_g)
    return (y_prompt, y_sample, jnp.stack(p_h), jnp.stack(p_re), jnp.stack(p_im),
            jnp.stack(s_h), jnp.stack(s_re), jnp.stack(s_im))
```

```python
import functools
import math

import jax
import jax.numpy as jnp
from jax import lax
from jax.experimental import pallas as pl
from jax.experimental.pallas import tpu as pltpu

F32 = jnp.float32
BF16 = jnp.bfloat16

D_MODEL = 2048
HGRN_WIDTH = 1024
HEAD_DIM = 128
HEADS = 8
S5_WIDTH = 1024
S5_GROUPS = 64
S5_GROUP = 16
S5_STATE = 64
FFN_HIDDEN = 5632
IN_PROJ_WIDTH = 4 * HGRN_WIDTH + S5_WIDTH + 2 * D_MODEL
RMS_EPS = 1e-6

OFF_Q, OFF_F, OFF_I, OFF_OG = 0, HGRN_WIDTH, 2 * HGRN_WIDTH, 3 * HGRN_WIDTH
OFF_U = 4 * HGRN_WIDTH
OFF_GA = OFF_U + S5_WIDTH
OFF_GB = OFF_GA + D_MODEL

HGRN_SUB = 16
HGRN_CHUNK = 128
S5_CHUNK = 16
S5_FLAT = S5_CHUNK * S5_GROUP
N_POW = 10

V7X_VMEM_BYTES = 64 * 1024 * 1024
VMEM_LIMIT_CAP = V7X_VMEM_BYTES - 6 * 1024 * 1024


def _params(est_bytes):
    limit = min(max(int(est_bytes * 1.25) + (4 << 20), 32 << 20), VMEM_LIMIT_CAP)
    return pltpu.CompilerParams(vmem_limit_bytes=limit)


def _bdot(a, b):
    return jnp.dot(a, b, preferred_element_type=F32)


def _rms_rows(x, g):
    ms = jnp.mean(x * x, axis=-1, keepdims=True)
    return x * lax.rsqrt(ms + RMS_EPS) * g


def _inproj_kernel(x_ref, g_ref, w_ref, o_ref, h_scr):
    @pl.when(pl.program_id(1) == 0)
    def _():
        h_scr[...] = _rms_rows(x_ref[...], g_ref[...]).astype(BF16)

    o_ref[...] = _bdot(h_scr[...], w_ref[...].astype(BF16))


def _inproj(x, g, w, tm, tn):
    m, k = x.shape
    n = w.shape[1]
    est = 2 * tm * k * 4 + 2 * k * tn * 4 + 2 * tm * tn * 4 + tm * k * 2 + k * tn * 2 + tm * tn * 4
    return pl.pallas_call(
        _inproj_kernel,
        grid=(m // tm, n // tn),
        in_specs=[pl.BlockSpec((tm, k), lambda i, j: (i, 0)),
                  pl.BlockSpec((1, k), lambda i, j: (0, 0)),
                  pl.BlockSpec((k, tn), lambda i, j: (0, j))],
        out_specs=pl.BlockSpec((tm, tn), lambda i, j: (i, j)),
        out_shape=jax.ShapeDtypeStruct((m, n), F32),
        scratch_shapes=[pltpu.VMEM((tm, k), BF16)],
        compiler_params=_params(est),
        name="inproj",
    )(x, g, w)


def _glu_kernel(z_ref, wa_ref, wb_ref, o_ref):
    z = z_ref[...]
    a = _bdot(z, wa_ref[...].astype(BF16))
    b = _bdot(z, wb_ref[...].astype(BF16))
    o_ref[...] = (a * jax.nn.sigmoid(b)).astype(BF16)


def _glu(z, w, tm, tn):
    m, k = z.shape
    n = w.shape[1] // 2
    nj = n // tn
    est = 2 * tm * k * 2 + 4 * k * tn * 4 + 2 * tm * tn * 2 + 2 * k * tn * 2 + 3 * tm * tn * 4
    return pl.pallas_call(
        _glu_kernel,
        grid=(m // tm, nj),
        in_specs=[pl.BlockSpec((tm, k), lambda i, j: (i, 0)),
                  pl.BlockSpec((k, tn), lambda i, j: (0, j)),
                  pl.BlockSpec((k, tn), lambda i, j: (0, j + nj))],
        out_specs=pl.BlockSpec((tm, tn), lambda i, j: (i, j)),
        out_shape=jax.ShapeDtypeStruct((m, n), BF16),
        compiler_params=_params(est),
        name="s5_glu",
    )(z, w, w)


def _merge_kernel(oa_ref, ob_ref, wa_ref, wb_ref, ga_ref, gb_ref, o_ref):
    a = _bdot(oa_ref[...], wa_ref[...].astype(BF16))
    b = _bdot(ob_ref[...], wb_ref[...].astype(BF16))
    o_ref[...] = (jax.nn.sigmoid(ga_ref[...]) * a + jax.nn.sigmoid(gb_ref[...]) * b).astype(BF16)


def _merge(o_a, o_b, w_a, w_b, proj, tm, tn):
    m, k = o_a.shape
    n = w_a.shape[1]
    ja, jb = OFF_GA // tn, OFF_GB // tn
    est = 4 * tm * k * 2 + 4 * k * tn * 4 + 4 * tm * tn * 4 + 2 * tm * tn * 2 + 2 * k * tn * 2 + 3 * tm * tn * 4
    return pl.pallas_call(
        _merge_kernel,
        grid=(m // tm, n // tn),
        in_specs=[pl.BlockSpec((tm, k), lambda i, j: (i, 0)),
                  pl.BlockSpec((tm, k), lambda i, j: (i, 0)),
                  pl.BlockSpec((k, tn), lambda i, j: (0, j)),
                  pl.BlockSpec((k, tn), lambda i, j: (0, j)),
                  pl.BlockSpec((tm, tn), lambda i, j: (i, j + ja)),
                  pl.BlockSpec((tm, tn), lambda i, j: (i, j + jb))],
        out_specs=pl.BlockSpec((tm, tn), lambda i, j: (i, j)),
        out_shape=jax.ShapeDtypeStruct((m, n), BF16),
        compiler_params=_params(est),
        name="gated_merge",
    )(o_a, o_b, w_a, w_b, proj, proj)


def _outproj_kernel(m_ref, w_ref, x_ref, o_ref):
    o_ref[...] = x_ref[...] + _bdot(m_ref[...], w_ref[...].astype(BF16))


def _outproj(merged, w, x, tm, tn):
    m, k = merged.shape
    n = w.shape[1]
    est = 2 * tm * k * 2 + 2 * k * tn * 4 + 4 * tm * tn * 4 + k * tn * 2 + tm * tn * 4
    return pl.pallas_call(
        _outproj_kernel,
        grid=(m // tm, n // tn),
        in_specs=[pl.BlockSpec((tm, k), lambda i, j: (i, 0)),
                  pl.BlockSpec((k, tn), lambda i, j: (0, j)),
                  pl.BlockSpec((tm, tn), lambda i, j: (i, j))],
        out_specs=pl.BlockSpec((tm, tn), lambda i, j: (i, j)),
        out_shape=jax.ShapeDtypeStruct((m, n), F32),
        compiler_params=_params(est),
        name="out_proj",
    )(merged, w, x)


def _ffn_up_kernel(x_ref, g_ref, wa_ref, wb_ref, o_ref, h_scr):
    @pl.when(pl.program_id(1) == 0)
    def _():
        h_scr[...] = _rms_rows(x_ref[...], g_ref[...]).astype(BF16)

    h = h_scr[...]
    a = _bdot(h, wa_ref[...].astype(BF16))
    b = _bdot(h, wb_ref[...].astype(BF16))
    o_ref[...] = (jax.nn.silu(a) * b).astype(BF16)


def _ffn_up(x, g, w, tm, tn):
    m, k = x.shape
    n = w.shape[1] // 2
    nj = n // tn
    est = 2 * tm * k * 4 + 4 * k * tn * 4 + 2 * tm * tn * 2 + tm * k * 2 + 2 * k * tn * 2 + 3 * tm * tn * 4
    return pl.pallas_call(
        _ffn_up_kernel,
        grid=(m // tm, nj),
        in_specs=[pl.BlockSpec((tm, k), lambda i, j: (i, 0)),
                  pl.BlockSpec((1, k), lambda i, j: (0, 0)),
                  pl.BlockSpec((k, tn), lambda i, j: (0, j)),
                  pl.BlockSpec((k, tn), lambda i, j: (0, j + nj))],
        out_specs=pl.BlockSpec((tm, tn), lambda i, j: (i, j)),
        out_shape=jax.ShapeDtypeStruct((m, n), BF16),
        scratch_shapes=[pltpu.VMEM((tm, k), BF16)],
        compiler_params=_params(est),
        name="ffn_up",
    )(x, g, w, w)


def _ffn_down_kernel(nj, tn, a_ref, w_ref, x_ref, g_ref, o_ref, acc_scr):
    j = pl.program_id(1)
    acc_scr[j] = x_ref[...] + _bdot(a_ref[...], w_ref[...])

    @pl.when(j == nj - 1)
    def _():
        ss = jnp.zeros((acc_scr.shape[1], 1), F32)
        for jj in range(nj):
            v = acc_scr[jj]
            ss = ss + jnp.sum(v * v, axis=-1, keepdims=True)
        rs = lax.rsqrt(ss / (nj * tn) + RMS_EPS)
        for jj in range(nj):
            o_ref[:, jj * tn:(jj + 1) * tn] = acc_scr[jj] * rs * g_ref[:, jj * tn:(jj + 1) * tn]


def _ffn_down(act, w_bf16, x, g, tm, tn):
    m, k = act.shape
    n = w_bf16.shape[1]
    nj = n // tn
    est = 2 * tm * k * 2 + 2 * k * tn * 2 + 2 * tm * tn * 4 + 2 * tm * n * 4 + tm * n * 4 + 2 * tm * tn * 4
    return pl.pallas_call(
        functools.partial(_ffn_down_kernel, nj, tn),
        grid=(m // tm, nj),
        in_specs=[pl.BlockSpec((tm, k), lambda i, j: (i, 0)),
                  pl.BlockSpec((k, tn), lambda i, j: (0, j)),
                  pl.BlockSpec((tm, tn), lambda i, j: (i, j)),
                  pl.BlockSpec((1, n), lambda i, j: (0, 0))],
        out_specs=pl.BlockSpec((tm, n), lambda i, j: (i, 0)),
        out_shape=jax.ShapeDtypeStruct((m, n), F32),
        scratch_shapes=[pltpu.VMEM((nj, tm, tn), F32)],
        compiler_params=_params(est),
        name="ffn_down_final_norm",
    )(act, w_bf16, x, g)


def _lower_bound_kernel(lb_ref, o_ref):
    x = lb_ref[...]
    mx = jnp.max(x, axis=0, keepdims=True)
    e = jnp.exp(x - mx)
    o_ref[...] = e[0:1, :] / jnp.sum(e, axis=0, keepdims=True)


def _lower_bound(lb_logits):
    return pl.pallas_call(
        _lower_bound_kernel,
        out_shape=jax.ShapeDtypeStruct((1, HGRN_WIDTH), F32),
        name="hgrn_lower_bound",
    )(lb_logits)


def _split3(x):
    hi = x.astype(BF16)
    r = x - hi.astype(F32)
    mid = r.astype(BF16)
    lo = (r - mid.astype(F32)).astype(BF16)
    return hi, mid, lo


def _rows_ref(b, block, pick):
    parts = []
    for s0 in range(0, HGRN_CHUNK, block):
        parts.append(jnp.broadcast_to(b[s0 + pick:s0 + pick + 1, :], (block, b.shape[1])))
    return parts[0] if len(parts) == 1 else jnp.concatenate(parts, axis=0)


def _hgrn_prompt_kernel(n_chunks, q_ref, f_ref, i_ref, og_ref, lb_ref, gn_ref, oa_ref, s_ref, st_scr):
    t = pl.program_id(1)

    @pl.when(t == 0)
    def _():
        st_scr[...] = jnp.zeros(st_scr.shape, F32)

    c_ = HGRN_CHUNK
    row = lax.broadcasted_iota(jnp.int32, (c_, c_), 0)
    col = lax.broadcasted_iota(jnp.int32, (c_, c_), 1)
    tri = (col <= row).astype(BF16)
    m_diag = (row // HGRN_SUB == col // HGRN_SUB) & (col <= row)
    levels = (128, 64, 32)
    m_lvl = [(row // bs == col // bs) & (row % bs >= bs // 2) & (col % bs < bs // 2) for bs in levels]
    rcol = lax.broadcasted_iota(jnp.int32, (c_, 1), 0)
    second = [(rcol % bs) >= bs // 2 for bs in levels]

    lb = lb_ref[...]
    gn = gn_ref[...]

    def chunk(c, carry):
        r0 = pl.multiple_of(c * c_, c_)
        rows = pl.ds(r0, c_)
        f = lb + (1.0 - lb) * jax.nn.sigmoid(f_ref[rows, :])
        g = jnp.log(f)
        k_all = 1.0 - f
        g_hi, g_mid, g_lo = _split3(g)
        b = _bdot(tri, g_hi) + _bdot(tri, g_mid) + _bdot(tri, g_lo)
        b_last = b[c_ - 1:c_, :]
        e_in = jnp.exp(b)
        e_upd = jnp.exp(b_last - b)
        dec_all = jnp.exp(b_last)
        b_mid = _rows_ref(b, HGRN_SUB, HGRN_SUB // 2)
        e_dq = jnp.exp(b - b_mid)
        e_dk = jnp.exp(b_mid - b)
        e_lvl = []
        for li, bs in enumerate(levels):
            d = b - _rows_ref(b, bs, bs // 2 - 1)
            e_lvl.append(jnp.exp(jnp.where(second[li], d, -d)))
        q_all = q_ref[rows, :]
        v_all = i_ref[rows, :]
        og = og_ref[rows, :]
        for h in range(HEADS):
            ls = slice(h * HEAD_DIM, (h + 1) * HEAD_DIM)
            q, k, v = q_all[:, ls], k_all[:, ls], v_all[:, ls]
            nt = (((1,), (1,)), ((), ()))
            sc = lax.dot_general((q * e_dq[:, ls]).astype(BF16), (k * e_dk[:, ls]).astype(BF16), nt,
                                 preferred_element_type=F32)
            scores = jnp.where(m_diag, sc, 0.0)
            for li in range(len(levels)):
                e = e_lvl[li][:, ls]
                sc = lax.dot_general((q * e).astype(BF16), (k * e).astype(BF16), nt, preferred_element_type=F32)
                scores = jnp.where(m_lvl[li], sc, scores)
            v_b = v.astype(BF16)
            st = st_scr[h]
            o = _bdot(scores.astype(BF16), v_b)
            o = o + lax.dot_general((q * e_in[:, ls]).astype(BF16), st.astype(BF16), nt, preferred_element_type=F32)
            upd = lax.dot_general(v_b, (k * e_upd[:, ls]).astype(BF16), (((0,), (0,)), ((), ())),
                                  preferred_element_type=F32)
            st_scr[h] = dec_all[:, ls] * st + upd
            o = o * lax.rsqrt(jnp.mean(o * o, axis=-1, keepdims=True) + RMS_EPS) * gn[:, ls]
            oa_ref[rows, ls] = (o * jax.nn.silu(og[:, ls])).astype(BF16)
        return carry

    lax.fori_loop(0, n_chunks, chunk, 0)

    @pl.when(t == pl.num_programs(1) - 1)
    def _():
        for h in range(HEADS):
            s_ref[0, h] = st_scr[h].T


def _hgrn_prompt(proj, lb, gn, batch, seq, tt):
    m = batch * seq
    w = HGRN_WIDTH
    nt = seq // tt
    blk = lambda off: pl.BlockSpec((tt, w), lambda b, t, off=off: (b * nt + t, off // w))
    est = 8 * tt * w * 4 + 2 * tt * w * 2 + 3 * HEADS * HEAD_DIM * HEAD_DIM * 4 + 24 * HGRN_CHUNK * w * 4
    return pl.pallas_call(
        functools.partial(_hgrn_prompt_kernel, tt // HGRN_CHUNK),
        grid=(batch, nt),
        in_specs=[blk(OFF_Q), blk(OFF_F), blk(OFF_I), blk(OFF_OG),
                  pl.BlockSpec((1, w), lambda b, t: (0, 0)),
                  pl.BlockSpec((1, w), lambda b, t: (0, 0))],
        out_specs=[pl.BlockSpec((tt, w), lambda b, t: (b * nt + t, 0)),
                   pl.BlockSpec((1, HEADS, HEAD_DIM, HEAD_DIM), lambda b, t: (b, 0, 0, 0))],
        out_shape=[jax.ShapeDtypeStruct((m, w), BF16),
                   jax.ShapeDtypeStruct((batch, HEADS, HEAD_DIM, HEAD_DIM), F32)],
        scratch_shapes=[pltpu.VMEM((HEADS, HEAD_DIM, HEAD_DIM), F32)],
        compiler_params=_params(est),
        name="hgrn2_prompt",
    )(proj, proj, proj, proj, lb, gn)


def _hgrn_sample_kernel(nb, qt_ref, ft_ref, v_ref, og_ref, lbc_ref, gn_ref, s0_ref, oa_ref, s_ref, o_scr):
    lbc = lbc_ref[...]
    f_t = lbc + (1.0 - lbc) * jax.nn.sigmoid(ft_ref[...])
    k_t = 1.0 - f_t
    q_t = qt_ref[...]
    v = v_ref[...]
    for b in range(nb):
        s_new = f_t[:, b:b + 1] * s0_ref[b, 0] + k_t[:, b:b + 1] * v[b:b + 1, :]
        s_ref[b, 0] = s_new
        o_scr[b:b + 1, :] = jnp.sum(q_t[:, b:b + 1] * s_new, axis=0, keepdims=True)
    o = o_scr[...]
    o = o * lax.rsqrt(jnp.mean(o * o, axis=-1, keepdims=True) + RMS_EPS) * gn_ref[...]
    oa_ref[...] = (o * jax.nn.silu(og_ref[...])).astype(BF16)


def _hgrn_sample(q_t, f_t, proj, lb_col, gn, s0):
    nb = s0.shape[0]
    hd = HEAD_DIM
    est = 4 * nb * hd * hd * 4 + 16 * nb * hd * 4
    return pl.pallas_call(
        functools.partial(_hgrn_sample_kernel, nb),
        grid=(HEADS,),
        in_specs=[pl.BlockSpec((hd, nb), lambda h: (h, 0)),
                  pl.BlockSpec((hd, nb), lambda h: (h, 0)),
                  pl.BlockSpec((nb, hd), lambda h: (0, OFF_I // hd + h)),
                  pl.BlockSpec((nb, hd), lambda h: (0, OFF_OG // hd + h)),
                  pl.BlockSpec((hd, 1), lambda h: (h, 0)),
                  pl.BlockSpec((1, hd), lambda h: (0, h)),
                  pl.BlockSpec((nb, 1, hd, hd), lambda h: (0, h, 0, 0))],
        out_specs=[pl.BlockSpec((nb, hd), lambda h: (0, h)),
                   pl.BlockSpec((nb, 1, hd, hd), lambda h: (0, h, 0, 0))],
        out_shape=[jax.ShapeDtypeStruct((nb, HGRN_WIDTH), BF16),
                   jax.ShapeDtypeStruct(s0.shape, F32)],
        scratch_shapes=[pltpu.VMEM((nb, hd), F32)],
        compiler_params=_params(est),
        name="hgrn2_sample",
    )(q_t, f_t, proj, proj, lb_col, gn, s0)


def _cmul(ar, ai, br, bi):
    return ar * br - ai * bi, ar * bi + ai * br


def _gelu_exact(x):
    return 0.5 * x * (1.0 + lax.erf(x * math.sqrt(0.5)))


def _s5_discretise_kernel(lre_ref, lim_ref, ldt_ref, pre_ref, pim_ref, zre_ref, zim_ref):
    lam_re, lam_im = lre_ref[...], lim_ref[...]
    dt = jnp.exp(ldt_ref[...])
    mag = jnp.exp(lam_re * dt)
    ab_re = mag * jnp.cos(lam_im * dt)
    ab_im = mag * jnp.sin(lam_im * dt)
    den = lam_re * lam_re + lam_im * lam_im
    nr, ni = ab_re - 1.0, ab_im
    zre_ref[...] = (nr * lam_re + ni * lam_im) / den
    zim_ref[...] = (ni * lam_re - nr * lam_im) / den
    pr, pi = ab_re, ab_im
    for kk in range(N_POW):
        pre_ref[kk] = pr
        pim_ref[kk] = pi
        pr, pi = _cmul(pr, pi, pr, pi)


def _s5_discretise(lam_re, lam_im, log_dt):
    g, n = lam_re.shape
    return pl.pallas_call(
        _s5_discretise_kernel,
        out_shape=[jax.ShapeDtypeStruct((N_POW, g, n), F32), jax.ShapeDtypeStruct((N_POW, g, n), F32),
                   jax.ShapeDtypeStruct((g, n), F32), jax.ShapeDtypeStruct((g, n), F32)],
        name="s5_discretise",
    )(lam_re, lam_im, log_dt)


def _s5_weights_kernel(gb, prow_re, prow_im, pcol_re, pcol_im, zre_ref, zim_ref, btre_ref, btim_ref,
                       ctre_ref, ctim_ref, wt_ref, wx_ref, wc_ref, apw_ref):
    fl = S5_FLAT
    srow = lax.broadcasted_iota(jnp.int32, (fl, 1), 0) // S5_GROUP
    tlan = lax.broadcasted_iota(jnp.int32, (1, fl), 1) // S5_GROUP
    jrow = lax.broadcasted_iota(jnp.int32, (S5_STATE, 1), 0)
    for gi in range(gb):
        zr, zi = zre_ref[gi], zim_ref[gi]
        bbr, bbi = _cmul(zr, zi, btre_ref[gi], btim_ref[gi])
        xr = jnp.concatenate([bbr] * S5_CHUNK, axis=0)
        xi = jnp.concatenate([bbi] * S5_CHUNK, axis=0)
        e_x = (S5_CHUNK - 1) - srow
        for kk in range(4):
            ar, ai = prow_re[gi, kk:kk + 1, :], prow_im[gi, kk:kk + 1, :]
            yr, yi = _cmul(xr, xi, ar, ai)
            hit = ((e_x >> kk) & 1) == 1
            xr, xi = jnp.where(hit, yr, xr), jnp.where(hit, yi, xi)
        wx_ref[gi] = jnp.concatenate([xr, xi], axis=1).astype(BF16)
        cr, ci = ctre_ref[gi], ctim_ref[gi]
        for kk in range(4):
            ar, ai = pcol_re[gi, :, kk:kk + 1], pcol_im[gi, :, kk:kk + 1]
            yr, yi = _cmul(cr, ci, ar, ai)
            hit = ((tlan >> kk) & 1) == 1
            cr, ci = jnp.where(hit, yr, cr), jnp.where(hit, yi, ci)
        g0 = jnp.concatenate([cr, -ci], axis=0)
        c1r, c1i = _cmul(cr, ci, pcol_re[gi, :, 0:1], pcol_im[gi, :, 0:1])
        wc_ref[gi] = jnp.concatenate([c1r, -c1i], axis=0).astype(BF16)
        kflat = jnp.dot(jnp.concatenate([bbr, bbi], axis=1), g0, preferred_element_type=F32,
                        precision=lax.Precision.HIGHEST)
        pieces = [kflat]
        for s in range(1, S5_CHUNK):
            pieces.append(jnp.concatenate([jnp.zeros((S5_GROUP, s * S5_GROUP), F32),
                                           kflat[:, :fl - s * S5_GROUP]], axis=1))
        wt_ref[gi] = jnp.concatenate(pieces, axis=0).astype(BF16)
        pr = jnp.ones((S5_STATE, S5_STATE), F32)
        pi = jnp.zeros((S5_STATE, S5_STATE), F32)
        for kk in range(6):
            ar, ai = prow_re[gi, 4 + kk:5 + kk, :], prow_im[gi, 4 + kk:5 + kk, :]
            yr, yi = _cmul(pr, pi, ar, ai)
            hit = ((jrow >> kk) & 1) == 1
            pr, pi = jnp.where(hit, yr, pr), jnp.where(hit, yi, pi)
        apw_ref[gi] = jnp.concatenate([pr, pi], axis=1)


def _s5_weights(prow_re, prow_im, pcol_re, pcol_im, z_re, z_im, bt_re, bt_im, ct_re, ct_im, gb):
    g = S5_GROUPS
    n, p, fl = S5_STATE, S5_GROUP, S5_FLAT
    b3 = lambda a, b: pl.BlockSpec((gb, a, b), lambda i: (i, 0, 0))
    return pl.pallas_call(
        functools.partial(_s5_weights_kernel, gb),
        grid=(g // gb,),
        in_specs=[b3(N_POW, n), b3(N_POW, n), b3(n, N_POW), b3(n, N_POW), b3(1, n), b3(1, n),
                  b3(p, n), b3(p, n), b3(n, fl), b3(n, fl)],
        out_specs=[b3(fl, fl), b3(fl, 2 * n), b3(2 * n, fl), b3(n, 2 * n)],
        out_shape=[jax.ShapeDtypeStruct((g, fl, fl), BF16), jax.ShapeDtypeStruct((g, fl, 2 * n), BF16),
                   jax.ShapeDtypeStruct((g, 2 * n, fl), BF16), jax.ShapeDtypeStruct((g, n, 2 * n), F32)],
        name="s5_chunk_weights",
    )(prow_re, prow_im, pcol_re, pcol_im, z_re, z_im, bt_re, bt_im, ct_re, ct_im)


def _s5_prompt_kernel(gb, batch, u_ref, wt_ref, wx_ref, wc_ref, apw_ref, a16_ref, d_ref, z_ref, hend_ref,
                      x_scr, h_scr):
    n = S5_STATE
    nlo = u_ref.shape[1] // (2 * batch)
    slab = 2 * batch
    for gi in range(gb):
        u = u_ref[gi]
        u_b = u.astype(BF16)
        x_scr[gi] = _bdot(u_b, wx_ref[gi])
        ar, ai = a16_ref[gi, :, :n], a16_ref[gi, :, n:]
        hr = jnp.zeros((slab, n), F32)
        hi = jnp.zeros((slab, n), F32)
        for j in range(nlo):
            rs = slice(j * slab, (j + 1) * slab)
            h_scr[gi, rs, :n] = hr
            h_scr[gi, rs, n:] = hi
            nr, ni = _cmul(hr, hi, ar, ai)
            hr = nr + x_scr[gi, rs, :n]
            hi = ni + x_scr[gi, rs, n:]
        zeros = jnp.zeros((batch, n), F32)
        mr = jnp.concatenate([zeros, hr[:batch]], axis=0)
        mi = jnp.concatenate([zeros, hi[:batch]], axis=0)
        pr, pi = apw_ref[gi, :, :n], apw_ref[gi, :, n:]
        cr = pr[:, None, :] * mr[None] - pi[:, None, :] * mi[None]
        ci = pr[:, None, :] * mi[None] + pi[:, None, :] * mr[None]
        hs_r = h_scr[gi, :, :n] + cr.reshape(nlo * slab, n)
        hs_i = h_scr[gi, :, n:] + ci.reshape(nlo * slab, n)
        hs = jnp.concatenate([hs_r, hs_i], axis=1).astype(BF16)
        lr, li = _cmul(pr[nlo - 1:nlo], pi[nlo - 1:nlo], ar, ai)
        er, ei = _cmul(hr[:batch], hi[:batch], lr, li)
        hend_ref[gi] = jnp.concatenate([hr[batch:] + er, hi[batch:] + ei], axis=1)
        y = _bdot(u_b, wt_ref[gi]) + _bdot(hs, wc_ref[gi]) + d_ref[gi] * u
        z_ref[gi] = _gelu_exact(y).astype(BF16)


def _s5_prompt(u_flat, wt, wx, wc, apw, a16, d_flat, batch, gb):
    g, rows, fl = u_flat.shape
    n = S5_STATE
    b3 = lambda a, b: pl.BlockSpec((gb, a, b), lambda i: (i, 0, 0))
    est = gb * (2 * rows * fl * 4 + 2 * rows * fl * 2 + 2 * rows * 2 * n * 4 + 6 * rows * fl * 4) + (4 << 20)
    return pl.pallas_call(
        functools.partial(_s5_prompt_kernel, gb, batch),
        grid=(g // gb,),
        in_specs=[b3(rows, fl), b3(fl, fl), b3(fl, 2 * n), b3(2 * n, fl), b3(n, 2 * n), b3(1, 2 * n), b3(1, fl)],
        out_specs=[b3(rows, fl), b3(batch, 2 * n)],
        out_shape=[jax.ShapeDtypeStruct((g, rows, fl), BF16), jax.ShapeDtypeStruct((g, batch, 2 * n), F32)],
        scratch_shapes=[pltpu.VMEM((gb, rows, 2 * n), F32), pltpu.VMEM((gb, rows, 2 * n), F32)],
        compiler_params=_params(est),
        name="s5_prompt",
    )(u_flat, wt, wx, wc, apw, a16, d_flat)


def _s5_sample_kernel(gb, u_ref, bst_ref, ctre_ref, ctim_ref, a1_ref, hre_ref, him_ref, d_ref,
                      z_ref, ore_ref, oim_ref):
    n = S5_STATE
    u = u_ref[...]
    lane_g = lax.broadcasted_iota(jnp.int32, (1, gb * S5_GROUP), 1) // S5_GROUP
    bst = bst_ref[...]
    y = d_ref[...] * u
    for gi in range(gb):
        mine = lane_g == gi
        bu = _bdot(jnp.where(mine, u, 0.0).astype(BF16), bst)
        ar, ai = a1_ref[gi, :, :n], a1_ref[gi, :, n:]
        nr, ni = _cmul(hre_ref[gi], him_ref[gi], ar, ai)
        hr = nr + bu[:, :n]
        hi = ni + bu[:, n:]
        ore_ref[gi] = hr
        oim_ref[gi] = hi
        cmat = jnp.concatenate([jnp.where(mine, ctre_ref[gi], 0.0), jnp.where(mine, -ctim_ref[gi], 0.0)], axis=0)
        y = y + _bdot(jnp.concatenate([hr, hi], axis=1).astype(BF16), cmat.astype(BF16))
    z_ref[...] = _gelu_exact(y).astype(BF16)


def _s5_sample(proj, bstack, ct_re, ct_im, a1, h_re, h_im, d_row, gb):
    nb = proj.shape[0]
    g, n, p = S5_GROUPS, S5_STATE, S5_GROUP
    lanes = gb * p
    b3 = lambda a, b: pl.BlockSpec((gb, a, b), lambda i: (i, 0, 0))
    return pl.pallas_call(
        functools.partial(_s5_sample_kernel, gb),
        grid=(g // gb,),
        in_specs=[pl.BlockSpec((nb, lanes), lambda i: (0, OFF_U // lanes + i)),
                  pl.BlockSpec((lanes, 2 * n), lambda i: (i, 0)),
                  b3(n, lanes), b3(n, lanes), b3(1, 2 * n), b3(nb, n), b3(nb, n),
                  pl.BlockSpec((1, lanes), lambda i: (0, i))],
        out_specs=[pl.BlockSpec((nb, lanes), lambda i: (0, i)), b3(nb, n), b3(nb, n)],
        out_shape=[jax.ShapeDtypeStruct((nb, S5_WIDTH), BF16),
                   jax.ShapeDtypeStruct((g, nb, n), F32), jax.ShapeDtypeStruct((g, nb, n), F32)],
        name="s5_sample",
    )(proj, bstack, ct_re, ct_im, a1, h_re, h_im, d_row)


def _dense_tail(x, proj, o_a, z, w, tm):
    o_b = _glu(z, w['w_s5_glu'], tm, 512)
    merged = _merge(o_a, o_b, w['w_proj_a'], w['w_proj_b'], proj, tm, 512)
    x1 = _outproj(merged, w['w_out'], x, tm, 512)
    act = _ffn_up(x1, w['norm2_g'], w['w_ffn_up'], tm, 256)
    return _ffn_down(act, w['w_ffn_down_bf16'], x1, w['final_norm_g'], min(tm, 512), 512)


def kernel(x_prompt, x_sample, state_hgrn, state_s5_re, state_s5_im, lb_logits, norm1_g, w_in, hgrn_norm_g,
           s5_lam_re, s5_lam_im, s5_log_dt, s5_B_re, s5_B_im, s5_C_re, s5_C_im, s5_D, w_s5_glu, w_proj_a,
           w_proj_b, w_out, norm2_g, w_ffn_up, w_ffn_down, final_norm_g):
    l = 0
    bp, seq, d = x_prompt.shape
    nb = x_sample.shape[0]
    g, n, p = S5_GROUPS, S5_STATE, S5_GROUP
    w = {'w_s5_glu': w_s5_glu[l], 'w_proj_a': w_proj_a[l], 'w_proj_b': w_proj_b[l], 'w_out': w_out[l],
         'norm2_g': norm2_g[l][None, :], 'w_ffn_up': w_ffn_up[l], 'w_ffn_down_bf16': w_ffn_down[l].astype(BF16),
         'final_norm_g': final_norm_g[None, :]}
    g1 = norm1_g[l][None, :]
    gn = hgrn_norm_g[l][None, :]

    xp = x_prompt.reshape(bp * seq, d)
    xs = x_sample.reshape(nb, d)
    proj_p = _inproj(xp, g1, w_in[l], 1024, 512)
    proj_s = _inproj(xs, g1, w_in[l], nb, 512)

    lb = _lower_bound(lb_logits)
    oa_p, sh_p = _hgrn_prompt(proj_p, lb, gn, bp, seq, 256)
    q_t = proj_s[:, OFF_Q:OFF_Q + HGRN_WIDTH].T
    f_t = proj_s[:, OFF_F:OFF_F + HGRN_WIDTH].T
    oa_s, sh_s = _hgrn_sample(q_t, f_t, proj_s, lb.reshape(HGRN_WIDTH, 1), gn, state_hgrn[l])

    pw_re, pw_im, z_re, z_im = _s5_discretise(s5_lam_re[l], s5_lam_im[l], s5_log_dt[l][:, None])
    prow_re, prow_im = pw_re.transpose(1, 0, 2), pw_im.transpose(1, 0, 2)
    pcol_re, pcol_im = pw_re.transpose(1, 2, 0), pw_im.transpose(1, 2, 0)
    bt_re, bt_im = s5_B_re[l].transpose(0, 2, 1), s5_B_im[l].transpose(0, 2, 1)
    ct_re = jnp.tile(s5_C_re[l].transpose(0, 2, 1), (1, 1, S5_CHUNK))
    ct_im = jnp.tile(s5_C_im[l].transpose(0, 2, 1), (1, 1, S5_CHUNK))
    wt, wx, wc, apw = _s5_weights(prow_re, prow_im, pcol_re, pcol_im, z_re[:, None, :], z_im[:, None, :],
                                  bt_re, bt_im, ct_re, ct_im, 8)
    d_gp = s5_D[l].reshape(g, 1, p)
    d_flat = jnp.tile(d_gp, (1, 1, S5_CHUNK))
    a16 = jnp.concatenate([prow_re[:, 4:5, :], prow_im[:, 4:5, :]], axis=2)
    a1 = jnp.concatenate([prow_re[:, 0:1, :], prow_im[:, 0:1, :]], axis=2)

    nlo = seq // S5_CHUNK // 2
    u_p = proj_p[:, OFF_U:OFF_U + S5_WIDTH].reshape(bp, 2, nlo, S5_CHUNK, g, p)
    u_flat = u_p.transpose(4, 2, 1, 0, 3, 5).reshape(g, nlo * 2 * bp, S5_FLAT)
    z_flat, hend = _s5_prompt(u_flat, wt, wx, wc, apw, a16, d_flat, bp, 4)
    z_p = z_flat.reshape(g, nlo, 2, bp, S5_CHUNK, p).transpose(3, 2, 1, 4, 0, 5).reshape(bp * seq, S5_WIDTH)
    s5re_p = hend[:, :, :n].transpose(1, 0, 2)
    s5im_p = hend[:, :, n:].transpose(1, 0, 2)

    bstack = wx[:, S5_FLAT - p:, :].reshape(g * p, 2 * n)
    gbs = 128 // p
    z_s, s5re_s, s5im_s = _s5_sample(proj_s, bstack, ct_re[:, :, :gbs * p], ct_im[:, :, :gbs * p], a1,
                                     state_s5_re[l].transpose(1, 0, 2), state_s5_im[l].transpose(1, 0, 2),
                                     s5_D[l][None, :], gbs)

    y_p = _dense_tail(xp, proj_p, oa_p, z_p, w, 1024)
    y_s = _dense_tail(xs, proj_s, oa_s, z_s, w, nb)

    return (y_p.reshape(bp, seq, d), y_s.reshape(nb, 1, d),
            sh_p[None], s5re_p[None], s5im_p[None],
            sh_s[None], s5re_s.transpose(1, 0, 2)[None], s5im_s.transpose(1, 0, 2)[None])
```

```python
import functools
import math

import jax
import jax.numpy as jnp
from jax import lax
from jax.experimental import pallas as pl
from jax.experimental.pallas import tpu as pltpu

F32 = jnp.float32
BF16 = jnp.bfloat16

D_MODEL = 2048
HGRN_WIDTH = 1024
HEAD_DIM = 128
HEADS = 8
S5_WIDTH = 1024
S5_GROUPS = 64
S5_GROUP = 16
S5_STATE = 64
FFN_HIDDEN = 5632
IN_PROJ_WIDTH = 4 * HGRN_WIDTH + S5_WIDTH + 2 * D_MODEL
RMS_EPS = 1e-6

OFF_Q, OFF_F, OFF_I, OFF_OG = 0, HGRN_WIDTH, 2 * HGRN_WIDTH, 3 * HGRN_WIDTH
OFF_U = 4 * HGRN_WIDTH
OFF_GA = OFF_U + S5_WIDTH
OFF_GB = OFF_GA + D_MODEL

HGRN_SUB = 16
HGRN_CHUNK = 128
S5_CHUNK = 16
S5_FLAT = S5_CHUNK * S5_GROUP
N_POW = 10

V7X_VMEM_BYTES = 64 * 1024 * 1024
VMEM_LIMIT_CAP = V7X_VMEM_BYTES - 6 * 1024 * 1024


def _params(est_bytes):
    limit = min(max(int(est_bytes * 1.25) + (4 << 20), 32 << 20), VMEM_LIMIT_CAP)
    return pltpu.CompilerParams(vmem_limit_bytes=limit)


def _bdot(a, b):
    return jnp.dot(a, b, preferred_element_type=F32)


def _rms_rows(x, g):
    ms = jnp.mean(x * x, axis=-1, keepdims=True)
    return x * lax.rsqrt(ms + RMS_EPS) * g


def _inproj_kernel(x_ref, g_ref, w_ref, o_ref, h_scr):
    @pl.when(pl.program_id(1) == 0)
    def _():
        h_scr[...] = _rms_rows(x_ref[...], g_ref[...]).astype(BF16)

    o_ref[...] = _bdot(h_scr[...], w_ref[...].astype(BF16)).astype(BF16)


def _inproj(x, g, w, tm, tn):
    m, k = x.shape
    n = w.shape[1]
    est = 2 * tm * k * 4 + 2 * k * tn * 4 + 2 * tm * tn * 2 + tm * k * 2 + k * tn * 2 + tm * tn * 4
    return pl.pallas_call(
        _inproj_kernel,
        grid=(m // tm, n // tn),
        in_specs=[pl.BlockSpec((tm, k), lambda i, j: (i, 0)),
                  pl.BlockSpec((1, k), lambda i, j: (0, 0)),
                  pl.BlockSpec((k, tn), lambda i, j: (0, j))],
        out_specs=pl.BlockSpec((tm, tn), lambda i, j: (i, j)),
        out_shape=jax.ShapeDtypeStruct((m, n), BF16),
        scratch_shapes=[pltpu.VMEM((tm, k), BF16)],
        compiler_params=_params(est),
        name="inproj",
    )(x, g, w)


def _glu_kernel(z_ref, wa_ref, wb_ref, o_ref):
    z = z_ref[...]
    a = _bdot(z, wa_ref[...].astype(BF16))
    b = _bdot(z, wb_ref[...].astype(BF16))
    o_ref[...] = (a * jax.nn.sigmoid(b)).astype(BF16)


def _glu(z, w, tm, tn):
    m, k = z.shape
    n = w.shape[1] // 2
    nj = n // tn
    est = 2 * tm * k * 2 + 4 * k * tn * 4 + 2 * tm * tn * 2 + 2 * k * tn * 2 + 3 * tm * tn * 4
    return pl.pallas_call(
        _glu_kernel,
        grid=(m // tm, nj),
        in_specs=[pl.BlockSpec((tm, k), lambda i, j: (i, 0)),
                  pl.BlockSpec((k, tn), lambda i, j: (0, j)),
                  pl.BlockSpec((k, tn), lambda i, j: (0, j + nj))],
        out_specs=pl.BlockSpec((tm, tn), lambda i, j: (i, j)),
        out_shape=jax.ShapeDtypeStruct((m, n), BF16),
        compiler_params=_params(est),
        name="s5_glu",
    )(z, w, w)


def _merge_kernel(oa_ref, ob_ref, wa_ref, wb_ref, ga_ref, gb_ref, o_ref):
    a = _bdot(oa_ref[...], wa_ref[...].astype(BF16))
    b = _bdot(ob_ref[...], wb_ref[...].astype(BF16))
    ga, gb = ga_ref[...].astype(F32), gb_ref[...].astype(F32)
    o_ref[...] = (jax.nn.sigmoid(ga) * a + jax.nn.sigmoid(gb) * b).astype(BF16)


def _merge(o_a, o_b, w_a, w_b, proj, tm, tn):
    m, k = o_a.shape
    n = w_a.shape[1]
    ja, jb = OFF_GA // tn, OFF_GB // tn
    est = 4 * tm * k * 2 + 4 * k * tn * 4 + 4 * tm * tn * 4 + 2 * tm * tn * 2 + 2 * k * tn * 2 + 3 * tm * tn * 4
    return pl.pallas_call(
        _merge_kernel,
        grid=(m // tm, n // tn),
        in_specs=[pl.BlockSpec((tm, k), lambda i, j: (i, 0)),
                  pl.BlockSpec((tm, k), lambda i, j: (i, 0)),
                  pl.BlockSpec((k, tn), lambda i, j: (0, j)),
                  pl.BlockSpec((k, tn), lambda i, j: (0, j)),
                  pl.BlockSpec((tm, tn), lambda i, j: (i, j + ja)),
                  pl.BlockSpec((tm, tn), lambda i, j: (i, j + jb))],
        out_specs=pl.BlockSpec((tm, tn), lambda i, j: (i, j)),
        out_shape=jax.ShapeDtypeStruct((m, n), BF16),
        compiler_params=_params(est),
        name="gated_merge",
    )(o_a, o_b, w_a, w_b, proj, proj)


def _outproj_kernel(m_ref, w_ref, x_ref, o_ref):
    o_ref[...] = x_ref[...] + _bdot(m_ref[...], w_ref[...].astype(BF16))


def _outproj(merged, w, x, tm, tn):
    m, k = merged.shape
    n = w.shape[1]
    est = 2 * tm * k * 2 + 2 * k * tn * 4 + 4 * tm * tn * 4 + k * tn * 2 + tm * tn * 4
    return pl.pallas_call(
        _outproj_kernel,
        grid=(m // tm, n // tn),
        in_specs=[pl.BlockSpec((tm, k), lambda i, j: (i, 0)),
                  pl.BlockSpec((k, tn), lambda i, j: (0, j)),
                  pl.BlockSpec((tm, tn), lambda i, j: (i, j))],
        out_specs=pl.BlockSpec((tm, tn), lambda i, j: (i, j)),
        out_shape=jax.ShapeDtypeStruct((m, n), F32),
        compiler_params=_params(est),
        name="out_proj",
    )(merged, w, x)


def _ffn_up_kernel(x_ref, g_ref, wa_ref, wb_ref, o_ref, h_scr):
    @pl.when(pl.program_id(1) == 0)
    def _():
        h_scr[...] = _rms_rows(x_ref[...], g_ref[...]).astype(BF16)

    h = h_scr[...]
    a = _bdot(h, wa_ref[...].astype(BF16))
    b = _bdot(h, wb_ref[...].astype(BF16))
    o_ref[...] = (jax.nn.silu(a) * b).astype(BF16)


def _ffn_up(x, g, w, tm, tn):
    m, k = x.shape
    n = w.shape[1] // 2
    nj = n // tn
    est = 2 * tm * k * 4 + 4 * k * tn * 4 + 2 * tm * tn * 2 + tm * k * 2 + 2 * k * tn * 2 + 3 * tm * tn * 4
    return pl.pallas_call(
        _ffn_up_kernel,
        grid=(m // tm, nj),
        in_specs=[pl.BlockSpec((tm, k), lambda i, j: (i, 0)),
                  pl.BlockSpec((1, k), lambda i, j: (0, 0)),
                  pl.BlockSpec((k, tn), lambda i, j: (0, j)),
                  pl.BlockSpec((k, tn), lambda i, j: (0, j + nj))],
        out_specs=pl.BlockSpec((tm, tn), lambda i, j: (i, j)),
        out_shape=jax.ShapeDtypeStruct((m, n), BF16),
        scratch_shapes=[pltpu.VMEM((tm, k), BF16)],
        compiler_params=_params(est),
        name="ffn_up",
    )(x, g, w, w)


def _ffn_down_kernel(nj, tn, a_ref, w_ref, x_ref, g_ref, o_ref, acc_scr):
    j = pl.program_id(1)
    acc_scr[j] = x_ref[...] + _bdot(a_ref[...], w_ref[...])

    @pl.when(j == nj - 1)
    def _():
        ss = jnp.zeros((acc_scr.shape[1], 1), F32)
        for jj in range(nj):
            v = acc_scr[jj]
            ss = ss + jnp.sum(v * v, axis=-1, keepdims=True)
        rs = lax.rsqrt(ss / (nj * tn) + RMS_EPS)
        for jj in range(nj):
            o_ref[:, jj * tn:(jj + 1) * tn] = acc_scr[jj] * rs * g_ref[:, jj * tn:(jj + 1) * tn]


def _ffn_down(act, w_bf16, x, g, tm, tn):
    m, k = act.shape
    n = w_bf16.shape[1]
    nj = n // tn
    est = 2 * tm * k * 2 + 2 * k * tn * 2 + 2 * tm * tn * 4 + 2 * tm * n * 4 + tm * n * 4 + 2 * tm * tn * 4
    return pl.pallas_call(
        functools.partial(_ffn_down_kernel, nj, tn),
        grid=(m // tm, nj),
        in_specs=[pl.BlockSpec((tm, k), lambda i, j: (i, 0)),
                  pl.BlockSpec((k, tn), lambda i, j: (0, j)),
                  pl.BlockSpec((tm, tn), lambda i, j: (i, j)),
                  pl.BlockSpec((1, n), lambda i, j: (0, 0))],
        out_specs=pl.BlockSpec((tm, n), lambda i, j: (i, 0)),
        out_shape=jax.ShapeDtypeStruct((m, n), F32),
        scratch_shapes=[pltpu.VMEM((nj, tm, tn), F32)],
        compiler_params=_params(est),
        name="ffn_down_final_norm",
    )(act, w_bf16, x, g)


def _lower_bound_kernel(lb_ref, o_ref):
    x = lb_ref[...]
    mx = jnp.max(x, axis=0, keepdims=True)
    e = jnp.exp(x - mx)
    o_ref[...] = e[0:1, :] / jnp.sum(e, axis=0, keepdims=True)


def _lower_bound(lb_logits):
    return pl.pallas_call(
        _lower_bound_kernel,
        out_shape=jax.ShapeDtypeStruct((1, HGRN_WIDTH), F32),
        name="hgrn_lower_bound",
    )(lb_logits)


def _split3(x):
    hi = x.astype(BF16)
    r = x - hi.astype(F32)
    mid = r.astype(BF16)
    lo = (r - mid.astype(F32)).astype(BF16)
    return hi, mid, lo


def _rows_ref(b, block, pick):
    parts = []
    for s0 in range(0, HGRN_CHUNK, block):
        parts.append(jnp.broadcast_to(b[s0 + pick:s0 + pick + 1, :], (block, b.shape[1])))
    return parts[0] if len(parts) == 1 else jnp.concatenate(parts, axis=0)


def _hgrn_prompt_kernel(n_chunks, q_ref, f_ref, i_ref, og_ref, lb_ref, gn_ref, oa_ref, s_ref, st_scr):
    t = pl.program_id(1)

    @pl.when(t == 0)
    def _():
        st_scr[...] = jnp.zeros(st_scr.shape, F32)

    c_ = HGRN_CHUNK
    row = lax.broadcasted_iota(jnp.int32, (c_, c_), 0)
    col = lax.broadcasted_iota(jnp.int32, (c_, c_), 1)
    tri = (col <= row).astype(BF16)
    m_diag = (row // HGRN_SUB == col // HGRN_SUB) & (col <= row)
    levels = (128, 64, 32)
    m_lvl = [(row // bs == col // bs) & (row % bs >= bs // 2) & (col % bs < bs // 2) for bs in levels]
    rcol = lax.broadcasted_iota(jnp.int32, (c_, 1), 0)
    second = [(rcol % bs) >= bs // 2 for bs in levels]

    lb = lb_ref[...]
    gn = gn_ref[...]

    def chunk(c, carry):
        r0 = pl.multiple_of(c * c_, c_)
        rows = pl.ds(r0, c_)
        f = lb + (1.0 - lb) * jax.nn.sigmoid(f_ref[rows, :].astype(F32))
        g = jnp.log(f)
        k_all = 1.0 - f
        g_hi, g_mid, g_lo = _split3(g)
        b = _bdot(tri, g_hi) + _bdot(tri, g_mid) + _bdot(tri, g_lo)
        b_last = b[c_ - 1:c_, :]
        e_in = jnp.exp(b)
        e_upd = jnp.exp(b_last - b)
        dec_all = jnp.exp(b_last)
        b_mid = _rows_ref(b, HGRN_SUB, HGRN_SUB // 2)
        e_dq = jnp.exp(b - b_mid)
        e_dk = jnp.exp(b_mid - b)
        e_lvl = []
        for li, bs in enumerate(levels):
            d = b - _rows_ref(b, bs, bs // 2 - 1)
            e_lvl.append(jnp.exp(jnp.where(second[li], d, -d)))
        q_all = q_ref[rows, :].astype(F32)
        v_all = i_ref[rows, :].astype(F32)
        og = og_ref[rows, :].astype(F32)
        for h in range(HEADS):
            ls = slice(h * HEAD_DIM, (h + 1) * HEAD_DIM)
            q, k, v = q_all[:, ls], k_all[:, ls], v_all[:, ls]
            nt = (((1,), (1,)), ((), ()))
            sc = lax.dot_general((q * e_dq[:, ls]).astype(BF16), (k * e_dk[:, ls]).astype(BF16), nt,
                                 preferred_element_type=F32)
            scores = jnp.where(m_diag, sc, 0.0)
            for li in range(len(levels)):
                e = e_lvl[li][:, ls]
                sc = lax.dot_general((q * e).astype(BF16), (k * e).astype(BF16), nt, preferred_element_type=F32)
                scores = jnp.where(m_lvl[li], sc, scores)
            v_b = v.astype(BF16)
            st = st_scr[h]
            o = _bdot(scores.astype(BF16), v_b)
            o = o + lax.dot_general((q * e_in[:, ls]).astype(BF16), st.astype(BF16), nt, preferred_element_type=F32)
            upd = lax.dot_general(v_b, (k * e_upd[:, ls]).astype(BF16), (((0,), (0,)), ((), ())),
                                  preferred_element_type=F32)
            st_scr[h] = dec_all[:, ls] * st + upd
            o = o * lax.rsqrt(jnp.mean(o * o, axis=-1, keepdims=True) + RMS_EPS) * gn[:, ls]
            oa_ref[rows, ls] = (o * jax.nn.silu(og[:, ls])).astype(BF16)
        return carry

    lax.fori_loop(0, n_chunks, chunk, 0)

    @pl.when(t == pl.num_programs(1) - 1)
    def _():
        for h in range(HEADS):
            s_ref[0, h] = st_scr[h].T


def _hgrn_prompt(proj, lb, gn, batch, seq, tt):
    m = batch * seq
    w = HGRN_WIDTH
    nt = seq // tt
    blk = lambda off: pl.BlockSpec((tt, w), lambda b, t, off=off: (b * nt + t, off // w))
    est = 8 * tt * w * 4 + 2 * tt * w * 2 + 3 * HEADS * HEAD_DIM * HEAD_DIM * 4 + 24 * HGRN_CHUNK * w * 4
    return pl.pallas_call(
        functools.partial(_hgrn_prompt_kernel, tt // HGRN_CHUNK),
        grid=(batch, nt),
        in_specs=[blk(OFF_Q), blk(OFF_F), blk(OFF_I), blk(OFF_OG),
                  pl.BlockSpec((1, w), lambda b, t: (0, 0)),
                  pl.BlockSpec((1, w), lambda b, t: (0, 0))],
        out_specs=[pl.BlockSpec((tt, w), lambda b, t: (b * nt + t, 0)),
                   pl.BlockSpec((1, HEADS, HEAD_DIM, HEAD_DIM), lambda b, t: (b, 0, 0, 0))],
        out_shape=[jax.ShapeDtypeStruct((m, w), BF16),
                   jax.ShapeDtypeStruct((batch, HEADS, HEAD_DIM, HEAD_DIM), F32)],
        scratch_shapes=[pltpu.VMEM((HEADS, HEAD_DIM, HEAD_DIM), F32)],
        compiler_params=_params(est),
        name="hgrn2_prompt",
    )(proj, proj, proj, proj, lb, gn)


def _hgrn_sample_kernel(nb, qt_ref, ft_ref, v_ref, og_ref, lbc_ref, gn_ref, s0_ref, oa_ref, s_ref, o_scr):
    lbc = lbc_ref[...]
    f_t = lbc + (1.0 - lbc) * jax.nn.sigmoid(ft_ref[...])
    k_t = 1.0 - f_t
    q_t = qt_ref[...]
    v = v_ref[...].astype(F32)
    for b in range(nb):
        s_new = f_t[:, b:b + 1] * s0_ref[b, 0] + k_t[:, b:b + 1] * v[b:b + 1, :]
        s_ref[b, 0] = s_new
        o_scr[b:b + 1, :] = jnp.sum(q_t[:, b:b + 1] * s_new, axis=0, keepdims=True)
    o = o_scr[...]
    o = o * lax.rsqrt(jnp.mean(o * o, axis=-1, keepdims=True) + RMS_EPS) * gn_ref[...]
    oa_ref[...] = (o * jax.nn.silu(og_ref[...].astype(F32))).astype(BF16)


def _hgrn_sample(q_t, f_t, proj, lb_col, gn, s0):
    nb = s0.shape[0]
    hd = HEAD_DIM
    est = 4 * nb * hd * hd * 4 + 16 * nb * hd * 4
    return pl.pallas_call(
        functools.partial(_hgrn_sample_kernel, nb),
        grid=(HEADS,),
        in_specs=[pl.BlockSpec((hd, nb), lambda h: (h, 0)),
                  pl.BlockSpec((hd, nb), lambda h: (h, 0)),
                  pl.BlockSpec((nb, hd), lambda h: (0, OFF_I // hd + h)),
                  pl.BlockSpec((nb, hd), lambda h: (0, OFF_OG // hd + h)),
                  pl.BlockSpec((hd, 1), lambda h: (h, 0)),
                  pl.BlockSpec((1, hd), lambda h: (0, h)),
                  pl.BlockSpec((nb, 1, hd, hd), lambda h: (0, h, 0, 0))],
        out_specs=[pl.BlockSpec((nb, hd), lambda h: (0, h)),
                   pl.BlockSpec((nb, 1, hd, hd), lambda h: (0, h, 0, 0))],
        out_shape=[jax.ShapeDtypeStruct((nb, HGRN_WIDTH), BF16),
                   jax.ShapeDtypeStruct(s0.shape, F32)],
        scratch_shapes=[pltpu.VMEM((nb, hd), F32)],
        compiler_params=_params(est),
        name="hgrn2_sample",
    )(q_t, f_t, proj, proj, lb_col, gn, s0)


def _cmul(ar, ai, br, bi):
    return ar * br - ai * bi, ar * bi + ai * br


def _gelu_exact(x):
    return 0.5 * x * (1.0 + lax.erf(x * math.sqrt(0.5)))


def _s5_discretise_kernel(lre_ref, lim_ref, ldt_ref, pre_ref, pim_ref, zre_ref, zim_ref):
    lam_re, lam_im = lre_ref[...], lim_ref[...]
    dt = jnp.exp(ldt_ref[...])
    mag = jnp.exp(lam_re * dt)
    ab_re = mag * jnp.cos(lam_im * dt)
    ab_im = mag * jnp.sin(lam_im * dt)
    den = lam_re * lam_re + lam_im * lam_im
    nr, ni = ab_re - 1.0, ab_im
    zre_ref[...] = (nr * lam_re + ni * lam_im) / den
    zim_ref[...] = (ni * lam_re - nr * lam_im) / den
    pr, pi = ab_re, ab_im
    for kk in range(N_POW):
        pre_ref[kk] = pr
        pim_ref[kk] = pi
        pr, pi = _cmul(pr, pi, pr, pi)


def _s5_discretise(lam_re, lam_im, log_dt):
    g, n = lam_re.shape
    return pl.pallas_call(
        _s5_discretise_kernel,
        out_shape=[jax.ShapeDtypeStruct((N_POW, g, n), F32), jax.ShapeDtypeStruct((N_POW, g, n), F32),
                   jax.ShapeDtypeStruct((g, n), F32), jax.ShapeDtypeStruct((g, n), F32)],
        name="s5_discretise",
    )(lam_re, lam_im, log_dt)


def _s5_weights_kernel(gb, prow_re, prow_im, pcol_re, pcol_im, zre_ref, zim_ref, btre_ref, btim_ref,
                       ctre_ref, ctim_ref, wt_ref, wx_ref, wc_ref, apw_ref):
    fl = S5_FLAT
    srow = lax.broadcasted_iota(jnp.int32, (fl, 1), 0) // S5_GROUP
    tlan = lax.broadcasted_iota(jnp.int32, (1, fl), 1) // S5_GROUP
    jrow = lax.broadcasted_iota(jnp.int32, (S5_STATE, 1), 0)
    for gi in range(gb):
        zr, zi = zre_ref[gi], zim_ref[gi]
        bbr, bbi = _cmul(zr, zi, btre_ref[gi], btim_ref[gi])
        xr = jnp.concatenate([bbr] * S5_CHUNK, axis=0)
        xi = jnp.concatenate([bbi] * S5_CHUNK, axis=0)
        e_x = (S5_CHUNK - 1) - srow
        for kk in range(4):
            ar, ai = prow_re[gi, kk:kk + 1, :], prow_im[gi, kk:kk + 1, :]
            yr, yi = _cmul(xr, xi, ar, ai)
            hit = ((e_x >> kk) & 1) == 1
            xr, xi = jnp.where(hit, yr, xr), jnp.where(hit, yi, xi)
        wx_ref[gi] = jnp.concatenate([xr, xi], axis=1).astype(BF16)
        cr, ci = ctre_ref[gi], ctim_ref[gi]
        for kk in range(4):
            ar, ai = pcol_re[gi, :, kk:kk + 1], pcol_im[gi, :, kk:kk + 1]
            yr, yi = _cmul(cr, ci, ar, ai)
            hit = ((tlan >> kk) & 1) == 1
            cr, ci = jnp.where(hit, yr, cr), jnp.where(hit, yi, ci)
        g0 = jnp.concatenate([cr, -ci], axis=0)
        c1r, c1i = _cmul(cr, ci, pcol_re[gi, :, 0:1], pcol_im[gi, :, 0:1])
        wc_ref[gi] = jnp.concatenate([c1r, -c1i], axis=0).astype(BF16)
        kflat = jnp.dot(jnp.concatenate([bbr, bbi], axis=1), g0, preferred_element_type=F32,
                        precision=lax.Precision.HIGHEST)
        pieces = [kflat]
        for s in range(1, S5_CHUNK):
            pieces.append(jnp.concatenate([jnp.zeros((S5_GROUP, s * S5_GROUP), F32),
                                           kflat[:, :fl - s * S5_GROUP]], axis=1))
        wt_ref[gi] = jnp.concatenate(pieces, axis=0).astype(BF16)
        pr = jnp.ones((S5_STATE, S5_STATE), F32)
        pi = jnp.zeros((S5_STATE, S5_STATE), F32)
        for kk in range(6):
            ar, ai = prow_re[gi, 4 + kk:5 + kk, :], prow_im[gi, 4 + kk:5 + kk, :]
            yr, yi = _cmul(pr, pi, ar, ai)
            hit = ((jrow >> kk) & 1) == 1
            pr, pi = jnp.where(hit, yr, pr), jnp.where(hit, yi, pi)
        apw_ref[gi] = jnp.concatenate([pr, pi], axis=1)


def _s5_weights(prow_re, prow_im, pcol_re, pcol_im, z_re, z_im, bt_re, bt_im, ct_re, ct_im, gb):
    g = S5_GROUPS
    n, p, fl = S5_STATE, S5_GROUP, S5_FLAT
    b3 = lambda a, b: pl.BlockSpec((gb, a, b), lambda i: (i, 0, 0))
    return pl.pallas_call(
        functools.partial(_s5_weights_kernel, gb),
        grid=(g // gb,),
        in_specs=[b3(N_POW, n), b3(N_POW, n), b3(n, N_POW), b3(n, N_POW), b3(1, n), b3(1, n),
                  b3(p, n), b3(p, n), b3(n, fl), b3(n, fl)],
        out_specs=[b3(fl, fl), b3(fl, 2 * n), b3(2 * n, fl), b3(n, 2 * n)],
        out_shape=[jax.ShapeDtypeStruct((g, fl, fl), BF16), jax.ShapeDtypeStruct((g, fl, 2 * n), BF16),
                   jax.ShapeDtypeStruct((g, 2 * n, fl), BF16), jax.ShapeDtypeStruct((g, n, 2 * n), F32)],
        name="s5_chunk_weights",
    )(prow_re, prow_im, pcol_re, pcol_im, z_re, z_im, bt_re, bt_im, ct_re, ct_im)


LANE_GROUPS = 128 // S5_GROUP
RELAYOUT_ROWS = 64


def _block_transpose8(xs):
    lane_blk = lax.broadcasted_iota(jnp.int32, (1, 128), 1) // S5_GROUP
    xs = list(xs)
    for d in (4, 2, 1):
        upper = (lane_blk & d) != 0
        nxt = list(xs)
        for a in range(8):
            if a & d:
                continue
            lo, hi = xs[a], xs[a + d]
            nxt[a] = jnp.where(upper, pltpu.roll(hi, S5_GROUP * d, axis=1), lo)
            nxt[a + d] = jnp.where(upper, hi, pltpu.roll(lo, 128 - S5_GROUP * d, axis=1))
        xs = nxt
    return xs


def _s5_prompt_kernel(nseq, u_ref, wt_ref, wx_ref, wc_ref, apw_ref, a16_ref, d_ref, z_ref, hend_ref,
                      tok_scr, flat_scr, x_scr, h_scr):
    n = S5_STATE
    gb = LANE_GROUPS
    rows = u_ref.shape[0] // S5_CHUNK
    slab = 2 * nseq
    nlo = rows // slab
    rb = RELAYOUT_ROWS
    halves = S5_FLAT // 128

    tok_scr[...] = u_ref[...].astype(F32)

    def to_flat(r, carry):
        r0 = pl.multiple_of(r * rb, rb)
        for sh in range(halves):
            xs = [tok_scr[pl.ds(r0 * S5_CHUNK + 8 * sh + s, rb, stride=S5_CHUNK), :] for s in range(8)]
            ys = _block_transpose8(xs)
            for gi in range(gb):
                flat_scr[gi, pl.ds(r0, rb), sh * 128:(sh + 1) * 128] = ys[gi]
        return carry

    lax.fori_loop(0, rows // rb, to_flat, 0)

    odd = (lax.broadcasted_iota(jnp.int32, (slab, 1), 0) % 2) == 1
    for gi in range(gb):
        u = flat_scr[gi]
        u_b = u.astype(BF16)
        x_scr[gi] = _bdot(u_b, wx_ref[gi])
        ar, ai = a16_ref[gi, :, :n], a16_ref[gi, :, n:]
        hr = jnp.zeros((slab, n), F32)
        hi = jnp.zeros((slab, n), F32)
        for j in range(nlo):
            rs = pl.ds(j, slab, stride=nlo)
            h_scr[gi, rs, :] = jnp.concatenate([hr, hi], axis=1)
            xj = x_scr[gi, rs, :]
            nr, ni = _cmul(hr, hi, ar, ai)
            hr = nr + xj[:, :n]
            hi = ni + xj[:, n:]
        mr = jnp.where(odd, pltpu.roll(hr, 1, axis=0), 0.0)
        mi = jnp.where(odd, pltpu.roll(hi, 1, axis=0), 0.0)
        pr, pi = apw_ref[gi, :, :n], apw_ref[gi, :, n:]
        for k in range(1, slab, 2):
            cr, ci = _cmul(pr, pi, mr[k:k + 1], mi[k:k + 1])
            h_scr[gi, k * nlo:(k + 1) * nlo, :n] += cr
            h_scr[gi, k * nlo:(k + 1) * nlo, n:] += ci
        lr, li = _cmul(pr[nlo - 1:nlo], pi[nlo - 1:nlo], ar, ai)
        er, ei = _cmul(mr, mi, lr, li)
        hend_ref[gi] = jnp.concatenate([hr + er, hi + ei], axis=1)
        y = _bdot(u_b, wt_ref[gi]) + _bdot(h_scr[gi].astype(BF16), wc_ref[gi]) + d_ref[gi] * u
        flat_scr[gi] = _gelu_exact(y)

    def to_tokens(r, carry):
        r0 = pl.multiple_of(r * rb, rb)
        for sh in range(halves):
            ys = [flat_scr[gi, pl.ds(r0, rb), sh * 128:(sh + 1) * 128] for gi in range(gb)]
            xs = _block_transpose8(ys)
            for s in range(8):
                tok_scr[pl.ds(r0 * S5_CHUNK + 8 * sh + s, rb, stride=S5_CHUNK), :] = xs[s]
        return carry

    lax.fori_loop(0, rows // rb, to_tokens, 0)
    z_ref[...] = tok_scr[...].astype(BF16)


def _s5_prompt(proj, wt, wx, wc, apw, a16, d_flat, nseq):
    m = proj.shape[0]
    g, n, fl, gb = S5_GROUPS, S5_STATE, S5_FLAT, LANE_GROUPS
    rows = m // S5_CHUNK
    b3 = lambda a, b: pl.BlockSpec((gb, a, b), lambda i: (i, 0, 0))
    est = (4 * m * 128 * 2 + m * 128 * 4 + gb * rows * (fl + 4 * n) * 4 + 6 * rows * fl * 4
           + 2 * gb * (fl * fl + 4 * n * fl) * 2)
    return pl.pallas_call(
        functools.partial(_s5_prompt_kernel, nseq),
        grid=(g // gb,),
        in_specs=[pl.BlockSpec((m, 128), lambda i: (0, OFF_U // 128 + i)),
                  b3(fl, fl), b3(fl, 2 * n), b3(2 * n, fl), b3(n, 2 * n), b3(1, 2 * n), b3(1, fl)],
        out_specs=[pl.BlockSpec((m, 128), lambda i: (0, i)), b3(2 * nseq, 2 * n)],
        out_shape=[jax.ShapeDtypeStruct((m, S5_WIDTH), BF16), jax.ShapeDtypeStruct((g, 2 * nseq, 2 * n), F32)],
        scratch_shapes=[pltpu.VMEM((m, 128), F32), pltpu.VMEM((gb, rows, fl), F32),
                        pltpu.VMEM((gb, rows, 2 * n), F32), pltpu.VMEM((gb, rows, 2 * n), F32)],
        compiler_params=_params(est),
        name="s5_prompt",
    )(proj, wt, wx, wc, apw, a16, d_flat)


def _s5_sample_kernel(gb, u_ref, bst_ref, ctre_ref, ctim_ref, a1_ref, hre_ref, him_ref, d_ref,
                      z_ref, ore_ref, oim_ref):
    n = S5_STATE
    u = u_ref[...].astype(F32)
    lane_g = lax.broadcasted_iota(jnp.int32, (1, gb * S5_GROUP), 1) // S5_GROUP
    bst = bst_ref[...]
    y = d_ref[...] * u
    for gi in range(gb):
        mine = lane_g == gi
        bu = _bdot(jnp.where(mine, u, 0.0).astype(BF16), bst)
        ar, ai = a1_ref[gi, :, :n], a1_ref[gi, :, n:]
        nr, ni = _cmul(hre_ref[gi], him_ref[gi], ar, ai)
        hr = nr + bu[:, :n]
        hi = ni + bu[:, n:]
        ore_ref[gi] = hr
        oim_ref[gi] = hi
        cmat = jnp.concatenate([jnp.where(mine, ctre_ref[gi], 0.0), jnp.where(mine, -ctim_ref[gi], 0.0)], axis=0)
        y = y + _bdot(jnp.concatenate([hr, hi], axis=1).astype(BF16), cmat.astype(BF16))
    z_ref[...] = _gelu_exact(y).astype(BF16)


def _s5_sample(proj, bstack, ct_re, ct_im, a1, h_re, h_im, d_row, gb):
    nb = proj.shape[0]
    g, n, p = S5_GROUPS, S5_STATE, S5_GROUP
    lanes = gb * p
    b3 = lambda a, b: pl.BlockSpec((gb, a, b), lambda i: (i, 0, 0))
    return pl.pallas_call(
        functools.partial(_s5_sample_kernel, gb),
        grid=(g // gb,),
        in_specs=[pl.BlockSpec((nb, lanes), lambda i: (0, OFF_U // lanes + i)),
                  pl.BlockSpec((lanes, 2 * n), lambda i: (i, 0)),
                  b3(n, lanes), b3(n, lanes), b3(1, 2 * n), b3(nb, n), b3(nb, n),
                  pl.BlockSpec((1, lanes), lambda i: (0, i))],
        out_specs=[pl.BlockSpec((nb, lanes), lambda i: (0, i)), b3(nb, n), b3(nb, n)],
        out_shape=[jax.ShapeDtypeStruct((nb, S5_WIDTH), BF16),
                   jax.ShapeDtypeStruct((g, nb, n), F32), jax.ShapeDtypeStruct((g, nb, n), F32)],
        name="s5_sample",
    )(proj, bstack, ct_re, ct_im, a1, h_re, h_im, d_row)


def _dense_tail(x, proj, o_a, z, w, tm):
    o_b = _glu(z, w['w_s5_glu'], tm, 512)
    merged = _merge(o_a, o_b, w['w_proj_a'], w['w_proj_b'], proj, tm, 512)
    x1 = _outproj(merged, w['w_out'], x, tm, 512)
    act = _ffn_up(x1, w['norm2_g'], w['w_ffn_up'], tm, 256)
    return _ffn_down(act, w['w_ffn_down_bf16'], x1, w['final_norm_g'], min(tm, 512), 512)


def kernel(x_prompt, x_sample, state_hgrn, state_s5_re, state_s5_im, lb_logits, norm1_g, w_in, hgrn_norm_g,
           s5_lam_re, s5_lam_im, s5_log_dt, s5_B_re, s5_B_im, s5_C_re, s5_C_im, s5_D, w_s5_glu, w_proj_a,
           w_proj_b, w_out, norm2_g, w_ffn_up, w_ffn_down, final_norm_g):
    l = 0
    bp, seq, d = x_prompt.shape
    nb = x_sample.shape[0]
    g, n, p = S5_GROUPS, S5_STATE, S5_GROUP
    w = {'w_s5_glu': w_s5_glu[l], 'w_proj_a': w_proj_a[l], 'w_proj_b': w_proj_b[l], 'w_out': w_out[l],
         'norm2_g': norm2_g[l][None, :], 'w_ffn_up': w_ffn_up[l], 'w_ffn_down_bf16': w_ffn_down[l].astype(BF16),
         'final_norm_g': final_norm_g[None, :]}
    g1 = norm1_g[l][None, :]
    gn = hgrn_norm_g[l][None, :]

    xp = x_prompt.reshape(bp * seq, d)
    xs = x_sample.reshape(nb, d)
    proj_p = _inproj(xp, g1, w_in[l], 1024, 512)
    proj_s = _inproj(xs, g1, w_in[l], nb, 512)

    lb = _lower_bound(lb_logits)
    oa_p, sh_p = _hgrn_prompt(proj_p, lb, gn, bp, seq, 256)
    q_t = proj_s[:, OFF_Q:OFF_Q + HGRN_WIDTH].astype(F32).T
    f_t = proj_s[:, OFF_F:OFF_F + HGRN_WIDTH].astype(F32).T
    oa_s, sh_s = _hgrn_sample(q_t, f_t, proj_s, lb.reshape(HGRN_WIDTH, 1), gn, state_hgrn[l])

    pw_re, pw_im, z_re, z_im = _s5_discretise(s5_lam_re[l], s5_lam_im[l], s5_log_dt[l][:, None])
    prow_re, prow_im = pw_re.transpose(1, 0, 2), pw_im.transpose(1, 0, 2)
    pcol_re, pcol_im = pw_re.transpose(1, 2, 0), pw_im.transpose(1, 2, 0)
    bt_re, bt_im = s5_B_re[l].transpose(0, 2, 1), s5_B_im[l].transpose(0, 2, 1)
    ct_re = jnp.tile(s5_C_re[l].transpose(0, 2, 1), (1, 1, S5_CHUNK))
    ct_im = jnp.tile(s5_C_im[l].transpose(0, 2, 1), (1, 1, S5_CHUNK))
    wt, wx, wc, apw = _s5_weights(prow_re, prow_im, pcol_re, pcol_im, z_re[:, None, :], z_im[:, None, :],
                                  bt_re, bt_im, ct_re, ct_im, 8)
    d_gp = s5_D[l].reshape(g, 1, p)
    d_flat = jnp.tile(d_gp, (1, 1, S5_CHUNK))
    a16 = jnp.concatenate([prow_re[:, 4:5, :], prow_im[:, 4:5, :]], axis=2)
    a1 = jnp.concatenate([prow_re[:, 0:1, :], prow_im[:, 0:1, :]], axis=2)

    z_p, hend = _s5_prompt(proj_p, wt, wx, wc, apw, a16, d_flat, bp)
    hend = hend[:, 1::2, :]
    s5re_p = hend[:, :, :n].transpose(1, 0, 2)
    s5im_p = hend[:, :, n:].transpose(1, 0, 2)

    bstack = wx[:, S5_FLAT - p:, :].reshape(g * p, 2 * n)
    gbs = 128 // p
    z_s, s5re_s, s5im_s = _s5_sample(proj_s, bstack, ct_re[:, :, :gbs * p], ct_im[:, :, :gbs * p], a1,
                                     state_s5_re[l].transpose(1, 0, 2), state_s5_im[l].transpose(1, 0, 2),
                                     s5_D[l][None, :], gbs)

    y_p = _dense_tail(xp, proj_p, oa_p, z_p, w, 1024)
    y_s = _dense_tail(xs, proj_s, oa_s, z_s, w, nb)

    return (y_p.reshape(bp, seq, d), y_s.reshape(nb, 1, d),
            sh_p[None], s5re_p[None], s5im_p[None],
            sh_s[None], s5re_s.transpose(1, 0, 2)[None], s5im_s.transpose(1, 0, 2)[None])
```

```python
import functools
import math

import jax
import jax.numpy as jnp
from jax import lax
from jax.experimental import pallas as pl
from jax.experimental.pallas import tpu as pltpu

F32 = jnp.float32
BF16 = jnp.bfloat16

D_MODEL = 2048
HGRN_WIDTH = 1024
HEAD_DIM = 128
HEADS = 8
S5_WIDTH = 1024
S5_GROUPS = 64
S5_GROUP = 16
S5_STATE = 64
FFN_HIDDEN = 5632
IN_PROJ_WIDTH = 4 * HGRN_WIDTH + S5_WIDTH + 2 * D_MODEL
RMS_EPS = 1e-6

OFF_Q, OFF_F, OFF_I, OFF_OG = 0, HGRN_WIDTH, 2 * HGRN_WIDTH, 3 * HGRN_WIDTH
OFF_U = 4 * HGRN_WIDTH
OFF_GA = OFF_U + S5_WIDTH
OFF_GB = OFF_GA + D_MODEL

HGRN_SUB = 16
HGRN_CHUNK = 128
S5_CHUNK = 16
S5_FLAT = S5_CHUNK * S5_GROUP
N_POW = 10

V7X_VMEM_BYTES = 64 * 1024 * 1024
VMEM_LIMIT_CAP = V7X_VMEM_BYTES - 6 * 1024 * 1024


def _params(est_bytes):
    limit = min(max(int(est_bytes * 1.25) + (4 << 20), 32 << 20), VMEM_LIMIT_CAP)
    return pltpu.CompilerParams(vmem_limit_bytes=limit)


def _bdot(a, b):
    return jnp.dot(a, b, preferred_element_type=F32)


def _rms_rows(x, g):
    ms = jnp.mean(x * x, axis=-1, keepdims=True)
    return x * lax.rsqrt(ms + RMS_EPS) * g


def _inproj_kernel(x_ref, g_ref, w_ref, o_ref, h_scr):
    @pl.when(pl.program_id(1) == 0)
    def _():
        h_scr[...] = _rms_rows(x_ref[...], g_ref[...]).astype(BF16)

    o_ref[...] = _bdot(h_scr[...], w_ref[...].astype(BF16)).astype(BF16)


def _inproj(x, g, w, tm, tn):
    m, k = x.shape
    n = w.shape[1]
    single = pl.Buffered(1) if tm >= 2048 else None
    est = ((1 if single else 2) * tm * k * 4 + 2 * k * tn * 4 + 2 * tm * tn * 2 + tm * k * 2 + k * tn * 2
           + tm * tn * 6)
    return pl.pallas_call(
        _inproj_kernel,
        grid=(m // tm, n // tn),
        in_specs=[pl.BlockSpec((tm, k), lambda i, j: (i, 0), pipeline_mode=single),
                  pl.BlockSpec((1, k), lambda i, j: (0, 0)),
                  pl.BlockSpec((k, tn), lambda i, j: (0, j))],
        out_specs=pl.BlockSpec((tm, tn), lambda i, j: (i, j)),
        out_shape=jax.ShapeDtypeStruct((m, n), BF16),
        scratch_shapes=[pltpu.VMEM((tm, k), BF16)],
        compiler_params=_params(est),
        name="inproj",
    )(x, g, w)


def _glu_kernel(z_ref, wa_ref, wb_ref, o_ref):
    z = z_ref[...]
    a = _bdot(z, wa_ref[...].astype(BF16))
    b = _bdot(z, wb_ref[...].astype(BF16))
    o_ref[...] = (a * jax.nn.sigmoid(b)).astype(BF16)


def _glu(z, w, tm, tn):
    m, k = z.shape
    n = w.shape[1] // 2
    nj = n // tn
    est = 2 * tm * k * 2 + 4 * k * tn * 4 + 2 * tm * tn * 2 + 2 * k * tn * 2 + 3 * tm * tn * 4
    return pl.pallas_call(
        _glu_kernel,
        grid=(m // tm, nj),
        in_specs=[pl.BlockSpec((tm, k), lambda i, j: (i, 0)),
                  pl.BlockSpec((k, tn), lambda i, j: (0, j)),
                  pl.BlockSpec((k, tn), lambda i, j: (0, j + nj))],
        out_specs=pl.BlockSpec((tm, tn), lambda i, j: (i, j)),
        out_shape=jax.ShapeDtypeStruct((m, n), BF16),
        compiler_params=_params(est),
        name="s5_glu",
    )(z, w, w)


def _merge_kernel(oa_ref, ob_ref, wa_ref, wb_ref, ga_ref, gb_ref, o_ref):
    a = _bdot(oa_ref[...], wa_ref[...].astype(BF16))
    b = _bdot(ob_ref[...], wb_ref[...].astype(BF16))
    ga, gb = ga_ref[...].astype(F32), gb_ref[...].astype(F32)
    o_ref[...] = (jax.nn.sigmoid(ga) * a + jax.nn.sigmoid(gb) * b).astype(BF16)


def _merge(o_a, o_b, w_a, w_b, proj, tm, tn):
    m, k = o_a.shape
    n = w_a.shape[1]
    ja, jb = OFF_GA // tn, OFF_GB // tn
    est = 4 * tm * k * 2 + 4 * k * tn * 4 + 4 * tm * tn * 4 + 2 * tm * tn * 2 + 2 * k * tn * 2 + 3 * tm * tn * 4
    return pl.pallas_call(
        _merge_kernel,
        grid=(m // tm, n // tn),
        in_specs=[pl.BlockSpec((tm, k), lambda i, j: (i, 0)),
                  pl.BlockSpec((tm, k), lambda i, j: (i, 0)),
                  pl.BlockSpec((k, tn), lambda i, j: (0, j)),
                  pl.BlockSpec((k, tn), lambda i, j: (0, j)),
                  pl.BlockSpec((tm, tn), lambda i, j: (i, j + ja)),
                  pl.BlockSpec((tm, tn), lambda i, j: (i, j + jb))],
        out_specs=pl.BlockSpec((tm, tn), lambda i, j: (i, j)),
        out_shape=jax.ShapeDtypeStruct((m, n), BF16),
        compiler_params=_params(est),
        name="gated_merge",
    )(o_a, o_b, w_a, w_b, proj, proj)


def _outproj_kernel(m_ref, w_ref, x_ref, o_ref):
    o_ref[...] = x_ref[...] + _bdot(m_ref[...], w_ref[...].astype(BF16))


def _outproj(merged, w, x, tm, tn):
    m, k = merged.shape
    n = w.shape[1]
    est = 2 * tm * k * 2 + 2 * k * tn * 4 + 4 * tm * tn * 4 + k * tn * 2 + tm * tn * 4
    return pl.pallas_call(
        _outproj_kernel,
        grid=(m // tm, n // tn),
        in_specs=[pl.BlockSpec((tm, k), lambda i, j: (i, 0)),
                  pl.BlockSpec((k, tn), lambda i, j: (0, j)),
                  pl.BlockSpec((tm, tn), lambda i, j: (i, j))],
        out_specs=pl.BlockSpec((tm, tn), lambda i, j: (i, j)),
        out_shape=jax.ShapeDtypeStruct((m, n), F32),
        compiler_params=_params(est),
        name="out_proj",
    )(merged, w, x)


def _ffn_up_kernel(x_ref, g_ref, wa_ref, wb_ref, o_ref, h_scr):
    @pl.when(pl.program_id(1) == 0)
    def _():
        h_scr[...] = _rms_rows(x_ref[...], g_ref[...]).astype(BF16)

    h = h_scr[...]
    a = _bdot(h, wa_ref[...].astype(BF16))
    b = _bdot(h, wb_ref[...].astype(BF16))
    o_ref[...] = (jax.nn.silu(a) * b).astype(BF16)


def _ffn_up(x, g, w, tm, tn):
    m, k = x.shape
    n = w.shape[1] // 2
    nj = n // tn
    single = pl.Buffered(1) if tm >= 2048 else None
    est = ((1 if single else 2) * tm * k * 4 + 4 * k * tn * 4 + 2 * tm * tn * 2 + tm * k * 2 + 2 * k * tn * 2
           + 3 * tm * tn * 4)
    return pl.pallas_call(
        _ffn_up_kernel,
        grid=(m // tm, nj),
        in_specs=[pl.BlockSpec((tm, k), lambda i, j: (i, 0), pipeline_mode=single),
                  pl.BlockSpec((1, k), lambda i, j: (0, 0)),
                  pl.BlockSpec((k, tn), lambda i, j: (0, j)),
                  pl.BlockSpec((k, tn), lambda i, j: (0, j + nj))],
        out_specs=pl.BlockSpec((tm, tn), lambda i, j: (i, j)),
        out_shape=jax.ShapeDtypeStruct((m, n), BF16),
        scratch_shapes=[pltpu.VMEM((tm, k), BF16)],
        compiler_params=_params(est),
        name="ffn_up",
    )(x, g, w, w)


FFN_DOWN_COLS = 512


def _ffn_down_kernel(a_ref, w_ref, x_ref, g_ref, o_ref):
    kk = pl.program_id(1)

    @pl.when(kk == 0)
    def _():
        o_ref[...] = x_ref[...]

    a = a_ref[...]
    for n0 in range(0, o_ref.shape[1], FFN_DOWN_COLS):
        cols = slice(n0, n0 + FFN_DOWN_COLS)
        o_ref[:, cols] += _bdot(a, w_ref[:, cols].astype(BF16))

    @pl.when(kk == pl.num_programs(1) - 1)
    def _():
        o_ref[...] = _rms_rows(o_ref[...], g_ref[...])


def _ffn_down(act, w, x, g, tm, tk):
    m, k = act.shape
    n = w.shape[1]
    single = pl.Buffered(1) if tm >= 1024 else None
    est = (2 * tm * tk * 2 + 2 * tk * n * 4 + tk * n * 2 + (1 if single else 2) * tm * n * 4 + 2 * tm * n * 4
           + 2 * tm * n * 4)
    return pl.pallas_call(
        _ffn_down_kernel,
        grid=(m // tm, k // tk),
        in_specs=[pl.BlockSpec((tm, tk), lambda i, kk: (i, kk)),
                  pl.BlockSpec((tk, n), lambda i, kk: (kk, 0)),
                  pl.BlockSpec((tm, n), lambda i, kk: (i, 0), pipeline_mode=single),
                  pl.BlockSpec((1, n), lambda i, kk: (0, 0))],
        out_specs=pl.BlockSpec((tm, n), lambda i, kk: (i, 0)),
        out_shape=jax.ShapeDtypeStruct((m, n), F32),
        compiler_params=_params(est),
        name="ffn_down_final_norm",
    )(act, w, x, g)


def _lower_bound_kernel(lb_ref, o_ref):
    x = lb_ref[...]
    mx = jnp.max(x, axis=0, keepdims=True)
    e = jnp.exp(x - mx)
    o_ref[...] = e[0:1, :] / jnp.sum(e, axis=0, keepdims=True)


def _lower_bound(lb_logits):
    return pl.pallas_call(
        _lower_bound_kernel,
        out_shape=jax.ShapeDtypeStruct((1, HGRN_WIDTH), F32),
        name="hgrn_lower_bound",
    )(lb_logits)


def _split3(x):
    hi = x.astype(BF16)
    r = x - hi.astype(F32)
    mid = r.astype(BF16)
    lo = (r - mid.astype(F32)).astype(BF16)
    return hi, mid, lo


def _rows_ref(b, block, pick):
    parts = []
    for s0 in range(0, HGRN_CHUNK, block):
        parts.append(jnp.broadcast_to(b[s0 + pick:s0 + pick + 1, :], (block, b.shape[1])))
    return parts[0] if len(parts) == 1 else jnp.concatenate(parts, axis=0)


def _hgrn_prompt_kernel(n_chunks, q_ref, f_ref, i_ref, og_ref, lb_ref, gn_ref, oa_ref, s_ref, st_scr):
    t = pl.program_id(1)

    @pl.when(t == 0)
    def _():
        st_scr[...] = jnp.zeros(st_scr.shape, F32)

    c_ = HGRN_CHUNK
    row = lax.broadcasted_iota(jnp.int32, (c_, c_), 0)
    col = lax.broadcasted_iota(jnp.int32, (c_, c_), 1)
    tri = (col <= row).astype(BF16)
    m_diag = (row // HGRN_SUB == col // HGRN_SUB) & (col <= row)
    levels = (128, 64, 32)
    m_lvl = [(row // bs == col // bs) & (row % bs >= bs // 2) & (col % bs < bs // 2) for bs in levels]
    rcol = lax.broadcasted_iota(jnp.int32, (c_, 1), 0)
    second = [(rcol % bs) >= bs // 2 for bs in levels]

    lb = lb_ref[...]
    gn = gn_ref[...]

    def chunk(c, carry):
        r0 = pl.multiple_of(c * c_, c_)
        rows = pl.ds(r0, c_)
        f = lb + (1.0 - lb) * jax.nn.sigmoid(f_ref[rows, :].astype(F32))
        g = jnp.log(f)
        k_all = 1.0 - f
        g_hi, g_mid, g_lo = _split3(g)
        b = _bdot(tri, g_hi) + _bdot(tri, g_mid) + _bdot(tri, g_lo)
        b_last = b[c_ - 1:c_, :]
        e_in = jnp.exp(b)
        e_upd = jnp.exp(b_last - b)
        dec_all = jnp.exp(b_last)
        b_mid = _rows_ref(b, HGRN_SUB, HGRN_SUB // 2)
        e_dq = jnp.exp(b - b_mid)
        e_dk = jnp.exp(b_mid - b)
        e_lvl = []
        for li, bs in enumerate(levels):
            d = b - _rows_ref(b, bs, bs // 2 - 1)
            e_lvl.append(jnp.exp(jnp.where(second[li], d, -d)))
        q_all = q_ref[rows, :].astype(F32)
        v_all = i_ref[rows, :].astype(F32)
        og = og_ref[rows, :].astype(F32)
        for h in range(HEADS):
            ls = slice(h * HEAD_DIM, (h + 1) * HEAD_DIM)
            q, k, v = q_all[:, ls], k_all[:, ls], v_all[:, ls]
            nt = (((1,), (1,)), ((), ()))
            sc = lax.dot_general((q * e_dq[:, ls]).astype(BF16), (k * e_dk[:, ls]).astype(BF16), nt,
                                 preferred_element_type=F32)
            scores = jnp.where(m_diag, sc, 0.0)
            for li in range(len(levels)):
                e = e_lvl[li][:, ls]
                sc = lax.dot_general((q * e).astype(BF16), (k * e).astype(BF16), nt, preferred_element_type=F32)
                scores = jnp.where(m_lvl[li], sc, scores)
            v_b = v.astype(BF16)
            st = st_scr[h]
            o = _bdot(scores.astype(BF16), v_b)
            o = o + lax.dot_general((q * e_in[:, ls]).astype(BF16), st.astype(BF16), nt, preferred_element_type=F32)
            upd = lax.dot_general(v_b, (k * e_upd[:, ls]).astype(BF16), (((0,), (0,)), ((), ())),
                                  preferred_element_type=F32)
            st_scr[h] = dec_all[:, ls] * st + upd
            o = o * lax.rsqrt(jnp.mean(o * o, axis=-1, keepdims=True) + RMS_EPS) * gn[:, ls]
            oa_ref[rows, ls] = (o * jax.nn.silu(og[:, ls])).astype(BF16)
        return carry

    lax.fori_loop(0, n_chunks, chunk, 0)

    @pl.when(t == pl.num_programs(1) - 1)
    def _():
        for h in range(HEADS):
            s_ref[0, h] = st_scr[h].T


def _hgrn_prompt(proj, lb, gn, batch, seq, tt):
    m = batch * seq
    w = HGRN_WIDTH
    nt = seq // tt
    blk = lambda off: pl.BlockSpec((tt, w), lambda b, t, off=off: (b * nt + t, off // w))
    est = 8 * tt * w * 4 + 2 * tt * w * 2 + 3 * HEADS * HEAD_DIM * HEAD_DIM * 4 + 24 * HGRN_CHUNK * w * 4
    return pl.pallas_call(
        functools.partial(_hgrn_prompt_kernel, tt // HGRN_CHUNK),
        grid=(batch, nt),
        in_specs=[blk(OFF_Q), blk(OFF_F), blk(OFF_I), blk(OFF_OG),
                  pl.BlockSpec((1, w), lambda b, t: (0, 0)),
                  pl.BlockSpec((1, w), lambda b, t: (0, 0))],
        out_specs=[pl.BlockSpec((tt, w), lambda b, t: (b * nt + t, 0)),
                   pl.BlockSpec((1, HEADS, HEAD_DIM, HEAD_DIM), lambda b, t: (b, 0, 0, 0))],
        out_shape=[jax.ShapeDtypeStruct((m, w), BF16),
                   jax.ShapeDtypeStruct((batch, HEADS, HEAD_DIM, HEAD_DIM), F32)],
        scratch_shapes=[pltpu.VMEM((HEADS, HEAD_DIM, HEAD_DIM), F32)],
        compiler_params=_params(est),
        name="hgrn2_prompt",
    )(proj, proj, proj, proj, lb, gn)


def _hgrn_sample_kernel(nb, qt_ref, ft_ref, v_ref, og_ref, lbc_ref, gn_ref, s0_ref, oa_ref, s_ref, o_scr):
    lbc = lbc_ref[...]
    f_t = lbc + (1.0 - lbc) * jax.nn.sigmoid(ft_ref[...])
    k_t = 1.0 - f_t
    q_t = qt_ref[...]
    v = v_ref[...].astype(F32)
    for b in range(nb):
        s_new = f_t[:, b:b + 1] * s0_ref[b, 0] + k_t[:, b:b + 1] * v[b:b + 1, :]
        s_ref[b, 0] = s_new
        o_scr[b:b + 1, :] = jnp.sum(q_t[:, b:b + 1] * s_new, axis=0, keepdims=True)
    o = o_scr[...]
    o = o * lax.rsqrt(jnp.mean(o * o, axis=-1, keepdims=True) + RMS_EPS) * gn_ref[...]
    oa_ref[...] = (o * jax.nn.silu(og_ref[...].astype(F32))).astype(BF16)


def _hgrn_sample(q_t, f_t, proj, lb_col, gn, s0):
    nb = s0.shape[0]
    hd = HEAD_DIM
    est = 4 * nb * hd * hd * 4 + 16 * nb * hd * 4
    return pl.pallas_call(
        functools.partial(_hgrn_sample_kernel, nb),
        grid=(HEADS,),
        in_specs=[pl.BlockSpec((hd, nb), lambda h: (h, 0)),
                  pl.BlockSpec((hd, nb), lambda h: (h, 0)),
                  pl.BlockSpec((nb, hd), lambda h: (0, OFF_I // hd + h)),
                  pl.BlockSpec((nb, hd), lambda h: (0, OFF_OG // hd + h)),
                  pl.BlockSpec((hd, 1), lambda h: (h, 0)),
                  pl.BlockSpec((1, hd), lambda h: (0, h)),
                  pl.BlockSpec((nb, 1, hd, hd), lambda h: (0, h, 0, 0))],
        out_specs=[pl.BlockSpec((nb, hd), lambda h: (0, h)),
                   pl.BlockSpec((nb, 1, hd, hd), lambda h: (0, h, 0, 0))],
        out_shape=[jax.ShapeDtypeStruct((nb, HGRN_WIDTH), BF16),
                   jax.ShapeDtypeStruct(s0.shape, F32)],
        scratch_shapes=[pltpu.VMEM((nb, hd), F32)],
        compiler_params=_params(est),
        name="hgrn2_sample",
    )(q_t, f_t, proj, proj, lb_col, gn, s0)


def _cmul(ar, ai, br, bi):
    return ar * br - ai * bi, ar * bi + ai * br


def _gelu_exact(x):
    return 0.5 * x * (1.0 + lax.erf(x * math.sqrt(0.5)))


def _s5_discretise_kernel(lre_ref, lim_ref, ldt_ref, pre_ref, pim_ref, zre_ref, zim_ref):
    lam_re, lam_im = lre_ref[...], lim_ref[...]
    dt = jnp.exp(ldt_ref[...])
    mag = jnp.exp(lam_re * dt)
    ab_re = mag * jnp.cos(lam_im * dt)
    ab_im = mag * jnp.sin(lam_im * dt)
    den = lam_re * lam_re + lam_im * lam_im
    nr, ni = ab_re - 1.0, ab_im
    zre_ref[...] = (nr * lam_re + ni * lam_im) / den
    zim_ref[...] = (ni * lam_re - nr * lam_im) / den
    pr, pi = ab_re, ab_im
    for kk in range(N_POW):
        pre_ref[kk] = pr
        pim_ref[kk] = pi
        pr, pi = _cmul(pr, pi, pr, pi)


def _s5_discretise(lam_re, lam_im, log_dt):
    g, n = lam_re.shape
    return pl.pallas_call(
        _s5_discretise_kernel,
        out_shape=[jax.ShapeDtypeStruct((N_POW, g, n), F32), jax.ShapeDtypeStruct((N_POW, g, n), F32),
                   jax.ShapeDtypeStruct((g, n), F32), jax.ShapeDtypeStruct((g, n), F32)],
        name="s5_discretise",
    )(lam_re, lam_im, log_dt)


def _s5_weights_kernel(gb, prow_re, prow_im, pcol_re, pcol_im, zre_ref, zim_ref, btre_ref, btim_ref,
                       ctre_ref, ctim_ref, wt_ref, wx_ref, wc_ref, apw_ref):
    fl = S5_FLAT
    srow = lax.broadcasted_iota(jnp.int32, (fl, 1), 0) // S5_GROUP
    tlan = lax.broadcasted_iota(jnp.int32, (1, fl), 1) // S5_GROUP
    jrow = lax.broadcasted_iota(jnp.int32, (S5_STATE, 1), 0)
    for gi in range(gb):
        zr, zi = zre_ref[gi], zim_ref[gi]
        bbr, bbi = _cmul(zr, zi, btre_ref[gi], btim_ref[gi])
        xr = jnp.concatenate([bbr] * S5_CHUNK, axis=0)
        xi = jnp.concatenate([bbi] * S5_CHUNK, axis=0)
        e_x = (S5_CHUNK - 1) - srow
        for kk in range(4):
            ar, ai = prow_re[gi, kk:kk + 1, :], prow_im[gi, kk:kk + 1, :]
            yr, yi = _cmul(xr, xi, ar, ai)
            hit = ((e_x >> kk) & 1) == 1
            xr, xi = jnp.where(hit, yr, xr), jnp.where(hit, yi, xi)
        wx_ref[gi] = jnp.concatenate([xr, xi], axis=1).astype(BF16)
        cr, ci = ctre_ref[gi], ctim_ref[gi]
        for kk in range(4):
            ar, ai = pcol_re[gi, :, kk:kk + 1], pcol_im[gi, :, kk:kk + 1]
            yr, yi = _cmul(cr, ci, ar, ai)
            hit = ((tlan >> kk) & 1) == 1
            cr, ci = jnp.where(hit, yr, cr), jnp.where(hit, yi, ci)
        g0 = jnp.concatenate([cr, -ci], axis=0)
        c1r, c1i = _cmul(cr, ci, pcol_re[gi, :, 0:1], pcol_im[gi, :, 0:1])
        wc_ref[gi] = jnp.concatenate([c1r, -c1i], axis=0).astype(BF16)
        kflat = jnp.dot(jnp.concatenate([bbr, bbi], axis=1), g0, preferred_element_type=F32,
                        precision=lax.Precision.HIGHEST)
        pieces = [kflat]
        for s in range(1, S5_CHUNK):
            pieces.append(jnp.concatenate([jnp.zeros((S5_GROUP, s * S5_GROUP), F32),
                                           kflat[:, :fl - s * S5_GROUP]], axis=1))
        wt_ref[gi] = jnp.concatenate(pieces, axis=0).astype(BF16)
        pr = jnp.ones((S5_STATE, S5_STATE), F32)
        pi = jnp.zeros((S5_STATE, S5_STATE), F32)
        for kk in range(6):
            ar, ai = prow_re[gi, 4 + kk:5 + kk, :], prow_im[gi, 4 + kk:5 + kk, :]
            yr, yi = _cmul(pr, pi, ar, ai)
            hit = ((jrow >> kk) & 1) == 1
            pr, pi = jnp.where(hit, yr, pr), jnp.where(hit, yi, pi)
        apw_ref[gi] = jnp.concatenate([pr, pi], axis=1)


def _s5_weights(prow_re, prow_im, pcol_re, pcol_im, z_re, z_im, bt_re, bt_im, ct_re, ct_im, gb):
    g = S5_GROUPS
    n, p, fl = S5_STATE, S5_GROUP, S5_FLAT
    b3 = lambda a, b: pl.BlockSpec((gb, a, b), lambda i: (i, 0, 0))
    return pl.pallas_call(
        functools.partial(_s5_weights_kernel, gb),
        grid=(g // gb,),
        in_specs=[b3(N_POW, n), b3(N_POW, n), b3(n, N_POW), b3(n, N_POW), b3(1, n), b3(1, n),
                  b3(p, n), b3(p, n), b3(n, fl), b3(n, fl)],
        out_specs=[b3(fl, fl), b3(fl, 2 * n), b3(2 * n, fl), b3(n, 2 * n)],
        out_shape=[jax.ShapeDtypeStruct((g, fl, fl), BF16), jax.ShapeDtypeStruct((g, fl, 2 * n), BF16),
                   jax.ShapeDtypeStruct((g, 2 * n, fl), BF16), jax.ShapeDtypeStruct((g, n, 2 * n), F32)],
        name="s5_chunk_weights",
    )(prow_re, prow_im, pcol_re, pcol_im, z_re, z_im, bt_re, bt_im, ct_re, ct_im)


LANE_GROUPS = 128 // S5_GROUP
RELAYOUT_ROWS = 64


def _block_transpose8(xs):
    lane_blk = lax.broadcasted_iota(jnp.int32, (1, 128), 1) // S5_GROUP
    xs = list(xs)
    for d in (4, 2, 1):
        upper = (lane_blk & d) != 0
        nxt = list(xs)
        for a in range(8):
            if a & d:
                continue
            lo, hi = xs[a], xs[a + d]
            nxt[a] = jnp.where(upper, pltpu.roll(hi, S5_GROUP * d, axis=1), lo)
            nxt[a + d] = jnp.where(upper, hi, pltpu.roll(lo, 128 - S5_GROUP * d, axis=1))
        xs = nxt
    return xs


def _s5_prompt_kernel(nseq, u_ref, wt_ref, wx_ref, wc_ref, apw_ref, a16_ref, d_ref, z_ref, hend_ref,
                      tok_scr, flat_scr, x_scr, h_scr):
    n = S5_STATE
    gb = LANE_GROUPS
    rows = u_ref.shape[0] // S5_CHUNK
    slab = 2 * nseq
    nlo = rows // slab
    rb = RELAYOUT_ROWS
    halves = S5_FLAT // 128

    tok_scr[...] = u_ref[...].astype(F32)

    def to_flat(r, carry):
        r0 = pl.multiple_of(r * rb, rb)
        for sh in range(halves):
            xs = [tok_scr[pl.ds(r0 * S5_CHUNK + 8 * sh + s, rb, stride=S5_CHUNK), :] for s in range(8)]
            ys = _block_transpose8(xs)
            for gi in range(gb):
                flat_scr[gi, pl.ds(r0, rb), sh * 128:(sh + 1) * 128] = ys[gi]
        return carry

    lax.fori_loop(0, rows // rb, to_flat, 0)

    odd = (lax.broadcasted_iota(jnp.int32, (slab, 1), 0) % 2) == 1
    for gi in range(gb):
        u = flat_scr[gi]
        u_b = u.astype(BF16)
        x_scr[gi] = _bdot(u_b, wx_ref[gi])
        ar, ai = a16_ref[gi, :, :n], a16_ref[gi, :, n:]
        hr = jnp.zeros((slab, n), F32)
        hi = jnp.zeros((slab, n), F32)
        for j in range(nlo):
            rs = pl.ds(j, slab, stride=nlo)
            h_scr[gi, rs, :] = jnp.concatenate([hr, hi], axis=1)
            xj = x_scr[gi, rs, :]
            nr, ni = _cmul(hr, hi, ar, ai)
            hr = nr + xj[:, :n]
            hi = ni + xj[:, n:]
        mr = jnp.where(odd, pltpu.roll(hr, 1, axis=0), 0.0)
        mi = jnp.where(odd, pltpu.roll(hi, 1, axis=0), 0.0)
        pr, pi = apw_ref[gi, :, :n], apw_ref[gi, :, n:]
        for k in range(1, slab, 2):
            cr, ci = _cmul(pr, pi, mr[k:k + 1], mi[k:k + 1])
            h_scr[gi, k * nlo:(k + 1) * nlo, :n] += cr
            h_scr[gi, k * nlo:(k + 1) * nlo, n:] += ci
        lr, li = _cmul(pr[nlo - 1:nlo], pi[nlo - 1:nlo], ar, ai)
        er, ei = _cmul(mr, mi, lr, li)
        hend_ref[gi] = jnp.concatenate([hr + er, hi + ei], axis=1)
        y = _bdot(u_b, wt_ref[gi]) + _bdot(h_scr[gi].astype(BF16), wc_ref[gi]) + d_ref[gi] * u
        flat_scr[gi] = _gelu_exact(y)

    def to_tokens(r, carry):
        r0 = pl.multiple_of(r * rb, rb)
        for sh in range(halves):
            ys = [flat_scr[gi, pl.ds(r0, rb), sh * 128:(sh + 1) * 128] for gi in range(gb)]
            xs = _block_transpose8(ys)
            for s in range(8):
                tok_scr[pl.ds(r0 * S5_CHUNK + 8 * sh + s, rb, stride=S5_CHUNK), :] = xs[s]
        return carry

    lax.fori_loop(0, rows // rb, to_tokens, 0)
    z_ref[...] = tok_scr[...].astype(BF16)


def _s5_prompt(proj, wt, wx, wc, apw, a16, d_flat, nseq):
    m = proj.shape[0]
    g, n, fl, gb = S5_GROUPS, S5_STATE, S5_FLAT, LANE_GROUPS
    rows = m // S5_CHUNK
    b3 = lambda a, b: pl.BlockSpec((gb, a, b), lambda i: (i, 0, 0))
    est = (4 * m * 128 * 2 + m * 128 * 4 + gb * rows * (fl + 4 * n) * 4 + 6 * rows * fl * 4
           + 2 * gb * (fl * fl + 4 * n * fl) * 2)
    return pl.pallas_call(
        functools.partial(_s5_prompt_kernel, nseq),
        grid=(g // gb,),
        in_specs=[pl.BlockSpec((m, 128), lambda i: (0, OFF_U // 128 + i)),
                  b3(fl, fl), b3(fl, 2 * n), b3(2 * n, fl), b3(n, 2 * n), b3(1, 2 * n), b3(1, fl)],
        out_specs=[pl.BlockSpec((m, 128), lambda i: (0, i)), b3(2 * nseq, 2 * n)],
        out_shape=[jax.ShapeDtypeStruct((m, S5_WIDTH), BF16), jax.ShapeDtypeStruct((g, 2 * nseq, 2 * n), F32)],
        scratch_shapes=[pltpu.VMEM((m, 128), F32), pltpu.VMEM((gb, rows, fl), F32),
                        pltpu.VMEM((gb, rows, 2 * n), F32), pltpu.VMEM((gb, rows, 2 * n), F32)],
        compiler_params=_params(est),
        name="s5_prompt",
    )(proj, wt, wx, wc, apw, a16, d_flat)


def _s5_sample_kernel(gb, u_ref, bst_ref, ctre_ref, ctim_ref, a1_ref, hre_ref, him_ref, d_ref,
                      z_ref, ore_ref, oim_ref):
    n = S5_STATE
    u = u_ref[...].astype(F32)
    lane_g = lax.broadcasted_iota(jnp.int32, (1, gb * S5_GROUP), 1) // S5_GROUP
    bst = bst_ref[...]
    y = d_ref[...] * u
    for gi in range(gb):
        mine = lane_g == gi
        bu = _bdot(jnp.where(mine, u, 0.0).astype(BF16), bst)
        ar, ai = a1_ref[gi, :, :n], a1_ref[gi, :, n:]
        nr, ni = _cmul(hre_ref[gi], him_ref[gi], ar, ai)
        hr = nr + bu[:, :n]
        hi = ni + bu[:, n:]
        ore_ref[gi] = hr
        oim_ref[gi] = hi
        cmat = jnp.concatenate([jnp.where(mine, ctre_ref[gi], 0.0), jnp.where(mine, -ctim_ref[gi], 0.0)], axis=0)
        y = y + _bdot(jnp.concatenate([hr, hi], axis=1).astype(BF16), cmat.astype(BF16))
    z_ref[...] = _gelu_exact(y).astype(BF16)


def _s5_sample(proj, bstack, ct_re, ct_im, a1, h_re, h_im, d_row, gb):
    nb = proj.shape[0]
    g, n, p = S5_GROUPS, S5_STATE, S5_GROUP
    lanes = gb * p
    b3 = lambda a, b: pl.BlockSpec((gb, a, b), lambda i: (i, 0, 0))
    return pl.pallas_call(
        functools.partial(_s5_sample_kernel, gb),
        grid=(g // gb,),
        in_specs=[pl.BlockSpec((nb, lanes), lambda i: (0, OFF_U // lanes + i)),
                  pl.BlockSpec((lanes, 2 * n), lambda i: (i, 0)),
                  b3(n, lanes), b3(n, lanes), b3(1, 2 * n), b3(nb, n), b3(nb, n),
                  pl.BlockSpec((1, lanes), lambda i: (0, i))],
        out_specs=[pl.BlockSpec((nb, lanes), lambda i: (0, i)), b3(nb, n), b3(nb, n)],
        out_shape=[jax.ShapeDtypeStruct((nb, S5_WIDTH), BF16),
                   jax.ShapeDtypeStruct((g, nb, n), F32), jax.ShapeDtypeStruct((g, nb, n), F32)],
        name="s5_sample",
    )(proj, bstack, ct_re, ct_im, a1, h_re, h_im, d_row)


def _dense_tail(x, proj, o_a, z, w, tm):
    o_b = _glu(z, w['w_s5_glu'], tm, 512)
    merged = _merge(o_a, o_b, w['w_proj_a'], w['w_proj_b'], proj, tm, 512)
    x1 = _outproj(merged, w['w_out'], x, tm, 512)
    act = _ffn_up(x1, w['norm2_g'], w['w_ffn_up'], tm, 256)
    return _ffn_down(act, w['w_ffn_down'], x1, w['final_norm_g'], min(tm, 1024), 512)


def kernel(x_prompt, x_sample, state_hgrn, state_s5_re, state_s5_im, lb_logits, norm1_g, w_in, hgrn_norm_g,
           s5_lam_re, s5_lam_im, s5_log_dt, s5_B_re, s5_B_im, s5_C_re, s5_C_im, s5_D, w_s5_glu, w_proj_a,
           w_proj_b, w_out, norm2_g, w_ffn_up, w_ffn_down, final_norm_g):
    l = 0
    bp, seq, d = x_prompt.shape
    nb = x_sample.shape[0]
    g, n, p = S5_GROUPS, S5_STATE, S5_GROUP
    w = {'w_s5_glu': w_s5_glu[l], 'w_proj_a': w_proj_a[l], 'w_proj_b': w_proj_b[l], 'w_out': w_out[l],
         'norm2_g': norm2_g[l][None, :], 'w_ffn_up': w_ffn_up[l], 'w_ffn_down': w_ffn_down[l],
         'final_norm_g': final_norm_g[None, :]}
    g1 = norm1_g[l][None, :]
    gn = hgrn_norm_g[l][None, :]

    xp = x_prompt.reshape(bp * seq, d)
    xs = x_sample.reshape(nb, d)
    proj_p = _inproj(xp, g1, w_in[l], 2048, 512)
    proj_s = _inproj(xs, g1, w_in[l], nb, 512)

    lb = _lower_bound(lb_logits)
    oa_p, sh_p = _hgrn_prompt(proj_p, lb, gn, bp, seq, 256)
    q_t = proj_s[:, OFF_Q:OFF_Q + HGRN_WIDTH].astype(F32).T
    f_t = proj_s[:, OFF_F:OFF_F + HGRN_WIDTH].astype(F32).T
    oa_s, sh_s = _hgrn_sample(q_t, f_t, proj_s, lb.reshape(HGRN_WIDTH, 1), gn, state_hgrn[l])

    pw_re, pw_im, z_re, z_im = _s5_discretise(s5_lam_re[l], s5_lam_im[l], s5_log_dt[l][:, None])
    prow_re, prow_im = pw_re.transpose(1, 0, 2), pw_im.transpose(1, 0, 2)
    pcol_re, pcol_im = pw_re.transpose(1, 2, 0), pw_im.transpose(1, 2, 0)
    bt_re, bt_im = s5_B_re[l].transpose(0, 2, 1), s5_B_im[l].transpose(0, 2, 1)
    ct_re = jnp.tile(s5_C_re[l].transpose(0, 2, 1), (1, 1, S5_CHUNK))
    ct_im = jnp.tile(s5_C_im[l].transpose(0, 2, 1), (1, 1, S5_CHUNK))
    wt, wx, wc, apw = _s5_weights(prow_re, prow_im, pcol_re, pcol_im, z_re[:, None, :], z_im[:, None, :],
                                  bt_re, bt_im, ct_re, ct_im, 8)
    d_gp = s5_D[l].reshape(g, 1, p)
    d_flat = jnp.tile(d_gp, (1, 1, S5_CHUNK))
    a16 = jnp.concatenate([prow_re[:, 4:5, :], prow_im[:, 4:5, :]], axis=2)
    a1 = jnp.concatenate([prow_re[:, 0:1, :], prow_im[:, 0:1, :]], axis=2)

    z_p, hend = _s5_prompt(proj_p, wt, wx, wc, apw, a16, d_flat, bp)
    hend = hend[:, 1::2, :]
    s5re_p = hend[:, :, :n].transpose(1, 0, 2)
    s5im_p = hend[:, :, n:].transpose(1, 0, 2)

    bstack = wx[:, S5_FLAT - p:, :].reshape(g * p, 2 * n)
    gbs = 128 // p
    z_s, s5re_s, s5im_s = _s5_sample(proj_s, bstack, ct_re[:, :, :gbs * p], ct_im[:, :, :gbs * p], a1,
                                     state_s5_re[l].transpose(1, 0, 2), state_s5_im[l].transpose(1, 0, 2),
                                     s5_D[l][None, :], gbs)

    y_p = _dense_tail(xp, proj_p, oa_p, z_p, w, 2048)
    y_s = _dense_tail(xs, proj_s, oa_s, z_s, w, nb)

    return (y_p.reshape(bp, seq, d), y_s.reshape(nb, 1, d),
            sh_p[None], s5re_p[None], s5im_p[None],
            sh_s[None], s5re_s.transpose(1, 0, 2)[None], s5im_s.transpose(1, 0, 2)[None])
```

```python
import functools
import math

import jax
import jax.numpy as jnp
from jax import lax
from jax.experimental import pallas as pl
from jax.experimental.pallas import tpu as pltpu

F32 = jnp.float32
BF16 = jnp.bfloat16

D_MODEL = 2048
HGRN_WIDTH = 1024
HEAD_DIM = 128
HEADS = 8
S5_WIDTH = 1024
S5_GROUPS = 64
S5_GROUP = 16
S5_STATE = 64
FFN_HIDDEN = 5632
IN_PROJ_WIDTH = 4 * HGRN_WIDTH + S5_WIDTH + 2 * D_MODEL
RMS_EPS = 1e-6

OFF_Q, OFF_F, OFF_I, OFF_OG = 0, HGRN_WIDTH, 2 * HGRN_WIDTH, 3 * HGRN_WIDTH
OFF_U = 4 * HGRN_WIDTH
OFF_GA = OFF_U + S5_WIDTH
OFF_GB = OFF_GA + D_MODEL

HGRN_SUB = 16
HGRN_CHUNK = 128
S5_CHUNK = 16
S5_FLAT = S5_CHUNK * S5_GROUP
N_POW = 10

V7X_VMEM_BYTES = 64 * 1024 * 1024
VMEM_LIMIT_CAP = V7X_VMEM_BYTES - 6 * 1024 * 1024


def _params(est_bytes):
    limit = min(max(int(est_bytes * 1.25) + (4 << 20), 32 << 20), VMEM_LIMIT_CAP)
    return pltpu.CompilerParams(vmem_limit_bytes=limit)


def _bdot(a, b):
    return jnp.dot(a, b, preferred_element_type=F32)


def _rms_rows(x, g):
    ms = jnp.mean(x * x, axis=-1, keepdims=True)
    return x * lax.rsqrt(ms + RMS_EPS) * g


def _inproj_kernel(x_ref, g_ref, w_ref, o_ref, h_scr):
    @pl.when(pl.program_id(1) == 0)
    def _():
        h_scr[...] = _rms_rows(x_ref[...], g_ref[...]).astype(BF16)

    o_ref[...] = _bdot(h_scr[...], w_ref[...].astype(BF16)).astype(BF16)


def _inproj(x, g, w, tm, tn):
    m, k = x.shape
    n = w.shape[1]
    single = pl.Buffered(1) if tm >= 2048 else None
    est = ((1 if single else 2) * tm * k * 4 + 2 * k * tn * 4 + 2 * tm * tn * 2 + tm * k * 2 + k * tn * 2
           + tm * tn * 6)
    return pl.pallas_call(
        _inproj_kernel,
        grid=(m // tm, n // tn),
        in_specs=[pl.BlockSpec((tm, k), lambda i, j: (i, 0), pipeline_mode=single),
                  pl.BlockSpec((1, k), lambda i, j: (0, 0)),
                  pl.BlockSpec((k, tn), lambda i, j: (0, j))],
        out_specs=pl.BlockSpec((tm, tn), lambda i, j: (i, j)),
        out_shape=jax.ShapeDtypeStruct((m, n), BF16),
        scratch_shapes=[pltpu.VMEM((tm, k), BF16)],
        compiler_params=_params(est),
        name="inproj",
    )(x, g, w)


def _glu_kernel(z_ref, wa_ref, wb_ref, o_ref):
    z = z_ref[...]
    a = _bdot(z, wa_ref[...].astype(BF16))
    b = _bdot(z, wb_ref[...].astype(BF16))
    o_ref[...] = (a * jax.nn.sigmoid(b)).astype(BF16)


def _glu(z, w, tm, tn):
    m, k = z.shape
    n = w.shape[1] // 2
    nj = n // tn
    est = 2 * tm * k * 2 + 4 * k * tn * 4 + 2 * tm * tn * 2 + 2 * k * tn * 2 + 3 * tm * tn * 4
    return pl.pallas_call(
        _glu_kernel,
        grid=(m // tm, nj),
        in_specs=[pl.BlockSpec((tm, k), lambda i, j: (i, 0)),
                  pl.BlockSpec((k, tn), lambda i, j: (0, j)),
                  pl.BlockSpec((k, tn), lambda i, j: (0, j + nj))],
        out_specs=pl.BlockSpec((tm, tn), lambda i, j: (i, j)),
        out_shape=jax.ShapeDtypeStruct((m, n), BF16),
        compiler_params=_params(est),
        name="s5_glu",
    )(z, w, w)


def _merge_kernel(oa_ref, ob_ref, wa_ref, wb_ref, ga_ref, gb_ref, o_ref):
    a = _bdot(oa_ref[...], wa_ref[...].astype(BF16))
    b = _bdot(ob_ref[...], wb_ref[...].astype(BF16))
    ga, gb = ga_ref[...].astype(F32), gb_ref[...].astype(F32)
    o_ref[...] = (jax.nn.sigmoid(ga) * a + jax.nn.sigmoid(gb) * b).astype(BF16)


def _merge(o_a, o_b, w_a, w_b, proj, tm, tn):
    m, k = o_a.shape
    n = w_a.shape[1]
    ja, jb = OFF_GA // tn, OFF_GB // tn
    est = 4 * tm * k * 2 + 4 * k * tn * 4 + 4 * tm * tn * 4 + 2 * tm * tn * 2 + 2 * k * tn * 2 + 3 * tm * tn * 4
    return pl.pallas_call(
        _merge_kernel,
        grid=(m // tm, n // tn),
        in_specs=[pl.BlockSpec((tm, k), lambda i, j: (i, 0)),
                  pl.BlockSpec((tm, k), lambda i, j: (i, 0)),
                  pl.BlockSpec((k, tn), lambda i, j: (0, j)),
                  pl.BlockSpec((k, tn), lambda i, j: (0, j)),
                  pl.BlockSpec((tm, tn), lambda i, j: (i, j + ja)),
                  pl.BlockSpec((tm, tn), lambda i, j: (i, j + jb))],
        out_specs=pl.BlockSpec((tm, tn), lambda i, j: (i, j)),
        out_shape=jax.ShapeDtypeStruct((m, n), BF16),
        compiler_params=_params(est),
        name="gated_merge",
    )(o_a, o_b, w_a, w_b, proj, proj)


def _outproj_kernel(m_ref, w_ref, x_ref, o_ref):
    o_ref[...] = x_ref[...] + _bdot(m_ref[...], w_ref[...].astype(BF16))


def _outproj(merged, w, x, tm, tn):
    m, k = merged.shape
    n = w.shape[1]
    est = 2 * tm * k * 2 + 2 * k * tn * 4 + 4 * tm * tn * 4 + k * tn * 2 + tm * tn * 4
    return pl.pallas_call(
        _outproj_kernel,
        grid=(m // tm, n // tn),
        in_specs=[pl.BlockSpec((tm, k), lambda i, j: (i, 0)),
                  pl.BlockSpec((k, tn), lambda i, j: (0, j)),
                  pl.BlockSpec((tm, tn), lambda i, j: (i, j))],
        out_specs=pl.BlockSpec((tm, tn), lambda i, j: (i, j)),
        out_shape=jax.ShapeDtypeStruct((m, n), F32),
        compiler_params=_params(est),
        name="out_proj",
    )(merged, w, x)


def _ffn_up_kernel(x_ref, g_ref, wa_ref, wb_ref, o_ref, h_scr):
    @pl.when(pl.program_id(1) == 0)
    def _():
        h_scr[...] = _rms_rows(x_ref[...], g_ref[...]).astype(BF16)

    h = h_scr[...]
    a = _bdot(h, wa_ref[...].astype(BF16))
    b = _bdot(h, wb_ref[...].astype(BF16))
    o_ref[...] = (jax.nn.silu(a) * b).astype(BF16)


def _ffn_up(x, g, w, tm, tn):
    m, k = x.shape
    n = w.shape[1] // 2
    nj = n // tn
    single = pl.Buffered(1) if tm >= 2048 else None
    est = ((1 if single else 2) * tm * k * 4 + 4 * k * tn * 4 + 2 * tm * tn * 2 + tm * k * 2 + 2 * k * tn * 2
           + 3 * tm * tn * 4)
    return pl.pallas_call(
        _ffn_up_kernel,
        grid=(m // tm, nj),
        in_specs=[pl.BlockSpec((tm, k), lambda i, j: (i, 0), pipeline_mode=single),
                  pl.BlockSpec((1, k), lambda i, j: (0, 0)),
                  pl.BlockSpec((k, tn), lambda i, j: (0, j)),
                  pl.BlockSpec((k, tn), lambda i, j: (0, j + nj))],
        out_specs=pl.BlockSpec((tm, tn), lambda i, j: (i, j)),
        out_shape=jax.ShapeDtypeStruct((m, n), BF16),
        scratch_shapes=[pltpu.VMEM((tm, k), BF16)],
        compiler_params=_params(est),
        name="ffn_up",
    )(x, g, w, w)


FFN_DOWN_COLS = 512


def _ffn_down_kernel(a_ref, w_ref, x_ref, g_ref, o_ref):
    kk = pl.program_id(1)

    @pl.when(kk == 0)
    def _():
        o_ref[...] = x_ref[...]

    a = a_ref[...]
    for n0 in range(0, o_ref.shape[1], FFN_DOWN_COLS):
        cols = slice(n0, n0 + FFN_DOWN_COLS)
        o_ref[:, cols] += _bdot(a, w_ref[:, cols])

    @pl.when(kk == pl.num_programs(1) - 1)
    def _():
        o_ref[...] = _rms_rows(o_ref[...], g_ref[...])


def _ffn_down(act, w, x, g, tm, tk):
    m, k = act.shape
    n = w.shape[1]
    est = 2 * tm * tk * 2 + 2 * tk * n * 2 + 4 * tm * n * 4 + 2 * tm * n * 4
    return pl.pallas_call(
        _ffn_down_kernel,
        grid=(m // tm, k // tk),
        in_specs=[pl.BlockSpec((tm, tk), lambda i, kk: (i, kk)),
                  pl.BlockSpec((tk, n), lambda i, kk: (kk, 0)),
                  pl.BlockSpec((tm, n), lambda i, kk: (i, 0)),
                  pl.BlockSpec((1, n), lambda i, kk: (0, 0))],
        out_specs=pl.BlockSpec((tm, n), lambda i, kk: (i, 0)),
        out_shape=jax.ShapeDtypeStruct((m, n), F32),
        compiler_params=_params(est),
        name="ffn_down_final_norm",
    )(act, w, x, g)


def _lower_bound_kernel(lb_ref, o_ref):
    x = lb_ref[...]
    mx = jnp.max(x, axis=0, keepdims=True)
    e = jnp.exp(x - mx)
    o_ref[...] = e[0:1, :] / jnp.sum(e, axis=0, keepdims=True)


def _lower_bound(lb_logits):
    return pl.pallas_call(
        _lower_bound_kernel,
        out_shape=jax.ShapeDtypeStruct((1, HGRN_WIDTH), F32),
        name="hgrn_lower_bound",
    )(lb_logits)


def _split3(x):
    hi = x.astype(BF16)
    r = x - hi.astype(F32)
    mid = r.astype(BF16)
    lo = (r - mid.astype(F32)).astype(BF16)
    return hi, mid, lo


def _rows_ref(b, block, pick):
    parts = []
    for s0 in range(0, HGRN_CHUNK, block):
        parts.append(jnp.broadcast_to(b[s0 + pick:s0 + pick + 1, :], (block, b.shape[1])))
    return parts[0] if len(parts) == 1 else jnp.concatenate(parts, axis=0)


def _hgrn_prompt_kernel(n_chunks, q_ref, f_ref, i_ref, og_ref, lb_ref, gn_ref, oa_ref, s_ref, st_scr):
    t = pl.program_id(1)

    @pl.when(t == 0)
    def _():
        st_scr[...] = jnp.zeros(st_scr.shape, F32)

    c_ = HGRN_CHUNK
    row = lax.broadcasted_iota(jnp.int32, (c_, c_), 0)
    col = lax.broadcasted_iota(jnp.int32, (c_, c_), 1)
    tri = (col <= row).astype(BF16)
    m_diag = (row // HGRN_SUB == col // HGRN_SUB) & (col <= row)
    levels = (128, 64, 32)
    m_lvl = [(row // bs == col // bs) & (row % bs >= bs // 2) & (col % bs < bs // 2) for bs in levels]
    rcol = lax.broadcasted_iota(jnp.int32, (c_, 1), 0)
    second = [(rcol % bs) >= bs // 2 for bs in levels]

    lb = lb_ref[...]
    gn = gn_ref[...]

    def chunk(c, carry):
        r0 = pl.multiple_of(c * c_, c_)
        rows = pl.ds(r0, c_)
        f = lb + (1.0 - lb) * jax.nn.sigmoid(f_ref[rows, :].astype(F32))
        g = jnp.log(f)
        k_all = 1.0 - f
        g_hi, g_mid, g_lo = _split3(g)
        b = _bdot(tri, g_hi) + _bdot(tri, g_mid) + _bdot(tri, g_lo)
        b_last = b[c_ - 1:c_, :]
        e_in = jnp.exp(b)
        e_upd = jnp.exp(b_last - b)
        dec_all = jnp.exp(b_last)
        b_mid = _rows_ref(b, HGRN_SUB, HGRN_SUB // 2)
        e_dq = jnp.exp(b - b_mid)
        e_dk = jnp.exp(b_mid - b)
        e_lvl = []
        for li, bs in enumerate(levels):
            d = b - _rows_ref(b, bs, bs // 2 - 1)
            e_lvl.append(jnp.exp(jnp.where(second[li], d, -d)))
        q_all = q_ref[rows, :].astype(F32)
        v_all = i_ref[rows, :].astype(F32)
        og = og_ref[rows, :].astype(F32)
        for h in range(HEADS):
            ls = slice(h * HEAD_DIM, (h + 1) * HEAD_DIM)
            q, k, v = q_all[:, ls], k_all[:, ls], v_all[:, ls]
            nt = (((1,), (1,)), ((), ()))
            sc = lax.dot_general((q * e_dq[:, ls]).astype(BF16), (k * e_dk[:, ls]).astype(BF16), nt,
                                 preferred_element_type=F32)
            scores = jnp.where(m_diag, sc, 0.0)
            for li in range(len(levels)):
                e = e_lvl[li][:, ls]
                sc = lax.dot_general((q * e).astype(BF16), (k * e).astype(BF16), nt, preferred_element_type=F32)
                scores = jnp.where(m_lvl[li], sc, scores)
            v_b = v.astype(BF16)
            st = st_scr[h]
            o = _bdot(scores.astype(BF16), v_b)
            o = o + lax.dot_general((q * e_in[:, ls]).astype(BF16), st.astype(BF16), nt, preferred_element_type=F32)
            upd = lax.dot_general(v_b, (k * e_upd[:, ls]).astype(BF16), (((0,), (0,)), ((), ())),
                                  preferred_element_type=F32)
            st_scr[h] = dec_all[:, ls] * st + upd
            o = o * lax.rsqrt(jnp.mean(o * o, axis=-1, keepdims=True) + RMS_EPS) * gn[:, ls]
            oa_ref[rows, ls] = (o * jax.nn.silu(og[:, ls])).astype(BF16)
        return carry

    lax.fori_loop(0, n_chunks, chunk, 0)

    @pl.when(t == pl.num_programs(1) - 1)
    def _():
        for h in range(HEADS):
            s_ref[0, h] = st_scr[h].T


def _hgrn_prompt(proj, lb, gn, batch, seq, tt):
    m = batch * seq
    w = HGRN_WIDTH
    nt = seq // tt
    blk = lambda off: pl.BlockSpec((tt, w), lambda b, t, off=off: (b * nt + t, off // w))
    est = 8 * tt * w * 4 + 2 * tt * w * 2 + 3 * HEADS * HEAD_DIM * HEAD_DIM * 4 + 24 * HGRN_CHUNK * w * 4
    return pl.pallas_call(
        functools.partial(_hgrn_prompt_kernel, tt // HGRN_CHUNK),
        grid=(batch, nt),
        in_specs=[blk(OFF_Q), blk(OFF_F), blk(OFF_I), blk(OFF_OG),
                  pl.BlockSpec((1, w), lambda b, t: (0, 0)),
                  pl.BlockSpec((1, w), lambda b, t: (0, 0))],
        out_specs=[pl.BlockSpec((tt, w), lambda b, t: (b * nt + t, 0)),
                   pl.BlockSpec((1, HEADS, HEAD_DIM, HEAD_DIM), lambda b, t: (b, 0, 0, 0))],
        out_shape=[jax.ShapeDtypeStruct((m, w), BF16),
                   jax.ShapeDtypeStruct((batch, HEADS, HEAD_DIM, HEAD_DIM), F32)],
        scratch_shapes=[pltpu.VMEM((HEADS, HEAD_DIM, HEAD_DIM), F32)],
        compiler_params=_params(est),
        name="hgrn2_prompt",
    )(proj, proj, proj, proj, lb, gn)


def _hgrn_sample_kernel(nb, qt_ref, ft_ref, v_ref, og_ref, lbc_ref, gn_ref, s0_ref, oa_ref, s_ref, o_scr):
    lbc = lbc_ref[...]
    f_t = lbc + (1.0 - lbc) * jax.nn.sigmoid(ft_ref[...])
    k_t = 1.0 - f_t
    q_t = qt_ref[...]
    v = v_ref[...].astype(F32)
    for b in range(nb):
        s_new = f_t[:, b:b + 1] * s0_ref[b, 0] + k_t[:, b:b + 1] * v[b:b + 1, :]
        s_ref[b, 0] = s_new
        o_scr[b:b + 1, :] = jnp.sum(q_t[:, b:b + 1] * s_new, axis=0, keepdims=True)
    o = o_scr[...]
    o = o * lax.rsqrt(jnp.mean(o * o, axis=-1, keepdims=True) + RMS_EPS) * gn_ref[...]
    oa_ref[...] = (o * jax.nn.silu(og_ref[...].astype(F32))).astype(BF16)


def _hgrn_sample(q_t, f_t, proj, lb_col, gn, s0):
    nb = s0.shape[0]
    hd = HEAD_DIM
    est = 4 * nb * hd * hd * 4 + 16 * nb * hd * 4
    return pl.pallas_call(
        functools.partial(_hgrn_sample_kernel, nb),
        grid=(HEADS,),
        in_specs=[pl.BlockSpec((hd, nb), lambda h: (h, 0)),
                  pl.BlockSpec((hd, nb), lambda h: (h, 0)),
                  pl.BlockSpec((nb, hd), lambda h: (0, OFF_I // hd + h)),
                  pl.BlockSpec((nb, hd), lambda h: (0, OFF_OG // hd + h)),
                  pl.BlockSpec((hd, 1), lambda h: (h, 0)),
                  pl.BlockSpec((1, hd), lambda h: (0, h)),
                  pl.BlockSpec((nb, 1, hd, hd), lambda h: (0, h, 0, 0))],
        out_specs=[pl.BlockSpec((nb, hd), lambda h: (0, h)),
                   pl.BlockSpec((nb, 1, hd, hd), lambda h: (0, h, 0, 0))],
        out_shape=[jax.ShapeDtypeStruct((nb, HGRN_WIDTH), BF16),
                   jax.ShapeDtypeStruct(s0.shape, F32)],
        scratch_shapes=[pltpu.VMEM((nb, hd), F32)],
        compiler_params=_params(est),
        name="hgrn2_sample",
    )(q_t, f_t, proj, proj, lb_col, gn, s0)


def _cmul(ar, ai, br, bi):
    return ar * br - ai * bi, ar * bi + ai * br


def _gelu_exact(x):
    return 0.5 * x * (1.0 + lax.erf(x * math.sqrt(0.5)))


def _s5_discretise_kernel(lre_ref, lim_ref, ldt_ref, pre_ref, pim_ref, zre_ref, zim_ref):
    lam_re, lam_im = lre_ref[...], lim_ref[...]
    dt = jnp.exp(ldt_ref[...])
    mag = jnp.exp(lam_re * dt)
    ab_re = mag * jnp.cos(lam_im * dt)
    ab_im = mag * jnp.sin(lam_im * dt)
    den = lam_re * lam_re + lam_im * lam_im
    nr, ni = ab_re - 1.0, ab_im
    zre_ref[...] = (nr * lam_re + ni * lam_im) / den
    zim_ref[...] = (ni * lam_re - nr * lam_im) / den
    pr, pi = ab_re, ab_im
    for kk in range(N_POW):
        pre_ref[kk] = pr
        pim_ref[kk] = pi
        pr, pi = _cmul(pr, pi, pr, pi)


def _s5_discretise(lam_re, lam_im, log_dt):
    g, n = lam_re.shape
    return pl.pallas_call(
        _s5_discretise_kernel,
        out_shape=[jax.ShapeDtypeStruct((N_POW, g, n), F32), jax.ShapeDtypeStruct((N_POW, g, n), F32),
                   jax.ShapeDtypeStruct((g, n), F32), jax.ShapeDtypeStruct((g, n), F32)],
        name="s5_discretise",
    )(lam_re, lam_im, log_dt)


def _s5_weights_kernel(gb, prow_re, prow_im, pcol_re, pcol_im, zre_ref, zim_ref, btre_ref, btim_ref,
                       ctre_ref, ctim_ref, wt_ref, wx_ref, wc_ref, apw_ref):
    fl = S5_FLAT
    srow = lax.broadcasted_iota(jnp.int32, (fl, 1), 0) // S5_GROUP
    tlan = lax.broadcasted_iota(jnp.int32, (1, fl), 1) // S5_GROUP
    jrow = lax.broadcasted_iota(jnp.int32, (S5_STATE, 1), 0)
    for gi in range(gb):
        zr, zi = zre_ref[gi], zim_ref[gi]
        bbr, bbi = _cmul(zr, zi, btre_ref[gi], btim_ref[gi])
        xr = jnp.concatenate([bbr] * S5_CHUNK, axis=0)
        xi = jnp.concatenate([bbi] * S5_CHUNK, axis=0)
        e_x = (S5_CHUNK - 1) - srow
        for kk in range(4):
            ar, ai = prow_re[gi, kk:kk + 1, :], prow_im[gi, kk:kk + 1, :]
            yr, yi = _cmul(xr, xi, ar, ai)
            hit = ((e_x >> kk) & 1) == 1
            xr, xi = jnp.where(hit, yr, xr), jnp.where(hit, yi, xi)
        wx_ref[gi] = jnp.concatenate([xr, xi], axis=1).astype(BF16)
        cr, ci = ctre_ref[gi], ctim_ref[gi]
        for kk in range(4):
            ar, ai = pcol_re[gi, :, kk:kk + 1], pcol_im[gi, :, kk:kk + 1]
            yr, yi = _cmul(cr, ci, ar, ai)
            hit = ((tlan >> kk) & 1) == 1
            cr, ci = jnp.where(hit, yr, cr), jnp.where(hit, yi, ci)
        g0 = jnp.concatenate([cr, -ci], axis=0)
        c1r, c1i = _cmul(cr, ci, pcol_re[gi, :, 0:1], pcol_im[gi, :, 0:1])
        wc_ref[gi] = jnp.concatenate([c1r, -c1i], axis=0).astype(BF16)
        kflat = jnp.dot(jnp.concatenate([bbr, bbi], axis=1), g0, preferred_element_type=F32,
                        precision=lax.Precision.HIGHEST)
        pieces = [kflat]
        for s in range(1, S5_CHUNK):
            pieces.append(jnp.concatenate([jnp.zeros((S5_GROUP, s * S5_GROUP), F32),
                                           kflat[:, :fl - s * S5_GROUP]], axis=1))
        wt_ref[gi] = jnp.concatenate(pieces, axis=0).astype(BF16)
        pr = jnp.ones((S5_STATE, S5_STATE), F32)
        pi = jnp.zeros((S5_STATE, S5_STATE), F32)
        for kk in range(6):
            ar, ai = prow_re[gi, 4 + kk:5 + kk, :], prow_im[gi, 4 + kk:5 + kk, :]
            yr, yi = _cmul(pr, pi, ar, ai)
            hit = ((jrow >> kk) & 1) == 1
            pr, pi = jnp.where(hit, yr, pr), jnp.where(hit, yi, pi)
        apw_ref[gi] = jnp.concatenate([pr, pi], axis=1)


def _s5_weights(prow_re, prow_im, pcol_re, pcol_im, z_re, z_im, bt_re, bt_im, ct_re, ct_im, gb):
    g = S5_GROUPS
    n, p, fl = S5_STATE, S5_GROUP, S5_FLAT
    b3 = lambda a, b: pl.BlockSpec((gb, a, b), lambda i: (i, 0, 0))
    return pl.pallas_call(
        functools.partial(_s5_weights_kernel, gb),
        grid=(g // gb,),
        in_specs=[b3(N_POW, n), b3(N_POW, n), b3(n, N_POW), b3(n, N_POW), b3(1, n), b3(1, n),
                  b3(p, n), b3(p, n), b3(n, fl), b3(n, fl)],
        out_specs=[b3(fl, fl), b3(fl, 2 * n), b3(2 * n, fl), b3(n, 2 * n)],
        out_shape=[jax.ShapeDtypeStruct((g, fl, fl), BF16), jax.ShapeDtypeStruct((g, fl, 2 * n), BF16),
                   jax.ShapeDtypeStruct((g, 2 * n, fl), BF16), jax.ShapeDtypeStruct((g, n, 2 * n), F32)],
        name="s5_chunk_weights",
    )(prow_re, prow_im, pcol_re, pcol_im, z_re, z_im, bt_re, bt_im, ct_re, ct_im)


LANE_GROUPS = 128 // S5_GROUP
RELAYOUT_ROWS = 64


def _block_transpose8(xs):
    lane_blk = lax.broadcasted_iota(jnp.int32, (1, 128), 1) // S5_GROUP
    xs = list(xs)
    for d in (4, 2, 1):
        upper = (lane_blk & d) != 0
        nxt = list(xs)
        for a in range(8):
            if a & d:
                continue
            lo, hi = xs[a], xs[a + d]
            nxt[a] = jnp.where(upper, pltpu.roll(hi, S5_GROUP * d, axis=1), lo)
            nxt[a + d] = jnp.where(upper, hi, pltpu.roll(lo, 128 - S5_GROUP * d, axis=1))
        xs = nxt
    return xs


def _s5_prompt_kernel(nseq, u_ref, wt_ref, wx_ref, wc_ref, apw_ref, a16_ref, d_ref, z_ref, hend_ref,
                      tok_scr, flat_scr, x_scr, h_scr):
    n = S5_STATE
    gb = LANE_GROUPS
    rows = u_ref.shape[0] // S5_CHUNK
    slab = 2 * nseq
    nlo = rows // slab
    rb = RELAYOUT_ROWS
    halves = S5_FLAT // 128

    tok_scr[...] = u_ref[...].astype(F32)

    def to_flat(r, carry):
        r0 = pl.multiple_of(r * rb, rb)
        for sh in range(halves):
            xs = [tok_scr[pl.ds(r0 * S5_CHUNK + 8 * sh + s, rb, stride=S5_CHUNK), :] for s in range(8)]
            ys = _block_transpose8(xs)
            for gi in range(gb):
                flat_scr[gi, pl.ds(r0, rb), sh * 128:(sh + 1) * 128] = ys[gi]
        return carry

    lax.fori_loop(0, rows // rb, to_flat, 0)

    odd = (lax.broadcasted_iota(jnp.int32, (slab, 1), 0) % 2) == 1
    for gi in range(gb):
        u = flat_scr[gi]
        u_b = u.astype(BF16)
        x_scr[gi] = _bdot(u_b, wx_ref[gi])
        ar, ai = a16_ref[gi, :, :n], a16_ref[gi, :, n:]
        hr = jnp.zeros((slab, n), F32)
        hi = jnp.zeros((slab, n), F32)
        for j in range(nlo):
            rs = pl.ds(j, slab, stride=nlo)
            h_scr[gi, rs, :] = jnp.concatenate([hr, hi], axis=1)
            xj = x_scr[gi, rs, :]
            nr, ni = _cmul(hr, hi, ar, ai)
            hr = nr + xj[:, :n]
            hi = ni + xj[:, n:]
        mr = jnp.where(odd, pltpu.roll(hr, 1, axis=0), 0.0)
        mi = jnp.where(odd, pltpu.roll(hi, 1, axis=0), 0.0)
        pr, pi = apw_ref[gi, :, :n], apw_ref[gi, :, n:]
        for k in range(1, slab, 2):
            cr, ci = _cmul(pr, pi, mr[k:k + 1], mi[k:k + 1])
            h_scr[gi, k * nlo:(k + 1) * nlo, :n] += cr
            h_scr[gi, k * nlo:(k + 1) * nlo, n:] += ci
        lr, li = _cmul(pr[nlo - 1:nlo], pi[nlo - 1:nlo], ar, ai)
        er, ei = _cmul(mr, mi, lr, li)
        hend_ref[gi] = jnp.concatenate([hr + er, hi + ei], axis=1)
        y = _bdot(u_b, wt_ref[gi]) + _bdot(h_scr[gi].astype(BF16), wc_ref[gi]) + d_ref[gi] * u
        flat_scr[gi] = _gelu_exact(y)

    def to_tokens(r, carry):
        r0 = pl.multiple_of(r * rb, rb)
        for sh in range(halves):
            ys = [flat_scr[gi, pl.ds(r0, rb), sh * 128:(sh + 1) * 128] for gi in range(gb)]
            xs = _block_transpose8(ys)
            for s in range(8):
                tok_scr[pl.ds(r0 * S5_CHUNK + 8 * sh + s, rb, stride=S5_CHUNK), :] = xs[s]
        return carry

    lax.fori_loop(0, rows // rb, to_tokens, 0)
    z_ref[...] = tok_scr[...].astype(BF16)


def _s5_prompt(proj, wt, wx, wc, apw, a16, d_flat, nseq):
    m = proj.shape[0]
    g, n, fl, gb = S5_GROUPS, S5_STATE, S5_FLAT, LANE_GROUPS
    rows = m // S5_CHUNK
    b3 = lambda a, b: pl.BlockSpec((gb, a, b), lambda i: (i, 0, 0))
    est = (4 * m * 128 * 2 + m * 128 * 4 + gb * rows * (fl + 4 * n) * 4 + 6 * rows * fl * 4
           + 2 * gb * (fl * fl + 4 * n * fl) * 2)
    return pl.pallas_call(
        functools.partial(_s5_prompt_kernel, nseq),
        grid=(g // gb,),
        in_specs=[pl.BlockSpec((m, 128), lambda i: (0, OFF_U // 128 + i)),
                  b3(fl, fl), b3(fl, 2 * n), b3(2 * n, fl), b3(n, 2 * n), b3(1, 2 * n), b3(1, fl)],
        out_specs=[pl.BlockSpec((m, 128), lambda i: (0, i)), b3(2 * nseq, 2 * n)],
        out_shape=[jax.ShapeDtypeStruct((m, S5_WIDTH), BF16), jax.ShapeDtypeStruct((g, 2 * nseq, 2 * n), F32)],
        scratch_shapes=[pltpu.VMEM((m, 128), F32), pltpu.VMEM((gb, rows, fl), F32),
                        pltpu.VMEM((gb, rows, 2 * n), F32), pltpu.VMEM((gb, rows, 2 * n), F32)],
        compiler_params=_params(est),
        name="s5_prompt",
    )(proj, wt, wx, wc, apw, a16, d_flat)


def _s5_sample_kernel(gb, u_ref, bst_ref, ctre_ref, ctim_ref, a1_ref, hre_ref, him_ref, d_ref,
                      z_ref, ore_ref, oim_ref):
    n = S5_STATE
    u = u_ref[...].astype(F32)
    lane_g = lax.broadcasted_iota(jnp.int32, (1, gb * S5_GROUP), 1) // S5_GROUP
    bst = bst_ref[...]
    y = d_ref[...] * u
    for gi in range(gb):
        mine = lane_g == gi
        bu = _bdot(jnp.where(mine, u, 0.0).astype(BF16), bst)
        ar, ai = a1_ref[gi, :, :n], a1_ref[gi, :, n:]
        nr, ni = _cmul(hre_ref[:, gi, :], him_ref[:, gi, :], ar, ai)
        hr = nr + bu[:, :n]
        hi = ni + bu[:, n:]
        ore_ref[:, gi, :] = hr
        oim_ref[:, gi, :] = hi
        cmat = jnp.concatenate([jnp.where(mine, ctre_ref[gi], 0.0), jnp.where(mine, -ctim_ref[gi], 0.0)], axis=0)
        y = y + _bdot(jnp.concatenate([hr, hi], axis=1).astype(BF16), cmat.astype(BF16))
    z_ref[...] = _gelu_exact(y).astype(BF16)


def _s5_sample(proj, bstack, ct_re, ct_im, a1, h_re, h_im, d_row, gb):
    nb = proj.shape[0]
    g, n, p = S5_GROUPS, S5_STATE, S5_GROUP
    lanes = gb * p
    b3 = lambda a, b: pl.BlockSpec((gb, a, b), lambda i: (i, 0, 0))
    st = pl.BlockSpec((nb, gb, n), lambda i: (0, i, 0))
    return pl.pallas_call(
        functools.partial(_s5_sample_kernel, gb),
        grid=(g // gb,),
        in_specs=[pl.BlockSpec((nb, lanes), lambda i: (0, OFF_U // lanes + i)),
                  pl.BlockSpec((lanes, 2 * n), lambda i: (i, 0)),
                  b3(n, lanes), b3(n, lanes), b3(1, 2 * n), st, st,
                  pl.BlockSpec((1, lanes), lambda i: (0, i))],
        out_specs=[pl.BlockSpec((nb, lanes), lambda i: (0, i)), st, st],
        out_shape=[jax.ShapeDtypeStruct((nb, S5_WIDTH), BF16),
                   jax.ShapeDtypeStruct((nb, g, n), F32), jax.ShapeDtypeStruct((nb, g, n), F32)],
        name="s5_sample",
    )(proj, bstack, ct_re, ct_im, a1, h_re, h_im, d_row)


def _dense_tail(x, proj, o_a, z, w, tm):
    o_b = _glu(z, w['w_s5_glu'], tm, 512)
    merged = _merge(o_a, o_b, w['w_proj_a'], w['w_proj_b'], proj, tm, 512)
    x1 = _outproj(merged, w['w_out'], x, tm, 512)
    act = _ffn_up(x1, w['norm2_g'], w['w_ffn_up'], tm, 256)
    return _ffn_down(act, w['w_ffn_down'], x1, w['final_norm_g'], min(tm, 1024), 512)


def kernel(x_prompt, x_sample, state_hgrn, state_s5_re, state_s5_im, lb_logits, norm1_g, w_in, hgrn_norm_g,
           s5_lam_re, s5_lam_im, s5_log_dt, s5_B_re, s5_B_im, s5_C_re, s5_C_im, s5_D, w_s5_glu, w_proj_a,
           w_proj_b, w_out, norm2_g, w_ffn_up, w_ffn_down, final_norm_g):
    l = 0
    bp, seq, d = x_prompt.shape
    nb = x_sample.shape[0]
    g, n, p = S5_GROUPS, S5_STATE, S5_GROUP
    w = {'w_s5_glu': w_s5_glu[l], 'w_proj_a': w_proj_a[l], 'w_proj_b': w_proj_b[l], 'w_out': w_out[l],
         'norm2_g': norm2_g[l][None, :], 'w_ffn_up': w_ffn_up[l], 'w_ffn_down': w_ffn_down[l].astype(BF16),
         'final_norm_g': final_norm_g[None, :]}
    g1 = norm1_g[l][None, :]
    gn = hgrn_norm_g[l][None, :]

    xp = x_prompt.reshape(bp * seq, d)
    xs = x_sample.reshape(nb, d)
    proj_p = _inproj(xp, g1, w_in[l], 2048, 512)
    proj_s = _inproj(xs, g1, w_in[l], nb, 512)

    lb = _lower_bound(lb_logits)
    oa_p, sh_p = _hgrn_prompt(proj_p, lb, gn, bp, seq, 256)
    q_t = proj_s[:, OFF_Q:OFF_Q + HGRN_WIDTH].astype(F32).T
    f_t = proj_s[:, OFF_F:OFF_F + HGRN_WIDTH].astype(F32).T
    oa_s, sh_s = _hgrn_sample(q_t, f_t, proj_s, lb.reshape(HGRN_WIDTH, 1), gn, state_hgrn[l])

    pw_re, pw_im, z_re, z_im = _s5_discretise(s5_lam_re[l], s5_lam_im[l], s5_log_dt[l][:, None])
    prow_re, prow_im = pw_re.transpose(1, 0, 2), pw_im.transpose(1, 0, 2)
    pcol_re, pcol_im = pw_re.transpose(1, 2, 0), pw_im.transpose(1, 2, 0)
    bt_re, bt_im = s5_B_re[l].transpose(0, 2, 1), s5_B_im[l].transpose(0, 2, 1)
    ct_re = jnp.tile(s5_C_re[l].transpose(0, 2, 1), (1, 1, S5_CHUNK))
    ct_im = jnp.tile(s5_C_im[l].transpose(0, 2, 1), (1, 1, S5_CHUNK))
    wt, wx, wc, apw = _s5_weights(prow_re, prow_im, pcol_re, pcol_im, z_re[:, None, :], z_im[:, None, :],
                                  bt_re, bt_im, ct_re, ct_im, 8)
    d_gp = s5_D[l].reshape(g, 1, p)
    d_flat = jnp.tile(d_gp, (1, 1, S5_CHUNK))
    a16 = jnp.concatenate([prow_re[:, 4:5, :], prow_im[:, 4:5, :]], axis=2)
    a1 = jnp.concatenate([prow_re[:, 0:1, :], prow_im[:, 0:1, :]], axis=2)

    z_p, hend = _s5_prompt(proj_p, wt, wx, wc, apw, a16, d_flat, bp)
    hend = hend[:, 1::2, :]
    s5re_p = hend[:, :, :n].transpose(1, 0, 2)
    s5im_p = hend[:, :, n:].transpose(1, 0, 2)

    bstack = wx[:, S5_FLAT - p:, :].reshape(g * p, 2 * n)
    gbs = 128 // p
    z_s, s5re_s, s5im_s = _s5_sample(proj_s, bstack, ct_re[:, :, :gbs * p], ct_im[:, :, :gbs * p], a1,
                                     state_s5_re[l], state_s5_im[l],
                                     s5_D[l][None, :], gbs)

    y_p = _dense_tail(xp, proj_p, oa_p, z_p, w, 2048)
    y_s = _dense_tail(xs, proj_s, oa_s, z_s, w, nb)

    return (y_p.reshape(bp, seq, d), y_s.reshape(nb, 1, d),
            sh_p[None], s5re_p[None], s5im_p[None],
            sh_s[None], s5re_s[None], s5im_s[None])
```

```python
import functools
import math

import jax
import jax.numpy as jnp
from jax import lax
from jax.experimental import pallas as pl
from jax.experimental.pallas import tpu as pltpu

F32 = jnp.float32
BF16 = jnp.bfloat16

D_MODEL = 2048
HGRN_WIDTH = 1024
HEAD_DIM = 128
HEADS = 8
S5_WIDTH = 1024
S5_GROUPS = 64
S5_GROUP = 16
S5_STATE = 64
FFN_HIDDEN = 5632
IN_PROJ_WIDTH = 4 * HGRN_WIDTH + S5_WIDTH + 2 * D_MODEL
RMS_EPS = 1e-6

OFF_Q, OFF_F, OFF_I, OFF_OG = 0, HGRN_WIDTH, 2 * HGRN_WIDTH, 3 * HGRN_WIDTH
OFF_U = 4 * HGRN_WIDTH
OFF_GA = OFF_U + S5_WIDTH
OFF_GB = OFF_GA + D_MODEL

HGRN_SUB = 16
HGRN_CHUNK = 128
S5_CHUNK = 16
S5_FLAT = S5_CHUNK * S5_GROUP
N_POW = 10

V7X_VMEM_BYTES = 64 * 1024 * 1024
VMEM_LIMIT_CAP = V7X_VMEM_BYTES - 6 * 1024 * 1024


def _params(est_bytes):
    limit = min(max(int(est_bytes * 1.25) + (4 << 20), 32 << 20), VMEM_LIMIT_CAP)
    return pltpu.CompilerParams(vmem_limit_bytes=limit)


def _bdot(a, b):
    return jnp.dot(a, b, preferred_element_type=F32)


def _rms_rows(x, g):
    ms = jnp.mean(x * x, axis=-1, keepdims=True)
    return x * lax.rsqrt(ms + RMS_EPS) * g


def _inproj_kernel(x_ref, g_ref, w_ref, o_ref, h_scr):
    @pl.when(pl.program_id(1) == 0)
    def _():
        h_scr[...] = _rms_rows(x_ref[...], g_ref[...]).astype(BF16)

    o_ref[...] = _bdot(h_scr[...], w_ref[...].astype(BF16)).astype(BF16)


def _inproj(x, g, w, tm, tn):
    m, k = x.shape
    n = w.shape[1]
    single = pl.Buffered(1) if tm >= 2048 else None
    est = ((1 if single else 2) * tm * k * 4 + 2 * k * tn * 4 + 2 * tm * tn * 2 + tm * k * 2 + k * tn * 2
           + tm * tn * 6)
    return pl.pallas_call(
        _inproj_kernel,
        grid=(m // tm, n // tn),
        in_specs=[pl.BlockSpec((tm, k), lambda i, j: (i, 0), pipeline_mode=single),
                  pl.BlockSpec((1, k), lambda i, j: (0, 0)),
                  pl.BlockSpec((k, tn), lambda i, j: (0, j))],
        out_specs=pl.BlockSpec((tm, tn), lambda i, j: (i, j)),
        out_shape=jax.ShapeDtypeStruct((m, n), BF16),
        scratch_shapes=[pltpu.VMEM((tm, k), BF16)],
        compiler_params=_params(est),
        name="inproj",
    )(x, g, w)


def _glu_kernel(z_ref, wa_ref, wb_ref, o_ref):
    z = z_ref[...]
    a = _bdot(z, wa_ref[...].astype(BF16))
    b = _bdot(z, wb_ref[...].astype(BF16))
    o_ref[...] = (a * jax.nn.sigmoid(b)).astype(BF16)


def _glu(z, w, tm, tn):
    m, k = z.shape
    n = w.shape[1] // 2
    nj = n // tn
    est = 2 * tm * k * 2 + 4 * k * tn * 4 + 2 * tm * tn * 2 + 2 * k * tn * 2 + 3 * tm * tn * 4
    return pl.pallas_call(
        _glu_kernel,
        grid=(m // tm, nj),
        in_specs=[pl.BlockSpec((tm, k), lambda i, j: (i, 0)),
                  pl.BlockSpec((k, tn), lambda i, j: (0, j)),
                  pl.BlockSpec((k, tn), lambda i, j: (0, j + nj))],
        out_specs=pl.BlockSpec((tm, tn), lambda i, j: (i, j)),
        out_shape=jax.ShapeDtypeStruct((m, n), BF16),
        compiler_params=_params(est),
        name="s5_glu",
    )(z, w, w)


def _merge_kernel(oa_ref, ob_ref, wa_ref, wb_ref, ga_ref, gb_ref, o_ref):
    a = _bdot(oa_ref[...], wa_ref[...].astype(BF16))
    b = _bdot(ob_ref[...], wb_ref[...].astype(BF16))
    ga, gb = ga_ref[...].astype(F32), gb_ref[...].astype(F32)
    o_ref[...] = (jax.nn.sigmoid(ga) * a + jax.nn.sigmoid(gb) * b).astype(BF16)


def _merge(o_a, o_b, w_a, w_b, proj, tm, tn):
    m, k = o_a.shape
    n = w_a.shape[1]
    ja, jb = OFF_GA // tn, OFF_GB // tn
    est = 4 * tm * k * 2 + 4 * k * tn * 4 + 4 * tm * tn * 4 + 2 * tm * tn * 2 + 2 * k * tn * 2 + 3 * tm * tn * 4
    return pl.pallas_call(
        _merge_kernel,
        grid=(m // tm, n // tn),
        in_specs=[pl.BlockSpec((tm, k), lambda i, j: (i, 0)),
                  pl.BlockSpec((tm, k), lambda i, j: (i, 0)),
                  pl.BlockSpec((k, tn), lambda i, j: (0, j)),
                  pl.BlockSpec((k, tn), lambda i, j: (0, j)),
                  pl.BlockSpec((tm, tn), lambda i, j: (i, j + ja)),
                  pl.BlockSpec((tm, tn), lambda i, j: (i, j + jb))],
        out_specs=pl.BlockSpec((tm, tn), lambda i, j: (i, j)),
        out_shape=jax.ShapeDtypeStruct((m, n), BF16),
        compiler_params=_params(est),
        name="gated_merge",
    )(o_a, o_b, w_a, w_b, proj, proj)


def _outproj_kernel(m_ref, w_ref, x_ref, o_ref):
    o_ref[...] = x_ref[...] + _bdot(m_ref[...], w_ref[...].astype(BF16))


def _outproj(merged, w, x, tm, tn):
    m, k = merged.shape
    n = w.shape[1]
    est = 2 * tm * k * 2 + 2 * k * tn * 4 + 4 * tm * tn * 4 + k * tn * 2 + tm * tn * 4
    return pl.pallas_call(
        _outproj_kernel,
        grid=(m // tm, n // tn),
        in_specs=[pl.BlockSpec((tm, k), lambda i, j: (i, 0)),
                  pl.BlockSpec((k, tn), lambda i, j: (0, j)),
                  pl.BlockSpec((tm, tn), lambda i, j: (i, j))],
        out_specs=pl.BlockSpec((tm, tn), lambda i, j: (i, j)),
        out_shape=jax.ShapeDtypeStruct((m, n), F32),
        compiler_params=_params(est),
        name="out_proj",
    )(merged, w, x)


def _ffn_up_kernel(x_ref, g_ref, wa_ref, wb_ref, o_ref, h_scr):
    @pl.when(pl.program_id(1) == 0)
    def _():
        h_scr[...] = _rms_rows(x_ref[...], g_ref[...]).astype(BF16)

    h = h_scr[...]
    a = _bdot(h, wa_ref[...].astype(BF16))
    b = _bdot(h, wb_ref[...].astype(BF16))
    o_ref[...] = (jax.nn.silu(a) * b).astype(BF16)


def _ffn_up(x, g, w, tm, tn):
    m, k = x.shape
    n = w.shape[1] // 2
    nj = n // tn
    single = pl.Buffered(1) if tm >= 2048 else None
    est = ((1 if single else 2) * tm * k * 4 + 4 * k * tn * 4 + 2 * tm * tn * 2 + tm * k * 2 + 2 * k * tn * 2
           + 3 * tm * tn * 4)
    return pl.pallas_call(
        _ffn_up_kernel,
        grid=(m // tm, nj),
        in_specs=[pl.BlockSpec((tm, k), lambda i, j: (i, 0), pipeline_mode=single),
                  pl.BlockSpec((1, k), lambda i, j: (0, 0)),
                  pl.BlockSpec((k, tn), lambda i, j: (0, j)),
                  pl.BlockSpec((k, tn), lambda i, j: (0, j + nj))],
        out_specs=pl.BlockSpec((tm, tn), lambda i, j: (i, j)),
        out_shape=jax.ShapeDtypeStruct((m, n), BF16),
        scratch_shapes=[pltpu.VMEM((tm, k), BF16)],
        compiler_params=_params(est),
        name="ffn_up",
    )(x, g, w, w)


FFN_DOWN_COLS = 512


def _ffn_down_kernel(a_ref, w_ref, x_ref, g_ref, o_ref):
    kk = pl.program_id(1)

    @pl.when(kk == 0)
    def _():
        o_ref[...] = x_ref[...]

    a = a_ref[...]
    for n0 in range(0, o_ref.shape[1], FFN_DOWN_COLS):
        cols = slice(n0, n0 + FFN_DOWN_COLS)
        o_ref[:, cols] += _bdot(a, w_ref[:, cols])

    @pl.when(kk == pl.num_programs(1) - 1)
    def _():
        o_ref[...] = _rms_rows(o_ref[...], g_ref[...])


def _ffn_down(act, w, x, g, tm, tk):
    m, k = act.shape
    n = w.shape[1]
    est = 2 * tm * tk * 2 + 2 * tk * n * 2 + 4 * tm * n * 4 + 2 * tm * n * 4
    return pl.pallas_call(
        _ffn_down_kernel,
        grid=(m // tm, k // tk),
        in_specs=[pl.BlockSpec((tm, tk), lambda i, kk: (i, kk)),
                  pl.BlockSpec((tk, n), lambda i, kk: (kk, 0)),
                  pl.BlockSpec((tm, n), lambda i, kk: (i, 0)),
                  pl.BlockSpec((1, n), lambda i, kk: (0, 0))],
        out_specs=pl.BlockSpec((tm, n), lambda i, kk: (i, 0)),
        out_shape=jax.ShapeDtypeStruct((m, n), F32),
        compiler_params=_params(est),
        name="ffn_down_final_norm",
    )(act, w, x, g)


def _lower_bound_kernel(lb_ref, o_ref):
    x = lb_ref[...]
    mx = jnp.max(x, axis=0, keepdims=True)
    e = jnp.exp(x - mx)
    o_ref[...] = e[0:1, :] / jnp.sum(e, axis=0, keepdims=True)


def _lower_bound(lb_logits):
    return pl.pallas_call(
        _lower_bound_kernel,
        out_shape=jax.ShapeDtypeStruct((1, HGRN_WIDTH), F32),
        name="hgrn_lower_bound",
    )(lb_logits)


def _split3(x):
    hi = x.astype(BF16)
    r = x - hi.astype(F32)
    mid = r.astype(BF16)
    lo = (r - mid.astype(F32)).astype(BF16)
    return hi, mid, lo


def _rows_ref(b, block, pick):
    parts = []
    for s0 in range(0, HGRN_CHUNK, block):
        parts.append(jnp.broadcast_to(b[s0 + pick:s0 + pick + 1, :], (block, b.shape[1])))
    return parts[0] if len(parts) == 1 else jnp.concatenate(parts, axis=0)


def _hgrn_prompt_kernel(n_chunks, q_ref, f_ref, i_ref, og_ref, lb_ref, gn_ref, oa_ref, s_ref, st_scr):
    t = pl.program_id(1)

    @pl.when(t == 0)
    def _():
        st_scr[...] = jnp.zeros(st_scr.shape, F32)

    c_ = HGRN_CHUNK
    row = lax.broadcasted_iota(jnp.int32, (c_, c_), 0)
    col = lax.broadcasted_iota(jnp.int32, (c_, c_), 1)
    tri = (col <= row).astype(BF16)
    m_diag = (row // HGRN_SUB == col // HGRN_SUB) & (col <= row)
    levels = (128, 64, 32)
    m_lvl = [(row // bs == col // bs) & (row % bs >= bs // 2) & (col % bs < bs // 2) for bs in levels]
    rcol = lax.broadcasted_iota(jnp.int32, (c_, 1), 0)
    second = [(rcol % bs) >= bs // 2 for bs in levels]

    lb = lb_ref[...]
    gn = gn_ref[...]

    def chunk(c, carry):
        r0 = pl.multiple_of(c * c_, c_)
        rows = pl.ds(r0, c_)
        f = lb + (1.0 - lb) * jax.nn.sigmoid(f_ref[rows, :].astype(F32))
        g = jnp.log(f)
        k_all = 1.0 - f
        g_hi, g_mid, g_lo = _split3(g)
        b = _bdot(tri, g_hi) + _bdot(tri, g_mid) + _bdot(tri, g_lo)
        b_last = b[c_ - 1:c_, :]
        e_in = jnp.exp(b)
        e_upd = jnp.exp(b_last - b)
        dec_all = jnp.exp(b_last)
        b_mid = _rows_ref(b, HGRN_SUB, HGRN_SUB // 2)
        e_dq = jnp.exp(b - b_mid)
        e_dk = jnp.exp(b_mid - b)
        e_lvl = []
        for li, bs in enumerate(levels):
            d = b - _rows_ref(b, bs, bs // 2 - 1)
            e_lvl.append(jnp.exp(jnp.where(second[li], d, -d)))
        q_all = q_ref[rows, :].astype(F32)
        v_all = i_ref[rows, :].astype(F32)
        og = og_ref[rows, :].astype(F32)
        for h in range(HEADS):
            ls = slice(h * HEAD_DIM, (h + 1) * HEAD_DIM)
            q, k, v = q_all[:, ls], k_all[:, ls], v_all[:, ls]
            nt = (((1,), (1,)), ((), ()))
            sc = lax.dot_general((q * e_dq[:, ls]).astype(BF16), (k * e_dk[:, ls]).astype(BF16), nt,
                                 preferred_element_type=F32)
            scores = jnp.where(m_diag, sc, 0.0)
            for li in range(len(levels)):
                e = e_lvl[li][:, ls]
                sc = lax.dot_general((q * e).astype(BF16), (k * e).astype(BF16), nt, preferred_element_type=F32)
                scores = jnp.where(m_lvl[li], sc, scores)
            v_b = v.astype(BF16)
            st = st_scr[h]
            o = _bdot(scores.astype(BF16), v_b)
            o = o + lax.dot_general((q * e_in[:, ls]).astype(BF16), st.astype(BF16), nt, preferred_element_type=F32)
            upd = lax.dot_general(v_b, (k * e_upd[:, ls]).astype(BF16), (((0,), (0,)), ((), ())),
                                  preferred_element_type=F32)
            st_scr[h] = dec_all[:, ls] * st + upd
            o = o * lax.rsqrt(jnp.mean(o * o, axis=-1, keepdims=True) + RMS_EPS) * gn[:, ls]
            oa_ref[rows, ls] = (o * jax.nn.silu(og[:, ls])).astype(BF16)
        return carry

    lax.fori_loop(0, n_chunks, chunk, 0)

    @pl.when(t == pl.num_programs(1) - 1)
    def _():
        for h in range(HEADS):
            s_ref[0, h] = st_scr[h].T


def _hgrn_prompt(proj, lb, gn, batch, seq, tt):
    m = batch * seq
    w = HGRN_WIDTH
    nt = seq // tt
    blk = lambda off: pl.BlockSpec((tt, w), lambda b, t, off=off: (b * nt + t, off // w))
    est = 8 * tt * w * 4 + 2 * tt * w * 2 + 3 * HEADS * HEAD_DIM * HEAD_DIM * 4 + 24 * HGRN_CHUNK * w * 4
    return pl.pallas_call(
        functools.partial(_hgrn_prompt_kernel, tt // HGRN_CHUNK),
        grid=(batch, nt),
        in_specs=[blk(OFF_Q), blk(OFF_F), blk(OFF_I), blk(OFF_OG),
                  pl.BlockSpec((1, w), lambda b, t: (0, 0)),
                  pl.BlockSpec((1, w), lambda b, t: (0, 0))],
        out_specs=[pl.BlockSpec((tt, w), lambda b, t: (b * nt + t, 0)),
                   pl.BlockSpec((1, HEADS, HEAD_DIM, HEAD_DIM), lambda b, t: (b, 0, 0, 0))],
        out_shape=[jax.ShapeDtypeStruct((m, w), BF16),
                   jax.ShapeDtypeStruct((batch, HEADS, HEAD_DIM, HEAD_DIM), F32)],
        scratch_shapes=[pltpu.VMEM((HEADS, HEAD_DIM, HEAD_DIM), F32)],
        compiler_params=_params(est),
        name="hgrn2_prompt",
    )(proj, proj, proj, proj, lb, gn)


def _hgrn_sample_kernel(nb, qt_ref, ft_ref, v_ref, og_ref, lbc_ref, gn_ref, s0_ref, oa_ref, s_ref, o_scr):
    lbc = lbc_ref[...]
    f_t = lbc + (1.0 - lbc) * jax.nn.sigmoid(ft_ref[...])
    q_t = qt_ref[...]
    v = v_ref[...].astype(F32)
    for b in range(nb):
        v_b = v[b:b + 1, :]
        s_new = v_b + f_t[:, b:b + 1] * (s0_ref[b, 0] - v_b)
        s_ref[b, 0] = s_new
        o_scr[b:b + 1, :] = jnp.sum(q_t[:, b:b + 1] * s_new, axis=0, keepdims=True)
    o = o_scr[...]
    o = o * lax.rsqrt(jnp.mean(o * o, axis=-1, keepdims=True) + RMS_EPS) * gn_ref[...]
    oa_ref[...] = (o * jax.nn.silu(og_ref[...].astype(F32))).astype(BF16)


def _hgrn_sample(q_t, f_t, proj, lb_col, gn, s0):
    nb = s0.shape[0]
    hd = HEAD_DIM
    est = 4 * nb * hd * hd * 4 + 16 * nb * hd * 4
    return pl.pallas_call(
        functools.partial(_hgrn_sample_kernel, nb),
        grid=(HEADS,),
        in_specs=[pl.BlockSpec((hd, nb), lambda h: (h, 0)),
                  pl.BlockSpec((hd, nb), lambda h: (h, 0)),
                  pl.BlockSpec((nb, hd), lambda h: (0, OFF_I // hd + h)),
                  pl.BlockSpec((nb, hd), lambda h: (0, OFF_OG // hd + h)),
                  pl.BlockSpec((hd, 1), lambda h: (h, 0)),
                  pl.BlockSpec((1, hd), lambda h: (0, h)),
                  pl.BlockSpec((nb, 1, hd, hd), lambda h: (0, h, 0, 0))],
        out_specs=[pl.BlockSpec((nb, hd), lambda h: (0, h)),
                   pl.BlockSpec((nb, 1, hd, hd), lambda h: (0, h, 0, 0))],
        out_shape=[jax.ShapeDtypeStruct((nb, HGRN_WIDTH), BF16),
                   jax.ShapeDtypeStruct(s0.shape, F32)],
        scratch_shapes=[pltpu.VMEM((nb, hd), F32)],
        compiler_params=_params(est),
        name="hgrn2_sample",
    )(q_t, f_t, proj, proj, lb_col, gn, s0)


def _cmul(ar, ai, br, bi):
    return ar * br - ai * bi, ar * bi + ai * br


def _gelu_exact(x):
    return 0.5 * x * (1.0 + lax.erf(x * math.sqrt(0.5)))


def _s5_discretise_kernel(lre_ref, lim_ref, ldt_ref, pre_ref, pim_ref, zre_ref, zim_ref):
    lam_re, lam_im = lre_ref[...], lim_ref[...]
    dt = jnp.exp(ldt_ref[...])
    mag = jnp.exp(lam_re * dt)
    ab_re = mag * jnp.cos(lam_im * dt)
    ab_im = mag * jnp.sin(lam_im * dt)
    den = lam_re * lam_re + lam_im * lam_im
    nr, ni = ab_re - 1.0, ab_im
    zre_ref[...] = (nr * lam_re + ni * lam_im) / den
    zim_ref[...] = (ni * lam_re - nr * lam_im) / den
    pr, pi = ab_re, ab_im
    for kk in range(N_POW):
        pre_ref[kk] = pr
        pim_ref[kk] = pi
        pr, pi = _cmul(pr, pi, pr, pi)


def _s5_discretise(lam_re, lam_im, log_dt):
    g, n = lam_re.shape
    return pl.pallas_call(
        _s5_discretise_kernel,
        out_shape=[jax.ShapeDtypeStruct((N_POW, g, n), F32), jax.ShapeDtypeStruct((N_POW, g, n), F32),
                   jax.ShapeDtypeStruct((g, n), F32), jax.ShapeDtypeStruct((g, n), F32)],
        name="s5_discretise",
    )(lam_re, lam_im, log_dt)


def _s5_weights_kernel(gb, prow_re, prow_im, pcol_re, pcol_im, zre_ref, zim_ref, btre_ref, btim_ref,
                       ctre_ref, ctim_ref, wt_ref, wx_ref, wc_ref, apw_ref):
    fl = S5_FLAT
    srow = lax.broadcasted_iota(jnp.int32, (fl, 1), 0) // S5_GROUP
    tlan = lax.broadcasted_iota(jnp.int32, (1, fl), 1) // S5_GROUP
    jrow = lax.broadcasted_iota(jnp.int32, (S5_STATE, 1), 0)
    for gi in range(gb):
        zr, zi = zre_ref[gi], zim_ref[gi]
        bbr, bbi = _cmul(zr, zi, btre_ref[gi], btim_ref[gi])
        xr = jnp.concatenate([bbr] * S5_CHUNK, axis=0)
        xi = jnp.concatenate([bbi] * S5_CHUNK, axis=0)
        e_x = (S5_CHUNK - 1) - srow
        for kk in range(4):
            ar, ai = prow_re[gi, kk:kk + 1, :], prow_im[gi, kk:kk + 1, :]
            yr, yi = _cmul(xr, xi, ar, ai)
            hit = ((e_x >> kk) & 1) == 1
            xr, xi = jnp.where(hit, yr, xr), jnp.where(hit, yi, xi)
        wx_ref[gi] = jnp.concatenate([xr, xi], axis=1).astype(BF16)
        cr, ci = ctre_ref[gi], ctim_ref[gi]
        for kk in range(4):
            ar, ai = pcol_re[gi, :, kk:kk + 1], pcol_im[gi, :, kk:kk + 1]
            yr, yi = _cmul(cr, ci, ar, ai)
            hit = ((tlan >> kk) & 1) == 1
            cr, ci = jnp.where(hit, yr, cr), jnp.where(hit, yi, ci)
        g0 = jnp.concatenate([cr, -ci], axis=0)
        c1r, c1i = _cmul(cr, ci, pcol_re[gi, :, 0:1], pcol_im[gi, :, 0:1])
        wc_ref[gi] = jnp.concatenate([c1r, -c1i], axis=0).astype(BF16)
        kflat = jnp.dot(jnp.concatenate([bbr, bbi], axis=1), g0, preferred_element_type=F32,
                        precision=lax.Precision.HIGHEST)
        pieces = [kflat]
        for s in range(1, S5_CHUNK):
            pieces.append(jnp.concatenate([jnp.zeros((S5_GROUP, s * S5_GROUP), F32),
                                           kflat[:, :fl - s * S5_GROUP]], axis=1))
        wt_ref[gi] = jnp.concatenate(pieces, axis=0).astype(BF16)
        pr = jnp.ones((S5_STATE, S5_STATE), F32)
        pi = jnp.zeros((S5_STATE, S5_STATE), F32)
        for kk in range(6):
            ar, ai = prow_re[gi, 4 + kk:5 + kk, :], prow_im[gi, 4 + kk:5 + kk, :]
            yr, yi = _cmul(pr, pi, ar, ai)
            hit = ((jrow >> kk) & 1) == 1
            pr, pi = jnp.where(hit, yr, pr), jnp.where(hit, yi, pi)
        apw_ref[gi] = jnp.concatenate([pr, pi], axis=1)


def _s5_weights(prow_re, prow_im, pcol_re, pcol_im, z_re, z_im, bt_re, bt_im, ct_re, ct_im, gb):
    g = S5_GROUPS
    n, p, fl = S5_STATE, S5_GROUP, S5_FLAT
    b3 = lambda a, b: pl.BlockSpec((gb, a, b), lambda i: (i, 0, 0))
    return pl.pallas_call(
        functools.partial(_s5_weights_kernel, gb),
        grid=(g // gb,),
        in_specs=[b3(N_POW, n), b3(N_POW, n), b3(n, N_POW), b3(n, N_POW), b3(1, n), b3(1, n),
                  b3(p, n), b3(p, n), b3(n, fl), b3(n, fl)],
        out_specs=[b3(fl, fl), b3(fl, 2 * n), b3(2 * n, fl), b3(n, 2 * n)],
        out_shape=[jax.ShapeDtypeStruct((g, fl, fl), BF16), jax.ShapeDtypeStruct((g, fl, 2 * n), BF16),
                   jax.ShapeDtypeStruct((g, 2 * n, fl), BF16), jax.ShapeDtypeStruct((g, n, 2 * n), F32)],
        name="s5_chunk_weights",
    )(prow_re, prow_im, pcol_re, pcol_im, z_re, z_im, bt_re, bt_im, ct_re, ct_im)


LANE_GROUPS = 128 // S5_GROUP
RELAYOUT_ROWS = 64


def _block_transpose8(xs):
    lane_blk = lax.broadcasted_iota(jnp.int32, (1, 128), 1) // S5_GROUP
    xs = list(xs)
    for d in (4, 2, 1):
        upper = (lane_blk & d) != 0
        nxt = list(xs)
        for a in range(8):
            if a & d:
                continue
            lo, hi = xs[a], xs[a + d]
            nxt[a] = jnp.where(upper, pltpu.roll(hi, S5_GROUP * d, axis=1), lo)
            nxt[a + d] = jnp.where(upper, hi, pltpu.roll(lo, 128 - S5_GROUP * d, axis=1))
        xs = nxt
    return xs


def _s5_prompt_kernel(nseq, u_ref, wt_ref, wx_ref, wc_ref, apw_ref, a16_ref, d_ref, z_ref, hend_ref,
                      tok_scr, flat_scr, x_scr, h_scr):
    n = S5_STATE
    gb = LANE_GROUPS
    rows = u_ref.shape[0] // S5_CHUNK
    slab = 2 * nseq
    nlo = rows // slab
    rb = RELAYOUT_ROWS
    halves = S5_FLAT // 128

    tok_scr[...] = u_ref[...].astype(F32)

    def to_flat(r, carry):
        r0 = pl.multiple_of(r * rb, rb)
        for sh in range(halves):
            xs = [tok_scr[pl.ds(r0 * S5_CHUNK + 8 * sh + s, rb, stride=S5_CHUNK), :] for s in range(8)]
            ys = _block_transpose8(xs)
            for gi in range(gb):
                flat_scr[gi, pl.ds(r0, rb), sh * 128:(sh + 1) * 128] = ys[gi]
        return carry

    lax.fori_loop(0, rows // rb, to_flat, 0)

    odd = (lax.broadcasted_iota(jnp.int32, (slab, 1), 0) % 2) == 1
    for gi in range(gb):
        u = flat_scr[gi]
        u_b = u.astype(BF16)
        x_scr[gi] = _bdot(u_b, wx_ref[gi])
        ar, ai = a16_ref[gi, :, :n], a16_ref[gi, :, n:]
        hr = jnp.zeros((slab, n), F32)
        hi = jnp.zeros((slab, n), F32)
        for j in range(nlo):
            rs = pl.ds(j, slab, stride=nlo)
            h_scr[gi, rs, :] = jnp.concatenate([hr, hi], axis=1)
            xj = x_scr[gi, rs, :]
            nr, ni = _cmul(hr, hi, ar, ai)
            hr = nr + xj[:, :n]
            hi = ni + xj[:, n:]
        mr = jnp.where(odd, pltpu.roll(hr, 1, axis=0), 0.0)
        mi = jnp.where(odd, pltpu.roll(hi, 1, axis=0), 0.0)
        pr, pi = apw_ref[gi, :, :n], apw_ref[gi, :, n:]
        for k in range(1, slab, 2):
            cr, ci = _cmul(pr, pi, mr[k:k + 1], mi[k:k + 1])
            h_scr[gi, k * nlo:(k + 1) * nlo, :n] += cr
            h_scr[gi, k * nlo:(k + 1) * nlo, n:] += ci
        lr, li = _cmul(pr[nlo - 1:nlo], pi[nlo - 1:nlo], ar, ai)
        er, ei = _cmul(mr, mi, lr, li)
        hend_ref[gi] = jnp.concatenate([hr + er, hi + ei], axis=1)
        y = _bdot(u_b, wt_ref[gi]) + _bdot(h_scr[gi].astype(BF16), wc_ref[gi]) + d_ref[gi] * u
        flat_scr[gi] = _gelu_exact(y)

    def to_tokens(r, carry):
        r0 = pl.multiple_of(r * rb, rb)
        for sh in range(halves):
            ys = [flat_scr[gi, pl.ds(r0, rb), sh * 128:(sh + 1) * 128] for gi in range(gb)]
            xs = _block_transpose8(ys)
            for s in range(8):
                tok_scr[pl.ds(r0 * S5_CHUNK + 8 * sh + s, rb, stride=S5_CHUNK), :] = xs[s]
        return carry

    lax.fori_loop(0, rows // rb, to_tokens, 0)
    z_ref[...] = tok_scr[...].astype(BF16)


def _s5_prompt(proj, wt, wx, wc, apw, a16, d_flat, nseq):
    m = proj.shape[0]
    g, n, fl, gb = S5_GROUPS, S5_STATE, S5_FLAT, LANE_GROUPS
    rows = m // S5_CHUNK
    b3 = lambda a, b: pl.BlockSpec((gb, a, b), lambda i: (i, 0, 0))
    est = (4 * m * 128 * 2 + m * 128 * 4 + gb * rows * (fl + 4 * n) * 4 + 6 * rows * fl * 4
           + 2 * gb * (fl * fl + 4 * n * fl) * 2)
    return pl.pallas_call(
        functools.partial(_s5_prompt_kernel, nseq),
        grid=(g // gb,),
        in_specs=[pl.BlockSpec((m, 128), lambda i: (0, OFF_U // 128 + i)),
                  b3(fl, fl), b3(fl, 2 * n), b3(2 * n, fl), b3(n, 2 * n), b3(1, 2 * n), b3(1, fl)],
        out_specs=[pl.BlockSpec((m, 128), lambda i: (0, i)), b3(2 * nseq, 2 * n)],
        out_shape=[jax.ShapeDtypeStruct((m, S5_WIDTH), BF16), jax.ShapeDtypeStruct((g, 2 * nseq, 2 * n), F32)],
        scratch_shapes=[pltpu.VMEM((m, 128), F32), pltpu.VMEM((gb, rows, fl), F32),
                        pltpu.VMEM((gb, rows, 2 * n), F32), pltpu.VMEM((gb, rows, 2 * n), F32)],
        compiler_params=_params(est),
        name="s5_prompt",
    )(proj, wt, wx, wc, apw, a16, d_flat)


def _s5_sample_kernel(gb, u_ref, bst_ref, ctre_ref, ctim_ref, a1_ref, hre_ref, him_ref, d_ref,
                      z_ref, ore_ref, oim_ref):
    n = S5_STATE
    u = u_ref[...].astype(F32)
    lane_g = lax.broadcasted_iota(jnp.int32, (1, gb * S5_GROUP), 1) // S5_GROUP
    bst = bst_ref[...]
    y = d_ref[...] * u
    for gi in range(gb):
        mine = lane_g == gi
        bu = _bdot(jnp.where(mine, u, 0.0).astype(BF16), bst)
        ar, ai = a1_ref[gi, :, :n], a1_ref[gi, :, n:]
        nr, ni = _cmul(hre_ref[gi], him_ref[gi], ar, ai)
        hr = nr + bu[:, :n]
        hi = ni + bu[:, n:]
        ore_ref[gi] = hr
        oim_ref[gi] = hi
        cmat = jnp.concatenate([jnp.where(mine, ctre_ref[gi], 0.0), jnp.where(mine, -ctim_ref[gi], 0.0)], axis=0)
        y = y + _bdot(jnp.concatenate([hr, hi], axis=1).astype(BF16), cmat.astype(BF16))
    z_ref[...] = _gelu_exact(y).astype(BF16)


def _s5_sample(proj, bstack, ct_re, ct_im, a1, h_re, h_im, d_row, gb):
    nb = proj.shape[0]
    g, n, p = S5_GROUPS, S5_STATE, S5_GROUP
    lanes = gb * p
    b3 = lambda a, b: pl.BlockSpec((gb, a, b), lambda i: (i, 0, 0))
    st = b3(nb, n)
    return pl.pallas_call(
        functools.partial(_s5_sample_kernel, gb),
        grid=(g // gb,),
        in_specs=[pl.BlockSpec((nb, lanes), lambda i: (0, OFF_U // lanes + i)),
                  pl.BlockSpec((lanes, 2 * n), lambda i: (i, 0)),
                  b3(n, lanes), b3(n, lanes), b3(1, 2 * n), st, st,
                  pl.BlockSpec((1, lanes), lambda i: (0, i))],
        out_specs=[pl.BlockSpec((nb, lanes), lambda i: (0, i)), st, st],
        out_shape=[jax.ShapeDtypeStruct((nb, S5_WIDTH), BF16),
                   jax.ShapeDtypeStruct((g, nb, n), F32), jax.ShapeDtypeStruct((g, nb, n), F32)],
        name="s5_sample",
    )(proj, bstack, ct_re, ct_im, a1, h_re, h_im, d_row)


def _dense_tail(x, proj, o_a, z, w, tm):
    o_b = _glu(z, w['w_s5_glu'], tm, 512)
    merged = _merge(o_a, o_b, w['w_proj_a'], w['w_proj_b'], proj, tm, 512)
    x1 = _outproj(merged, w['w_out'], x, tm, 512)
    act = _ffn_up(x1, w['norm2_g'], w['w_ffn_up'], tm, 256)
    return _ffn_down(act, w['w_ffn_down'], x1, w['final_norm_g'], min(tm, 1024), 512)


def kernel(x_prompt, x_sample, state_hgrn, state_s5_re, state_s5_im, lb_logits, norm1_g, w_in, hgrn_norm_g,
           s5_lam_re, s5_lam_im, s5_log_dt, s5_B_re, s5_B_im, s5_C_re, s5_C_im, s5_D, w_s5_glu, w_proj_a,
           w_proj_b, w_out, norm2_g, w_ffn_up, w_ffn_down, final_norm_g):
    l = 0
    bp, seq, d = x_prompt.shape
    nb = x_sample.shape[0]
    g, n, p = S5_GROUPS, S5_STATE, S5_GROUP
    w = {'w_s5_glu': w_s5_glu[l], 'w_proj_a': w_proj_a[l], 'w_proj_b': w_proj_b[l], 'w_out': w_out[l],
         'norm2_g': norm2_g[l][None, :], 'w_ffn_up': w_ffn_up[l], 'w_ffn_down': w_ffn_down[l].astype(BF16),
         'final_norm_g': final_norm_g[None, :]}
    g1 = norm1_g[l][None, :]
    gn = hgrn_norm_g[l][None, :]

    xp = x_prompt.reshape(bp * seq, d)
    xs = x_sample.reshape(nb, d)
    proj_p = _inproj(xp, g1, w_in[l], 2048, 512)
    proj_s = _inproj(xs, g1, w_in[l], nb, 512)

    lb = _lower_bound(lb_logits)
    oa_p, sh_p = _hgrn_prompt(proj_p, lb, gn, bp, seq, 256)
    q_t = proj_s[:, OFF_Q:OFF_Q + HGRN_WIDTH].astype(F32).T
    f_t = proj_s[:, OFF_F:OFF_F + HGRN_WIDTH].astype(F32).T
    oa_s, sh_s = _hgrn_sample(q_t, f_t, proj_s, lb.reshape(HGRN_WIDTH, 1), gn, state_hgrn[l])

    pw_re, pw_im, z_re, z_im = _s5_discretise(s5_lam_re[l], s5_lam_im[l], s5_log_dt[l][:, None])
    prow_re, prow_im = pw_re.transpose(1, 0, 2), pw_im.transpose(1, 0, 2)
    pcol_re, pcol_im = pw_re.transpose(1, 2, 0), pw_im.transpose(1, 2, 0)
    bt_re, bt_im = s5_B_re[l].transpose(0, 2, 1), s5_B_im[l].transpose(0, 2, 1)
    ct_re = jnp.tile(s5_C_re[l].transpose(0, 2, 1), (1, 1, S5_CHUNK))
    ct_im = jnp.tile(s5_C_im[l].transpose(0, 2, 1), (1, 1, S5_CHUNK))
    wt, wx, wc, apw = _s5_weights(prow_re, prow_im, pcol_re, pcol_im, z_re[:, None, :], z_im[:, None, :],
                                  bt_re, bt_im, ct_re, ct_im, 8)
    d_gp = s5_D[l].reshape(g, 1, p)
    d_flat = jnp.tile(d_gp, (1, 1, S5_CHUNK))
    a16 = jnp.concatenate([prow_re[:, 4:5, :], prow_im[:, 4:5, :]], axis=2)
    a1 = jnp.concatenate([prow_re[:, 0:1, :], prow_im[:, 0:1, :]], axis=2)

    z_p, hend = _s5_prompt(proj_p, wt, wx, wc, apw, a16, d_flat, bp)
    hend = hend[:, 1::2, :]
    s5re_p = hend[:, :, :n].transpose(1, 0, 2)
    s5im_p = hend[:, :, n:].transpose(1, 0, 2)

    bstack = wx[:, S5_FLAT - p:, :].reshape(g * p, 2 * n)
    gbs = 128 // p
    z_s, s5re_s, s5im_s = _s5_sample(proj_s, bstack, ct_re[:, :, :gbs * p], ct_im[:, :, :gbs * p], a1,
                                     state_s5_re[l].transpose(1, 0, 2), state_s5_im[l].transpose(1, 0, 2),
                                     s5_D[l][None, :], gbs)

    y_p = _dense_tail(xp, proj_p, oa_p, z_p, w, 2048)
    y_s = _dense_tail(xs, proj_s, oa_s, z_s, w, nb)

    return (y_p.reshape(bp, seq, d), y_s.reshape(nb, 1, d),
            sh_p[None], s5re_p[None], s5im_p[None],
            sh_s[None], s5re_s.transpose(1, 0, 2)[None], s5im_s.transpose(1, 0, 2)[None])
```

```python
import functools
import math

import jax
import jax.numpy as jnp
from jax import lax
from jax.experimental import pallas as pl
from jax.experimental.pallas import tpu as pltpu

F32 = jnp.float32
BF16 = jnp.bfloat16

D_MODEL = 2048
HGRN_WIDTH = 1024
HEAD_DIM = 128
HEADS = 8
S5_WIDTH = 1024
S5_GROUPS = 64
S5_GROUP = 16
S5_STATE = 64
FFN_HIDDEN = 5632
IN_PROJ_WIDTH = 4 * HGRN_WIDTH + S5_WIDTH + 2 * D_MODEL
RMS_EPS = 1e-6

OFF_Q, OFF_F, OFF_I, OFF_OG = 0, HGRN_WIDTH, 2 * HGRN_WIDTH, 3 * HGRN_WIDTH
OFF_U = 4 * HGRN_WIDTH
OFF_GA = OFF_U + S5_WIDTH
OFF_GB = OFF_GA + D_MODEL

HGRN_SUB = 16
HGRN_CHUNK = 128
S5_CHUNK = 16
S5_FLAT = S5_CHUNK * S5_GROUP
N_POW = 10

V7X_VMEM_BYTES = 64 * 1024 * 1024
VMEM_LIMIT_CAP = V7X_VMEM_BYTES - 6 * 1024 * 1024


def _params(est_bytes):
    limit = min(max(int(est_bytes * 1.25) + (4 << 20), 32 << 20), VMEM_LIMIT_CAP)
    return pltpu.CompilerParams(vmem_limit_bytes=limit)


def _bdot(a, b):
    return jnp.dot(a, b, preferred_element_type=F32)


def _rms_rows(x, g):
    ms = jnp.mean(x * x, axis=-1, keepdims=True)
    return x * lax.rsqrt(ms + RMS_EPS) * g


def _inproj_kernel(x_ref, g_ref, w_ref, o_ref, h_scr):
    @pl.when(pl.program_id(1) == 0)
    def _():
        h_scr[...] = _rms_rows(x_ref[...], g_ref[...]).astype(BF16)

    o_ref[...] = _bdot(h_scr[...], w_ref[...].astype(BF16)).astype(BF16)


def _inproj(x, g, w, tm, tn):
    m, k = x.shape
    n = w.shape[1]
    single = pl.Buffered(1) if tm >= 2048 else None
    est = ((1 if single else 2) * tm * k * 4 + 2 * k * tn * 4 + 2 * tm * tn * 2 + tm * k * 2 + k * tn * 2
           + tm * tn * 6)
    return pl.pallas_call(
        _inproj_kernel,
        grid=(m // tm, n // tn),
        in_specs=[pl.BlockSpec((tm, k), lambda i, j: (i, 0), pipeline_mode=single),
                  pl.BlockSpec((1, k), lambda i, j: (0, 0)),
                  pl.BlockSpec((k, tn), lambda i, j: (0, j))],
        out_specs=pl.BlockSpec((tm, tn), lambda i, j: (i, j)),
        out_shape=jax.ShapeDtypeStruct((m, n), BF16),
        scratch_shapes=[pltpu.VMEM((tm, k), BF16)],
        compiler_params=_params(est),
        name="inproj",
    )(x, g, w)


def _glu_kernel(z_ref, wa_ref, wb_ref, o_ref):
    z = z_ref[...]
    a = _bdot(z, wa_ref[...].astype(BF16))
    b = _bdot(z, wb_ref[...].astype(BF16))
    o_ref[...] = (a * jax.nn.sigmoid(b)).astype(BF16)


def _glu(z, w, tm, tn):
    m, k = z.shape
    n = w.shape[1] // 2
    nj = n // tn
    est = 2 * tm * k * 2 + 4 * k * tn * 4 + 2 * tm * tn * 2 + 2 * k * tn * 2 + 3 * tm * tn * 4
    return pl.pallas_call(
        _glu_kernel,
        grid=(m // tm, nj),
        in_specs=[pl.BlockSpec((tm, k), lambda i, j: (i, 0)),
                  pl.BlockSpec((k, tn), lambda i, j: (0, j)),
                  pl.BlockSpec((k, tn), lambda i, j: (0, j + nj))],
        out_specs=pl.BlockSpec((tm, tn), lambda i, j: (i, j)),
        out_shape=jax.ShapeDtypeStruct((m, n), BF16),
        compiler_params=_params(est),
        name="s5_glu",
    )(z, w, w)


def _merge_kernel(oa_ref, ob_ref, wa_ref, wb_ref, ga_ref, gb_ref, o_ref):
    a = _bdot(oa_ref[...], wa_ref[...].astype(BF16))
    b = _bdot(ob_ref[...], wb_ref[...].astype(BF16))
    ga, gb = ga_ref[...].astype(F32), gb_ref[...].astype(F32)
    o_ref[...] = (jax.nn.sigmoid(ga) * a + jax.nn.sigmoid(gb) * b).astype(BF16)


def _merge(o_a, o_b, w_a, w_b, proj, tm, tn):
    m, k = o_a.shape
    n = w_a.shape[1]
    ja, jb = OFF_GA // tn, OFF_GB // tn
    est = 4 * tm * k * 2 + 4 * k * tn * 4 + 4 * tm * tn * 4 + 2 * tm * tn * 2 + 2 * k * tn * 2 + 3 * tm * tn * 4
    return pl.pallas_call(
        _merge_kernel,
        grid=(m // tm, n // tn),
        in_specs=[pl.BlockSpec((tm, k), lambda i, j: (i, 0)),
                  pl.BlockSpec((tm, k), lambda i, j: (i, 0)),
                  pl.BlockSpec((k, tn), lambda i, j: (0, j)),
                  pl.BlockSpec((k, tn), lambda i, j: (0, j)),
                  pl.BlockSpec((tm, tn), lambda i, j: (i, j + ja)),
                  pl.BlockSpec((tm, tn), lambda i, j: (i, j + jb))],
        out_specs=pl.BlockSpec((tm, tn), lambda i, j: (i, j)),
        out_shape=jax.ShapeDtypeStruct((m, n), BF16),
        compiler_params=_params(est),
        name="gated_merge",
    )(o_a, o_b, w_a, w_b, proj, proj)


def _outproj_kernel(m_ref, w_ref, x_ref, o_ref):
    o_ref[...] = x_ref[...] + _bdot(m_ref[...], w_ref[...].astype(BF16))


def _outproj(merged, w, x, tm, tn):
    m, k = merged.shape
    n = w.shape[1]
    est = 2 * tm * k * 2 + 2 * k * tn * 4 + 4 * tm * tn * 4 + k * tn * 2 + tm * tn * 4
    return pl.pallas_call(
        _outproj_kernel,
        grid=(m // tm, n // tn),
        in_specs=[pl.BlockSpec((tm, k), lambda i, j: (i, 0)),
                  pl.BlockSpec((k, tn), lambda i, j: (0, j)),
                  pl.BlockSpec((tm, tn), lambda i, j: (i, j))],
        out_specs=pl.BlockSpec((tm, tn), lambda i, j: (i, j)),
        out_shape=jax.ShapeDtypeStruct((m, n), F32),
        compiler_params=_params(est),
        name="out_proj",
    )(merged, w, x)


def _ffn_up_kernel(cast_down, x_ref, g_ref, wa_ref, wb_ref, *rest):
    if cast_down:
        wd_ref, o_ref, wdb_ref, h_scr = rest
        wdb_ref[...] = wd_ref[...].astype(BF16)
    else:
        o_ref, h_scr = rest

    @pl.when(pl.program_id(1) == 0)
    def _():
        h_scr[...] = _rms_rows(x_ref[...], g_ref[...]).astype(BF16)

    h = h_scr[...]
    a = _bdot(h, wa_ref[...].astype(BF16))
    b = _bdot(h, wb_ref[...].astype(BF16))
    o_ref[...] = (jax.nn.silu(a) * b).astype(BF16)


def _ffn_up(x, g, w, tm, tn, w_down=None):
    m, k = x.shape
    n = w.shape[1] // 2
    nj = n // tn
    single = pl.Buffered(1) if tm >= 2048 else None
    est = ((1 if single else 2) * tm * k * 4 + 4 * k * tn * 4 + 2 * tm * tn * 2 + tm * k * 2 + 2 * k * tn * 2
           + 3 * tm * tn * 4)
    in_specs = [pl.BlockSpec((tm, k), lambda i, j: (i, 0), pipeline_mode=single),
                pl.BlockSpec((1, k), lambda i, j: (0, 0)),
                pl.BlockSpec((k, tn), lambda i, j: (0, j)),
                pl.BlockSpec((k, tn), lambda i, j: (0, j + nj))]
    out_specs = [pl.BlockSpec((tm, tn), lambda i, j: (i, j))]
    out_shape = [jax.ShapeDtypeStruct((m, n), BF16)]
    args = [x, g, w, w]
    if w_down is not None:
        rows = w_down.shape[0] // ((m // tm) * nj)
        slice_spec = pl.BlockSpec((rows, w_down.shape[1]), lambda i, j: (i * nj + j, 0))
        in_specs.append(slice_spec)
        out_specs.append(slice_spec)
        out_shape.append(jax.ShapeDtypeStruct(w_down.shape, BF16))
        args.append(w_down)
        est += 2 * rows * w_down.shape[1] * 6
    return pl.pallas_call(
        functools.partial(_ffn_up_kernel, w_down is not None),
        grid=(m // tm, nj),
        in_specs=in_specs,
        out_specs=out_specs,
        out_shape=out_shape,
        scratch_shapes=[pltpu.VMEM((tm, k), BF16)],
        compiler_params=_params(est),
        name="ffn_up",
    )(*args)


FFN_DOWN_COLS = 512


def _ffn_down_kernel(a_ref, w_ref, x_ref, g_ref, o_ref):
    kk = pl.program_id(1)

    @pl.when(kk == 0)
    def _():
        o_ref[...] = x_ref[...]

    a = a_ref[...]
    for n0 in range(0, o_ref.shape[1], FFN_DOWN_COLS):
        cols = slice(n0, n0 + FFN_DOWN_COLS)
        o_ref[:, cols] += _bdot(a, w_ref[:, cols])

    @pl.when(kk == pl.num_programs(1) - 1)
    def _():
        o_ref[...] = _rms_rows(o_ref[...], g_ref[...])


def _ffn_down(act, w, x, g, tm, tk):
    m, k = act.shape
    n = w.shape[1]
    est = 2 * tm * tk * 2 + 2 * tk * n * 2 + 4 * tm * n * 4 + 2 * tm * n * 4
    return pl.pallas_call(
        _ffn_down_kernel,
        grid=(m // tm, k // tk),
        in_specs=[pl.BlockSpec((tm, tk), lambda i, kk: (i, kk)),
                  pl.BlockSpec((tk, n), lambda i, kk: (kk, 0)),
                  pl.BlockSpec((tm, n), lambda i, kk: (i, 0)),
                  pl.BlockSpec((1, n), lambda i, kk: (0, 0))],
        out_specs=pl.BlockSpec((tm, n), lambda i, kk: (i, 0)),
        out_shape=jax.ShapeDtypeStruct((m, n), F32),
        compiler_params=_params(est),
        name="ffn_down_final_norm",
    )(act, w, x, g)


def _lower_bound_kernel(lb_ref, o_ref):
    x = lb_ref[...]
    mx = jnp.max(x, axis=0, keepdims=True)
    e = jnp.exp(x - mx)
    o_ref[...] = e[0:1, :] / jnp.sum(e, axis=0, keepdims=True)


def _lower_bound(lb_logits):
    return pl.pallas_call(
        _lower_bound_kernel,
        out_shape=jax.ShapeDtypeStruct((1, HGRN_WIDTH), F32),
        name="hgrn_lower_bound",
    )(lb_logits)


def _split3(x):
    hi = x.astype(BF16)
    r = x - hi.astype(F32)
    mid = r.astype(BF16)
    lo = (r - mid.astype(F32)).astype(BF16)
    return hi, mid, lo


def _rows_ref(b, block, pick):
    parts = []
    for s0 in range(0, HGRN_CHUNK, block):
        parts.append(jnp.broadcast_to(b[s0 + pick:s0 + pick + 1, :], (block, b.shape[1])))
    return parts[0] if len(parts) == 1 else jnp.concatenate(parts, axis=0)


def _hgrn_prompt_kernel(n_chunks, q_ref, f_ref, i_ref, og_ref, lb_ref, gn_ref, oa_ref, s_ref, st_scr):
    t = pl.program_id(1)

    @pl.when(t == 0)
    def _():
        st_scr[...] = jnp.zeros(st_scr.shape, F32)

    c_ = HGRN_CHUNK
    row = lax.broadcasted_iota(jnp.int32, (c_, c_), 0)
    col = lax.broadcasted_iota(jnp.int32, (c_, c_), 1)
    tri = (col <= row).astype(BF16)
    m_diag = (row // HGRN_SUB == col // HGRN_SUB) & (col <= row)
    levels = (128, 64, 32)
    m_lvl = [(row // bs == col // bs) & (row % bs >= bs // 2) & (col % bs < bs // 2) for bs in levels]
    rcol = lax.broadcasted_iota(jnp.int32, (c_, 1), 0)
    second = [(rcol % bs) >= bs // 2 for bs in levels]

    lb = lb_ref[...]
    gn = gn_ref[...]

    def chunk(c, carry):
        r0 = pl.multiple_of(c * c_, c_)
        rows = pl.ds(r0, c_)
        f = lb + (1.0 - lb) * jax.nn.sigmoid(f_ref[rows, :].astype(F32))
        g = jnp.log(f)
        k_all = 1.0 - f
        g_hi, g_mid, g_lo = _split3(g)
        b = _bdot(tri, g_hi) + _bdot(tri, g_mid) + _bdot(tri, g_lo)
        b_last = b[c_ - 1:c_, :]
        e_in = jnp.exp(b)
        e_upd = jnp.exp(b_last - b)
        dec_all = jnp.exp(b_last)
        b_mid = _rows_ref(b, HGRN_SUB, HGRN_SUB // 2)
        e_dq = jnp.exp(b - b_mid)
        e_dk = jnp.exp(b_mid - b)
        e_lvl = []
        for li, bs in enumerate(levels):
            d = b - _rows_ref(b, bs, bs // 2 - 1)
            e_lvl.append(jnp.exp(jnp.where(second[li], d, -d)))
        q_all = q_ref[rows, :].astype(F32)
        v_all = i_ref[rows, :].astype(F32)
        og = og_ref[rows, :].astype(F32)
        for h in range(HEADS):
            ls = slice(h * HEAD_DIM, (h + 1) * HEAD_DIM)
            q, k, v = q_all[:, ls], k_all[:, ls], v_all[:, ls]
            nt = (((1,), (1,)), ((), ()))
            sc = lax.dot_general((q * e_dq[:, ls]).astype(BF16), (k * e_dk[:, ls]).astype(BF16), nt,
                                 preferred_element_type=F32)
            scores = jnp.where(m_diag, sc, 0.0)
            for li in range(len(levels)):
                e = e_lvl[li][:, ls]
                sc = lax.dot_general((q * e).astype(BF16), (k * e).astype(BF16), nt, preferred_element_type=F32)
                scores = jnp.where(m_lvl[li], sc, scores)
            v_b = v.astype(BF16)
            st = st_scr[h]
            o = _bdot(scores.astype(BF16), v_b)
            o = o + lax.dot_general((q * e_in[:, ls]).astype(BF16), st.astype(BF16), nt, preferred_element_type=F32)
            upd = lax.dot_general(v_b, (k * e_upd[:, ls]).astype(BF16), (((0,), (0,)), ((), ())),
                                  preferred_element_type=F32)
            st_scr[h] = dec_all[:, ls] * st + upd
            o = o * lax.rsqrt(jnp.mean(o * o, axis=-1, keepdims=True) + RMS_EPS) * gn[:, ls]
            oa_ref[rows, ls] = (o * jax.nn.silu(og[:, ls])).astype(BF16)
        return carry

    lax.fori_loop(0, n_chunks, chunk, 0)

    @pl.when(t == pl.num_programs(1) - 1)
    def _():
        for h in range(HEADS):
            s_ref[0, h] = st_scr[h].T


def _hgrn_prompt(proj, lb, gn, batch, seq, tt):
    m = batch * seq
    w = HGRN_WIDTH
    nt = seq // tt
    blk = lambda off: pl.BlockSpec((tt, w), lambda b, t, off=off: (b * nt + t, off // w))
    est = 8 * tt * w * 4 + 2 * tt * w * 2 + 3 * HEADS * HEAD_DIM * HEAD_DIM * 4 + 24 * HGRN_CHUNK * w * 4
    return pl.pallas_call(
        functools.partial(_hgrn_prompt_kernel, tt // HGRN_CHUNK),
        grid=(batch, nt),
        in_specs=[blk(OFF_Q), blk(OFF_F), blk(OFF_I), blk(OFF_OG),
                  pl.BlockSpec((1, w), lambda b, t: (0, 0)),
                  pl.BlockSpec((1, w), lambda b, t: (0, 0))],
        out_specs=[pl.BlockSpec((tt, w), lambda b, t: (b * nt + t, 0)),
                   pl.BlockSpec((1, HEADS, HEAD_DIM, HEAD_DIM), lambda b, t: (b, 0, 0, 0))],
        out_shape=[jax.ShapeDtypeStruct((m, w), BF16),
                   jax.ShapeDtypeStruct((batch, HEADS, HEAD_DIM, HEAD_DIM), F32)],
        scratch_shapes=[pltpu.VMEM((HEADS, HEAD_DIM, HEAD_DIM), F32)],
        compiler_params=_params(est),
        name="hgrn2_prompt",
    )(proj, proj, proj, proj, lb, gn)


def _hgrn_sample_kernel(nb, qt_ref, ft_ref, v_ref, og_ref, lbc_ref, gn_ref, s0_ref, oa_ref, s_ref, o_scr):
    lbc = lbc_ref[...]
    f_t = lbc + (1.0 - lbc) * jax.nn.sigmoid(ft_ref[...])
    q_t = qt_ref[...]
    v = v_ref[...].astype(F32)
    for b in range(nb):
        v_b = v[b:b + 1, :]
        s_new = v_b + f_t[:, b:b + 1] * (s0_ref[b, 0] - v_b)
        s_ref[b, 0] = s_new
        o_scr[b:b + 1, :] = jnp.sum(q_t[:, b:b + 1] * s_new, axis=0, keepdims=True)
    o = o_scr[...]
    o = o * lax.rsqrt(jnp.mean(o * o, axis=-1, keepdims=True) + RMS_EPS) * gn_ref[...]
    oa_ref[...] = (o * jax.nn.silu(og_ref[...].astype(F32))).astype(BF16)


def _hgrn_sample(q_t, f_t, proj, lb_col, gn, s0):
    nb = s0.shape[0]
    hd = HEAD_DIM
    est = 4 * nb * hd * hd * 4 + 16 * nb * hd * 4
    return pl.pallas_call(
        functools.partial(_hgrn_sample_kernel, nb),
        grid=(HEADS,),
        in_specs=[pl.BlockSpec((hd, nb), lambda h: (h, 0)),
                  pl.BlockSpec((hd, nb), lambda h: (h, 0)),
                  pl.BlockSpec((nb, hd), lambda h: (0, OFF_I // hd + h)),
                  pl.BlockSpec((nb, hd), lambda h: (0, OFF_OG // hd + h)),
                  pl.BlockSpec((hd, 1), lambda h: (h, 0)),
                  pl.BlockSpec((1, hd), lambda h: (0, h)),
                  pl.BlockSpec((nb, 1, hd, hd), lambda h: (0, h, 0, 0))],
        out_specs=[pl.BlockSpec((nb, hd), lambda h: (0, h)),
                   pl.BlockSpec((nb, 1, hd, hd), lambda h: (0, h, 0, 0))],
        out_shape=[jax.ShapeDtypeStruct((nb, HGRN_WIDTH), BF16),
                   jax.ShapeDtypeStruct(s0.shape, F32)],
        scratch_shapes=[pltpu.VMEM((nb, hd), F32)],
        compiler_params=_params(est),
        name="hgrn2_sample",
    )(q_t, f_t, proj, proj, lb_col, gn, s0)


def _cmul(ar, ai, br, bi):
    return ar * br - ai * bi, ar * bi + ai * br


def _gelu_exact(x):
    return 0.5 * x * (1.0 + lax.erf(x * math.sqrt(0.5)))


def _s5_discretise_kernel(lre_ref, lim_ref, ldt_ref, pre_ref, pim_ref, zre_ref, zim_ref):
    lam_re, lam_im = lre_ref[...], lim_ref[...]
    dt = jnp.exp(ldt_ref[...])
    mag = jnp.exp(lam_re * dt)
    ab_re = mag * jnp.cos(lam_im * dt)
    ab_im = mag * jnp.sin(lam_im * dt)
    den = lam_re * lam_re + lam_im * lam_im
    nr, ni = ab_re - 1.0, ab_im
    zre_ref[...] = (nr * lam_re + ni * lam_im) / den
    zim_ref[...] = (ni * lam_re - nr * lam_im) / den
    pr, pi = ab_re, ab_im
    for kk in range(N_POW):
        pre_ref[kk] = pr
        pim_ref[kk] = pi
        pr, pi = _cmul(pr, pi, pr, pi)


def _s5_discretise(lam_re, lam_im, log_dt):
    g, n = lam_re.shape
    return pl.pallas_call(
        _s5_discretise_kernel,
        out_shape=[jax.ShapeDtypeStruct((N_POW, g, n), F32), jax.ShapeDtypeStruct((N_POW, g, n), F32),
                   jax.ShapeDtypeStruct((g, n), F32), jax.ShapeDtypeStruct((g, n), F32)],
        name="s5_discretise",
    )(lam_re, lam_im, log_dt)


def _s5_weights_kernel(gb, prow_re, prow_im, pcol_re, pcol_im, zre_ref, zim_ref, btre_ref, btim_ref,
                       ctre_ref, ctim_ref, wt_ref, wx_ref, wc_ref, apw_ref):
    fl = S5_FLAT
    srow = lax.broadcasted_iota(jnp.int32, (fl, 1), 0) // S5_GROUP
    tlan = lax.broadcasted_iota(jnp.int32, (1, fl), 1) // S5_GROUP
    jrow = lax.broadcasted_iota(jnp.int32, (S5_STATE, 1), 0)
    for gi in range(gb):
        zr, zi = zre_ref[gi], zim_ref[gi]
        bbr, bbi = _cmul(zr, zi, btre_ref[gi], btim_ref[gi])
        xr = jnp.concatenate([bbr] * S5_CHUNK, axis=0)
        xi = jnp.concatenate([bbi] * S5_CHUNK, axis=0)
        e_x = (S5_CHUNK - 1) - srow
        for kk in range(4):
            ar, ai = prow_re[gi, kk:kk + 1, :], prow_im[gi, kk:kk + 1, :]
            yr, yi = _cmul(xr, xi, ar, ai)
            hit = ((e_x >> kk) & 1) == 1
            xr, xi = jnp.where(hit, yr, xr), jnp.where(hit, yi, xi)
        wx_ref[gi] = jnp.concatenate([xr, xi], axis=1).astype(BF16)
        cr, ci = ctre_ref[gi], ctim_ref[gi]
        for kk in range(4):
            ar, ai = pcol_re[gi, :, kk:kk + 1], pcol_im[gi, :, kk:kk + 1]
            yr, yi = _cmul(cr, ci, ar, ai)
            hit = ((tlan >> kk) & 1) == 1
            cr, ci = jnp.where(hit, yr, cr), jnp.where(hit, yi, ci)
        g0 = jnp.concatenate([cr, -ci], axis=0)
        c1r, c1i = _cmul(cr, ci, pcol_re[gi, :, 0:1], pcol_im[gi, :, 0:1])
        wc_ref[gi] = jnp.concatenate([c1r, -c1i], axis=0).astype(BF16)
        kflat = jnp.dot(jnp.concatenate([bbr, bbi], axis=1), g0, preferred_element_type=F32,
                        precision=lax.Precision.HIGHEST)
        pieces = [kflat]
        for s in range(1, S5_CHUNK):
            pieces.append(jnp.concatenate([jnp.zeros((S5_GROUP, s * S5_GROUP), F32),
                                           kflat[:, :fl - s * S5_GROUP]], axis=1))
        wt_ref[gi] = jnp.concatenate(pieces, axis=0).astype(BF16)
        pr = jnp.ones((S5_STATE, S5_STATE), F32)
        pi = jnp.zeros((S5_STATE, S5_STATE), F32)
        for kk in range(6):
            ar, ai = prow_re[gi, 4 + kk:5 + kk, :], prow_im[gi, 4 + kk:5 + kk, :]
            yr, yi = _cmul(pr, pi, ar, ai)
            hit = ((jrow >> kk) & 1) == 1
            pr, pi = jnp.where(hit, yr, pr), jnp.where(hit, yi, pi)
        apw_ref[gi] = jnp.concatenate([pr, pi], axis=1)


def _s5_weights(prow_re, prow_im, pcol_re, pcol_im, z_re, z_im, bt_re, bt_im, ct_re, ct_im, gb):
    g = S5_GROUPS
    n, p, fl = S5_STATE, S5_GROUP, S5_FLAT
    b3 = lambda a, b: pl.BlockSpec((gb, a, b), lambda i: (i, 0, 0))
    return pl.pallas_call(
        functools.partial(_s5_weights_kernel, gb),
        grid=(g // gb,),
        in_specs=[b3(N_POW, n), b3(N_POW, n), b3(n, N_POW), b3(n, N_POW), b3(1, n), b3(1, n),
                  b3(p, n), b3(p, n), b3(n, fl), b3(n, fl)],
        out_specs=[b3(fl, fl), b3(fl, 2 * n), b3(2 * n, fl), b3(n, 2 * n)],
        out_shape=[jax.ShapeDtypeStruct((g, fl, fl), BF16), jax.ShapeDtypeStruct((g, fl, 2 * n), BF16),
                   jax.ShapeDtypeStruct((g, 2 * n, fl), BF16), jax.ShapeDtypeStruct((g, n, 2 * n), F32)],
        name="s5_chunk_weights",
    )(prow_re, prow_im, pcol_re, pcol_im, z_re, z_im, bt_re, bt_im, ct_re, ct_im)


LANE_GROUPS = 128 // S5_GROUP
RELAYOUT_ROWS = 64


def _block_transpose8(xs):
    lane_blk = lax.broadcasted_iota(jnp.int32, (1, 128), 1) // S5_GROUP
    xs = list(xs)
    for d in (4, 2, 1):
        upper = (lane_blk & d) != 0
        nxt = list(xs)
        for a in range(8):
            if a & d:
                continue
            lo, hi = xs[a], xs[a + d]
            nxt[a] = jnp.where(upper, pltpu.roll(hi, S5_GROUP * d, axis=1), lo)
            nxt[a + d] = jnp.where(upper, hi, pltpu.roll(lo, 128 - S5_GROUP * d, axis=1))
        xs = nxt
    return xs


def _s5_prompt_kernel(nseq, u_ref, wt_ref, wx_ref, wc_ref, apw_ref, a16_ref, d_ref, z_ref, hend_ref,
                      tok_scr, flat_scr, x_scr, h_scr):
    n = S5_STATE
    gb = LANE_GROUPS
    rows = u_ref.shape[0] // S5_CHUNK
    slab = 2 * nseq
    nlo = rows // slab
    rb = RELAYOUT_ROWS
    halves = S5_FLAT // 128

    tok_scr[...] = u_ref[...].astype(F32)

    def to_flat(r, carry):
        r0 = pl.multiple_of(r * rb, rb)
        for sh in range(halves):
            xs = [tok_scr[pl.ds(r0 * S5_CHUNK + 8 * sh + s, rb, stride=S5_CHUNK), :] for s in range(8)]
            ys = _block_transpose8(xs)
            for gi in range(gb):
                flat_scr[gi, pl.ds(r0, rb), sh * 128:(sh + 1) * 128] = ys[gi]
        return carry

    lax.fori_loop(0, rows // rb, to_flat, 0)

    odd = (lax.broadcasted_iota(jnp.int32, (slab, 1), 0) % 2) == 1
    for gi in range(gb):
        u = flat_scr[gi]
        u_b = u.astype(BF16)
        x_scr[gi] = _bdot(u_b, wx_ref[gi])
        ar, ai = a16_ref[gi, :, :n], a16_ref[gi, :, n:]
        hr = jnp.zeros((slab, n), F32)
        hi = jnp.zeros((slab, n), F32)
        for j in range(nlo):
            rs = pl.ds(j, slab, stride=nlo)
            h_scr[gi, rs, :] = jnp.concatenate([hr, hi], axis=1)
            xj = x_scr[gi, rs, :]
            nr, ni = _cmul(hr, hi, ar, ai)
            hr = nr + xj[:, :n]
            hi = ni + xj[:, n:]
        mr = jnp.where(odd, pltpu.roll(hr, 1, axis=0), 0.0)
        mi = jnp.where(odd, pltpu.roll(hi, 1, axis=0), 0.0)
        pr, pi = apw_ref[gi, :, :n], apw_ref[gi, :, n:]
        for k in range(1, slab, 2):
            cr, ci = _cmul(pr, pi, mr[k:k + 1], mi[k:k + 1])
            h_scr[gi, k * nlo:(k + 1) * nlo, :n] += cr
            h_scr[gi, k * nlo:(k + 1) * nlo, n:] += ci
        lr, li = _cmul(pr[nlo - 1:nlo], pi[nlo - 1:nlo], ar, ai)
        er, ei = _cmul(mr, mi, lr, li)
        hend_ref[gi] = jnp.concatenate([hr + er, hi + ei], axis=1)
        y = _bdot(u_b, wt_ref[gi]) + _bdot(h_scr[gi].astype(BF16), wc_ref[gi]) + d_ref[gi] * u
        flat_scr[gi] = _gelu_exact(y)

    def to_tokens(r, carry):
        r0 = pl.multiple_of(r * rb, rb)
        for sh in range(halves):
            ys = [flat_scr[gi, pl.ds(r0, rb), sh * 128:(sh + 1) * 128] for gi in range(gb)]
            xs = _block_transpose8(ys)
            for s in range(8):
                tok_scr[pl.ds(r0 * S5_CHUNK + 8 * sh + s, rb, stride=S5_CHUNK), :] = xs[s]
        return carry

    lax.fori_loop(0, rows // rb, to_tokens, 0)
    z_ref[...] = tok_scr[...].astype(BF16)


def _s5_prompt(proj, wt, wx, wc, apw, a16, d_flat, nseq):
    m = proj.shape[0]
    g, n, fl, gb = S5_GROUPS, S5_STATE, S5_FLAT, LANE_GROUPS
    rows = m // S5_CHUNK
    b3 = lambda a, b: pl.BlockSpec((gb, a, b), lambda i: (i, 0, 0))
    est = (4 * m * 128 * 2 + m * 128 * 4 + gb * rows * (fl + 4 * n) * 4 + 6 * rows * fl * 4
           + 2 * gb * (fl * fl + 4 * n * fl) * 2)
    return pl.pallas_call(
        functools.partial(_s5_prompt_kernel, nseq),
        grid=(g // gb,),
        in_specs=[pl.BlockSpec((m, 128), lambda i: (0, OFF_U // 128 + i)),
                  b3(fl, fl), b3(fl, 2 * n), b3(2 * n, fl), b3(n, 2 * n), b3(1, 2 * n), b3(1, fl)],
        out_specs=[pl.BlockSpec((m, 128), lambda i: (0, i)), b3(2 * nseq, 2 * n)],
        out_shape=[jax.ShapeDtypeStruct((m, S5_WIDTH), BF16), jax.ShapeDtypeStruct((g, 2 * nseq, 2 * n), F32)],
        scratch_shapes=[pltpu.VMEM((m, 128), F32), pltpu.VMEM((gb, rows, fl), F32),
                        pltpu.VMEM((gb, rows, 2 * n), F32), pltpu.VMEM((gb, rows, 2 * n), F32)],
        compiler_params=_params(est),
        name="s5_prompt",
    )(proj, wt, wx, wc, apw, a16, d_flat)


def _s5_sample_kernel(gb, u_ref, bst_ref, ctre_ref, ctim_ref, a1_ref, hre_ref, him_ref, d_ref,
                      z_ref, ore_ref, oim_ref):
    n = S5_STATE
    u = u_ref[...].astype(F32)
    lane_g = lax.broadcasted_iota(jnp.int32, (1, gb * S5_GROUP), 1) // S5_GROUP
    bst = bst_ref[...]
    y = d_ref[...] * u
    for gi in range(gb):
        mine = lane_g == gi
        bu = _bdot(jnp.where(mine, u, 0.0).astype(BF16), bst)
        ar, ai = a1_ref[gi, :, :n], a1_ref[gi, :, n:]
        nr, ni = _cmul(hre_ref[gi], him_ref[gi], ar, ai)
        hr = nr + bu[:, :n]
        hi = ni + bu[:, n:]
        ore_ref[gi] = hr
        oim_ref[gi] = hi
        cmat = jnp.concatenate([jnp.where(mine, ctre_ref[gi], 0.0), jnp.where(mine, -ctim_ref[gi], 0.0)], axis=0)
        y = y + _bdot(jnp.concatenate([hr, hi], axis=1).astype(BF16), cmat.astype(BF16))
    z_ref[...] = _gelu_exact(y).astype(BF16)


def _s5_sample(proj, bstack, ct_re, ct_im, a1, h_re, h_im, d_row, gb):
    nb = proj.shape[0]
    g, n, p = S5_GROUPS, S5_STATE, S5_GROUP
    lanes = gb * p
    b3 = lambda a, b: pl.BlockSpec((gb, a, b), lambda i: (i, 0, 0))
    st = b3(nb, n)
    return pl.pallas_call(
        functools.partial(_s5_sample_kernel, gb),
        grid=(g // gb,),
        in_specs=[pl.BlockSpec((nb, lanes), lambda i: (0, OFF_U // lanes + i)),
                  pl.BlockSpec((lanes, 2 * n), lambda i: (i, 0)),
                  b3(n, lanes), b3(n, lanes), b3(1, 2 * n), st, st,
                  pl.BlockSpec((1, lanes), lambda i: (0, i))],
        out_specs=[pl.BlockSpec((nb, lanes), lambda i: (0, i)), st, st],
        out_shape=[jax.ShapeDtypeStruct((nb, S5_WIDTH), BF16),
                   jax.ShapeDtypeStruct((g, nb, n), F32), jax.ShapeDtypeStruct((g, nb, n), F32)],
        name="s5_sample",
    )(proj, bstack, ct_re, ct_im, a1, h_re, h_im, d_row)


def _dense_tail(x, proj, o_a, z, w, tm, w_down_bf16=None):
    o_b = _glu(z, w['w_s5_glu'], tm, 512)
    merged = _merge(o_a, o_b, w['w_proj_a'], w['w_proj_b'], proj, tm, 512)
    x1 = _outproj(merged, w['w_out'], x, tm, 512)
    if w_down_bf16 is None:
        act, w_down_bf16 = _ffn_up(x1, w['norm2_g'], w['w_ffn_up'], tm, 256, w_down=w['w_ffn_down'])
    else:
        act, = _ffn_up(x1, w['norm2_g'], w['w_ffn_up'], tm, 256)
    y = _ffn_down(act, w_down_bf16, x1, w['final_norm_g'], min(tm, 1024), 512)
    return y, w_down_bf16


def kernel(x_prompt, x_sample, state_hgrn, state_s5_re, state_s5_im, lb_logits, norm1_g, w_in, hgrn_norm_g,
           s5_lam_re, s5_lam_im, s5_log_dt, s5_B_re, s5_B_im, s5_C_re, s5_C_im, s5_D, w_s5_glu, w_proj_a,
           w_proj_b, w_out, norm2_g, w_ffn_up, w_ffn_down, final_norm_g):
    l = 0
    bp, seq, d = x_prompt.shape
    nb = x_sample.shape[0]
    g, n, p = S5_GROUPS, S5_STATE, S5_GROUP
    w = {'w_s5_glu': w_s5_glu[l], 'w_proj_a': w_proj_a[l], 'w_proj_b': w_proj_b[l], 'w_out': w_out[l],
         'norm2_g': norm2_g[l][None, :], 'w_ffn_up': w_ffn_up[l], 'w_ffn_down': w_ffn_down[l],
         'final_norm_g': final_norm_g[None, :]}
    g1 = norm1_g[l][None, :]
    gn = hgrn_norm_g[l][None, :]

    xp = x_prompt.reshape(bp * seq, d)
    xs = x_sample.reshape(nb, d)
    proj_p = _inproj(xp, g1, w_in[l], 2048, 512)
    proj_s = _inproj(xs, g1, w_in[l], nb, 512)

    lb = _lower_bound(lb_logits)
    oa_p, sh_p = _hgrn_prompt(proj_p, lb, gn, bp, seq, 512)
    q_t = proj_s[:, OFF_Q:OFF_Q + HGRN_WIDTH].astype(F32).T
    f_t = proj_s[:, OFF_F:OFF_F + HGRN_WIDTH].astype(F32).T
    oa_s, sh_s = _hgrn_sample(q_t, f_t, proj_s, lb.reshape(HGRN_WIDTH, 1), gn, state_hgrn[l])

    pw_re, pw_im, z_re, z_im = _s5_discretise(s5_lam_re[l], s5_lam_im[l], s5_log_dt[l][:, None])
    prow_re, prow_im = pw_re.transpose(1, 0, 2), pw_im.transpose(1, 0, 2)
    pcol_re, pcol_im = pw_re.transpose(1, 2, 0), pw_im.transpose(1, 2, 0)
    bt_re, bt_im = s5_B_re[l].transpose(0, 2, 1), s5_B_im[l].transpose(0, 2, 1)
    ct_re = jnp.tile(s5_C_re[l].transpose(0, 2, 1), (1, 1, S5_CHUNK))
    ct_im = jnp.tile(s5_C_im[l].transpose(0, 2, 1), (1, 1, S5_CHUNK))
    wt, wx, wc, apw = _s5_weights(prow_re, prow_im, pcol_re, pcol_im, z_re[:, None, :], z_im[:, None, :],
                                  bt_re, bt_im, ct_re, ct_im, 8)
    d_gp = s5_D[l].reshape(g, 1, p)
    d_flat = jnp.tile(d_gp, (1, 1, S5_CHUNK))
    a16 = jnp.concatenate([prow_re[:, 4:5, :], prow_im[:, 4:5, :]], axis=2)
    a1 = jnp.concatenate([prow_re[:, 0:1, :], prow_im[:, 0:1, :]], axis=2)

    z_p, hend = _s5_prompt(proj_p, wt, wx, wc, apw, a16, d_flat, bp)
    hend = hend[:, 1::2, :]
    s5re_p = hend[:, :, :n].transpose(1, 0, 2)
    s5im_p = hend[:, :, n:].transpose(1, 0, 2)

    bstack = wx[:, S5_FLAT - p:, :].reshape(g * p, 2 * n)
    gbs = 128 // p
    z_s, s5re_s, s5im_s = _s5_sample(proj_s, bstack, ct_re[:, :, :gbs * p], ct_im[:, :, :gbs * p], a1,
                                     state_s5_re[l].transpose(1, 0, 2), state_s5_im[l].transpose(1, 0, 2),
                                     s5_D[l][None, :], gbs)

    y_p, w_down_bf16 = _dense_tail(xp, proj_p, oa_p, z_p, w, 2048)
    y_s, _ = _dense_tail(xs, proj_s, oa_s, z_s, w, nb, w_down_bf16)

    return (y_p.reshape(bp, seq, d), y_s.reshape(nb, 1, d),
            sh_p[None], s5re_p[None], s5im_p[None],
            sh_s[None], s5re_s.transpose(1, 0, 2)[None], s5im_s.transpose(1, 0, 2)[None])
```

```python
import functools
import math

import jax
import jax.numpy as jnp
from jax import lax
from jax.experimental import pallas as pl
from jax.experimental.pallas import tpu as pltpu

F32 = jnp.float32
BF16 = jnp.bfloat16

D_MODEL = 2048
HGRN_WIDTH = 1024
HEAD_DIM = 128
HEADS = 8
S5_WIDTH = 1024
S5_GROUPS = 64
S5_GROUP = 16
S5_STATE = 64
FFN_HIDDEN = 5632
IN_PROJ_WIDTH = 4 * HGRN_WIDTH + S5_WIDTH + 2 * D_MODEL
RMS_EPS = 1e-6

OFF_Q, OFF_F, OFF_I, OFF_OG = 0, HGRN_WIDTH, 2 * HGRN_WIDTH, 3 * HGRN_WIDTH
OFF_U = 4 * HGRN_WIDTH
OFF_GA = OFF_U + S5_WIDTH
OFF_GB = OFF_GA + D_MODEL

HGRN_SUB = 16
HGRN_CHUNK = 128
S5_CHUNK = 16
S5_FLAT = S5_CHUNK * S5_GROUP
N_POW = 10

V7X_VMEM_BYTES = 64 * 1024 * 1024
VMEM_LIMIT_CAP = V7X_VMEM_BYTES - 6 * 1024 * 1024


def _params(est_bytes):
    limit = min(max(int(est_bytes * 1.25) + (4 << 20), 32 << 20), VMEM_LIMIT_CAP)
    return pltpu.CompilerParams(vmem_limit_bytes=limit)


def _bdot(a, b):
    return jnp.dot(a, b, preferred_element_type=F32)


def _rms_rows(x, g):
    ms = jnp.mean(x * x, axis=-1, keepdims=True)
    return x * lax.rsqrt(ms + RMS_EPS) * g


def _inproj_kernel(x_ref, g_ref, w_ref, o_ref, h_scr):
    @pl.when(pl.program_id(1) == 0)
    def _():
        h_scr[...] = _rms_rows(x_ref[...], g_ref[...]).astype(BF16)

    o_ref[...] = _bdot(h_scr[...], w_ref[...].astype(BF16)).astype(BF16)


def _inproj(x, g, w, tm, tn):
    m, k = x.shape
    n = w.shape[1]
    single = pl.Buffered(1) if tm >= 2048 else None
    est = ((1 if single else 2) * tm * k * 4 + 2 * k * tn * 4 + 2 * tm * tn * 2 + tm * k * 2 + k * tn * 2
           + tm * tn * 6)
    return pl.pallas_call(
        _inproj_kernel,
        grid=(m // tm, n // tn),
        in_specs=[pl.BlockSpec((tm, k), lambda i, j: (i, 0), pipeline_mode=single),
                  pl.BlockSpec((1, k), lambda i, j: (0, 0)),
                  pl.BlockSpec((k, tn), lambda i, j: (0, j))],
        out_specs=pl.BlockSpec((tm, tn), lambda i, j: (i, j)),
        out_shape=jax.ShapeDtypeStruct((m, n), BF16),
        scratch_shapes=[pltpu.VMEM((tm, k), BF16)],
        compiler_params=_params(est),
        name="inproj",
    )(x, g, w)


def _glu_kernel(z_ref, wa_ref, wb_ref, o_ref):
    z = z_ref[...]
    a = _bdot(z, wa_ref[...].astype(BF16))
    b = _bdot(z, wb_ref[...].astype(BF16))
    o_ref[...] = (a * jax.nn.sigmoid(b)).astype(BF16)


def _glu(z, w, tm, tn):
    m, k = z.shape
    n = w.shape[1] // 2
    nj = n // tn
    est = 2 * tm * k * 2 + 4 * k * tn * 4 + 2 * tm * tn * 2 + 2 * k * tn * 2 + 3 * tm * tn * 4
    return pl.pallas_call(
        _glu_kernel,
        grid=(m // tm, nj),
        in_specs=[pl.BlockSpec((tm, k), lambda i, j: (i, 0)),
                  pl.BlockSpec((k, tn), lambda i, j: (0, j)),
                  pl.BlockSpec((k, tn), lambda i, j: (0, j + nj))],
        out_specs=pl.BlockSpec((tm, tn), lambda i, j: (i, j)),
        out_shape=jax.ShapeDtypeStruct((m, n), BF16),
        compiler_params=_params(est),
        name="s5_glu",
    )(z, w, w)


def _merge_kernel(oa_ref, ob_ref, wa_ref, wb_ref, ga_ref, gb_ref, o_ref):
    a = _bdot(oa_ref[...], wa_ref[...].astype(BF16))
    b = _bdot(ob_ref[...], wb_ref[...].astype(BF16))
    ga, gb = ga_ref[...].astype(F32), gb_ref[...].astype(F32)
    o_ref[...] = (jax.nn.sigmoid(ga) * a + jax.nn.sigmoid(gb) * b).astype(BF16)


def _merge(o_a, o_b, w_a, w_b, proj, tm, tn):
    m, k = o_a.shape
    n = w_a.shape[1]
    ja, jb = OFF_GA // tn, OFF_GB // tn
    est = 4 * tm * k * 2 + 4 * k * tn * 4 + 4 * tm * tn * 4 + 2 * tm * tn * 2 + 2 * k * tn * 2 + 3 * tm * tn * 4
    return pl.pallas_call(
        _merge_kernel,
        grid=(m // tm, n // tn),
        in_specs=[pl.BlockSpec((tm, k), lambda i, j: (i, 0)),
                  pl.BlockSpec((tm, k), lambda i, j: (i, 0)),
                  pl.BlockSpec((k, tn), lambda i, j: (0, j)),
                  pl.BlockSpec((k, tn), lambda i, j: (0, j)),
                  pl.BlockSpec((tm, tn), lambda i, j: (i, j + ja)),
                  pl.BlockSpec((tm, tn), lambda i, j: (i, j + jb))],
        out_specs=pl.BlockSpec((tm, tn), lambda i, j: (i, j)),
        out_shape=jax.ShapeDtypeStruct((m, n), BF16),
        compiler_params=_params(est),
        name="gated_merge",
    )(o_a, o_b, w_a, w_b, proj, proj)


def _outproj_kernel(m_ref, w_ref, x_ref, o_ref):
    o_ref[...] = x_ref[...] + _bdot(m_ref[...], w_ref[...].astype(BF16))


def _outproj(merged, w, x, tm, tn):
    m, k = merged.shape
    n = w.shape[1]
    est = 2 * tm * k * 2 + 2 * k * tn * 4 + 4 * tm * tn * 4 + k * tn * 2 + tm * tn * 4
    return pl.pallas_call(
        _outproj_kernel,
        grid=(m // tm, n // tn),
        in_specs=[pl.BlockSpec((tm, k), lambda i, j: (i, 0)),
                  pl.BlockSpec((k, tn), lambda i, j: (0, j)),
                  pl.BlockSpec((tm, tn), lambda i, j: (i, j))],
        out_specs=pl.BlockSpec((tm, tn), lambda i, j: (i, j)),
        out_shape=jax.ShapeDtypeStruct((m, n), F32),
        compiler_params=_params(est),
        name="out_proj",
    )(merged, w, x)


def _ffn_up_kernel(cast_down, x_ref, g_ref, wa_ref, wb_ref, *rest):
    if cast_down:
        wd_ref, o_ref, wdb_ref, h_scr = rest
        wdb_ref[...] = wd_ref[...].astype(BF16)
    else:
        o_ref, h_scr = rest

    @pl.when(pl.program_id(1) == 0)
    def _():
        h_scr[...] = _rms_rows(x_ref[...], g_ref[...]).astype(BF16)

    h = h_scr[...]
    a = _bdot(h, wa_ref[...].astype(BF16))
    b = _bdot(h, wb_ref[...].astype(BF16))
    o_ref[...] = (jax.nn.silu(a) * b).astype(BF16)


def _ffn_up(x, g, w, tm, tn, w_down=None):
    m, k = x.shape
    n = w.shape[1] // 2
    nj = n // tn
    single = pl.Buffered(1) if tm >= 2048 else None
    est = ((1 if single else 2) * tm * k * 4 + 4 * k * tn * 4 + 2 * tm * tn * 2 + tm * k * 2 + 2 * k * tn * 2
           + 3 * tm * tn * 4)
    in_specs = [pl.BlockSpec((tm, k), lambda i, j: (i, 0), pipeline_mode=single),
                pl.BlockSpec((1, k), lambda i, j: (0, 0)),
                pl.BlockSpec((k, tn), lambda i, j: (0, j)),
                pl.BlockSpec((k, tn), lambda i, j: (0, j + nj))]
    out_specs = [pl.BlockSpec((tm, tn), lambda i, j: (i, j))]
    out_shape = [jax.ShapeDtypeStruct((m, n), BF16)]
    args = [x, g, w, w]
    if w_down is not None:
        rows = w_down.shape[0] // ((m // tm) * nj)
        slice_spec = pl.BlockSpec((rows, w_down.shape[1]), lambda i, j: (i * nj + j, 0))
        in_specs.append(slice_spec)
        out_specs.append(slice_spec)
        out_shape.append(jax.ShapeDtypeStruct(w_down.shape, BF16))
        args.append(w_down)
        est += 2 * rows * w_down.shape[1] * 6
    return pl.pallas_call(
        functools.partial(_ffn_up_kernel, w_down is not None),
        grid=(m // tm, nj),
        in_specs=in_specs,
        out_specs=out_specs,
        out_shape=out_shape,
        scratch_shapes=[pltpu.VMEM((tm, k), BF16)],
        compiler_params=_params(est),
        name="ffn_up",
    )(*args)


FFN_DOWN_COLS = 512


def _ffn_down_kernel(a_ref, w_ref, x_ref, g_ref, o_ref):
    kk = pl.program_id(1)

    @pl.when(kk == 0)
    def _():
        o_ref[...] = x_ref[...]

    a = a_ref[...]
    for n0 in range(0, o_ref.shape[1], FFN_DOWN_COLS):
        cols = slice(n0, n0 + FFN_DOWN_COLS)
        o_ref[:, cols] += _bdot(a, w_ref[:, cols])

    @pl.when(kk == pl.num_programs(1) - 1)
    def _():
        o_ref[...] = _rms_rows(o_ref[...], g_ref[...])


def _ffn_down(act, w, x, g, tm, tk):
    m, k = act.shape
    n = w.shape[1]
    est = 2 * tm * tk * 2 + 2 * tk * n * 2 + 4 * tm * n * 4 + 2 * tm * n * 4
    return pl.pallas_call(
        _ffn_down_kernel,
        grid=(m // tm, k // tk),
        in_specs=[pl.BlockSpec((tm, tk), lambda i, kk: (i, kk)),
                  pl.BlockSpec((tk, n), lambda i, kk: (kk, 0)),
                  pl.BlockSpec((tm, n), lambda i, kk: (i, 0)),
                  pl.BlockSpec((1, n), lambda i, kk: (0, 0))],
        out_specs=pl.BlockSpec((tm, n), lambda i, kk: (i, 0)),
        out_shape=jax.ShapeDtypeStruct((m, n), F32),
        compiler_params=_params(est),
        name="ffn_down_final_norm",
    )(act, w, x, g)


def _lower_bound_kernel(lb_ref, o_ref):
    x = lb_ref[...]
    mx = jnp.max(x, axis=0, keepdims=True)
    e = jnp.exp(x - mx)
    o_ref[...] = e[0:1, :] / jnp.sum(e, axis=0, keepdims=True)


def _lower_bound(lb_logits):
    return pl.pallas_call(
        _lower_bound_kernel,
        out_shape=jax.ShapeDtypeStruct((1, HGRN_WIDTH), F32),
        name="hgrn_lower_bound",
    )(lb_logits)


def _split3(x):
    hi = x.astype(BF16)
    r = x - hi.astype(F32)
    mid = r.astype(BF16)
    lo = (r - mid.astype(F32)).astype(BF16)
    return hi, mid, lo


def _rows_ref(b, block, pick):
    parts = []
    for s0 in range(0, HGRN_CHUNK, block):
        parts.append(jnp.broadcast_to(b[s0 + pick:s0 + pick + 1, :], (block, b.shape[1])))
    return parts[0] if len(parts) == 1 else jnp.concatenate(parts, axis=0)


def _hgrn_prompt_kernel(n_chunks, q_ref, f_ref, i_ref, og_ref, lb_ref, gn_ref, oa_ref, s_ref, st_scr):
    t = pl.program_id(1)

    @pl.when(t == 0)
    def _():
        st_scr[...] = jnp.zeros(st_scr.shape, F32)

    c_ = HGRN_CHUNK
    row = lax.broadcasted_iota(jnp.int32, (c_, c_), 0)
    col = lax.broadcasted_iota(jnp.int32, (c_, c_), 1)
    tri = (col <= row).astype(BF16)
    m_diag = (row // HGRN_SUB == col // HGRN_SUB) & (col <= row)
    levels = (128, 64, 32)
    m_lvl = [(row // bs == col // bs) & (row % bs >= bs // 2) & (col % bs < bs // 2) for bs in levels]
    rcol = lax.broadcasted_iota(jnp.int32, (c_, 1), 0)
    second = [(rcol % bs) >= bs // 2 for bs in levels]

    lb = lb_ref[...]
    gn = gn_ref[...]

    def chunk(c, carry):
        r0 = pl.multiple_of(c * c_, c_)
        rows = pl.ds(r0, c_)
        f = lb + (1.0 - lb) * jax.nn.sigmoid(f_ref[rows, :].astype(F32))
        g = jnp.log(f)
        k_all = 1.0 - f
        g_hi, g_mid, g_lo = _split3(g)
        b = _bdot(tri, g_hi) + _bdot(tri, g_mid) + _bdot(tri, g_lo)
        b_last = b[c_ - 1:c_, :]
        e_in = jnp.exp(b)
        e_upd = jnp.exp(b_last - b)
        dec_all = jnp.exp(b_last)
        b_mid = _rows_ref(b, HGRN_SUB, HGRN_SUB // 2)
        e_dq = jnp.exp(b - b_mid)
        e_dk = jnp.exp(b_mid - b)
        e_lvl = []
        for li, bs in enumerate(levels):
            d = b - _rows_ref(b, bs, bs // 2 - 1)
            e_lvl.append(jnp.exp(jnp.where(second[li], d, -d)))
        q_all = q_ref[rows, :].astype(F32)
        v_all = i_ref[rows, :].astype(F32)
        og = og_ref[rows, :].astype(F32)
        for h in range(HEADS):
            ls = slice(h * HEAD_DIM, (h + 1) * HEAD_DIM)
            q, k, v = q_all[:, ls], k_all[:, ls], v_all[:, ls]
            nt = (((1,), (1,)), ((), ()))
            sc = lax.dot_general((q * e_dq[:, ls]).astype(BF16), (k * e_dk[:, ls]).astype(BF16), nt,
                                 preferred_element_type=F32)
            scores = jnp.where(m_diag, sc, 0.0)
            for li in range(len(levels)):
                e = e_lvl[li][:, ls]
                sc = lax.dot_general((q * e).astype(BF16), (k * e).astype(BF16), nt, preferred_element_type=F32)
                scores = jnp.where(m_lvl[li], sc, scores)
            v_b = v.astype(BF16)
            st = st_scr[h]
            o = _bdot(scores.astype(BF16), v_b)
            o = o + lax.dot_general((q * e_in[:, ls]).astype(BF16), st.astype(BF16), nt, preferred_element_type=F32)
            upd = lax.dot_general(v_b, (k * e_upd[:, ls]).astype(BF16), (((0,), (0,)), ((), ())),
                                  preferred_element_type=F32)
            st_scr[h] = dec_all[:, ls] * st + upd
            o = o * lax.rsqrt(jnp.mean(o * o, axis=-1, keepdims=True) + RMS_EPS) * gn[:, ls]
            oa_ref[rows, ls] = (o * jax.nn.silu(og[:, ls])).astype(BF16)
        return carry

    lax.fori_loop(0, n_chunks, chunk, 0)

    @pl.when(t == pl.num_programs(1) - 1)
    def _():
        for h in range(HEADS):
            s_ref[0, h] = st_scr[h].T


def _hgrn_prompt(proj, lb, gn, batch, seq, tt):
    m = batch * seq
    w = HGRN_WIDTH
    nt = seq // tt
    blk = lambda off: pl.BlockSpec((tt, w), lambda b, t, off=off: (b * nt + t, off // w))
    est = 8 * tt * w * 4 + 2 * tt * w * 2 + 3 * HEADS * HEAD_DIM * HEAD_DIM * 4 + 24 * HGRN_CHUNK * w * 4
    return pl.pallas_call(
        functools.partial(_hgrn_prompt_kernel, tt // HGRN_CHUNK),
        grid=(batch, nt),
        in_specs=[blk(OFF_Q), blk(OFF_F), blk(OFF_I), blk(OFF_OG),
                  pl.BlockSpec((1, w), lambda b, t: (0, 0)),
                  pl.BlockSpec((1, w), lambda b, t: (0, 0))],
        out_specs=[pl.BlockSpec((tt, w), lambda b, t: (b * nt + t, 0)),
                   pl.BlockSpec((1, HEADS, HEAD_DIM, HEAD_DIM), lambda b, t: (b, 0, 0, 0))],
        out_shape=[jax.ShapeDtypeStruct((m, w), BF16),
                   jax.ShapeDtypeStruct((batch, HEADS, HEAD_DIM, HEAD_DIM), F32)],
        scratch_shapes=[pltpu.VMEM((HEADS, HEAD_DIM, HEAD_DIM), F32)],
        compiler_params=_params(est),
        name="hgrn2_prompt",
    )(proj, proj, proj, proj, lb, gn)


def _hgrn_sample_kernel(nb, qt_ref, ft_ref, v_ref, og_ref, lbc_ref, gn_ref, s0_ref, oa_ref, s_ref, o_scr):
    lbc = lbc_ref[...]
    f_t = lbc + (1.0 - lbc) * jax.nn.sigmoid(ft_ref[...])
    q_t = qt_ref[...]
    v = v_ref[...].astype(F32)
    for b in range(nb):
        v_b = v[b:b + 1, :]
        s_new = v_b + f_t[:, b:b + 1] * (s0_ref[b, 0] - v_b)
        s_ref[b, 0] = s_new
        o_scr[b:b + 1, :] = jnp.sum(q_t[:, b:b + 1] * s_new, axis=0, keepdims=True)
    o = o_scr[...]
    o = o * lax.rsqrt(jnp.mean(o * o, axis=-1, keepdims=True) + RMS_EPS) * gn_ref[...]
    oa_ref[...] = (o * jax.nn.silu(og_ref[...].astype(F32))).astype(BF16)


def _hgrn_sample(q_t, f_t, proj, lb_col, gn, s0):
    nb = s0.shape[0]
    hd = HEAD_DIM
    est = 4 * nb * hd * hd * 4 + 16 * nb * hd * 4
    return pl.pallas_call(
        functools.partial(_hgrn_sample_kernel, nb),
        grid=(HEADS,),
        in_specs=[pl.BlockSpec((hd, nb), lambda h: (h, 0)),
                  pl.BlockSpec((hd, nb), lambda h: (h, 0)),
                  pl.BlockSpec((nb, hd), lambda h: (0, OFF_I // hd + h)),
                  pl.BlockSpec((nb, hd), lambda h: (0, OFF_OG // hd + h)),
                  pl.BlockSpec((hd, 1), lambda h: (h, 0)),
                  pl.BlockSpec((1, hd), lambda h: (0, h)),
                  pl.BlockSpec((nb, 1, hd, hd), lambda h: (0, h, 0, 0))],
        out_specs=[pl.BlockSpec((nb, hd), lambda h: (0, h)),
                   pl.BlockSpec((nb, 1, hd, hd), lambda h: (0, h, 0, 0))],
        out_shape=[jax.ShapeDtypeStruct((nb, HGRN_WIDTH), BF16),
                   jax.ShapeDtypeStruct(s0.shape, F32)],
        scratch_shapes=[pltpu.VMEM((nb, hd), F32)],
        compiler_params=_params(est),
        name="hgrn2_sample",
    )(q_t, f_t, proj, proj, lb_col, gn, s0)


def _cmul(ar, ai, br, bi):
    return ar * br - ai * bi, ar * bi + ai * br


def _gelu_exact(x):
    return 0.5 * x * (1.0 + lax.erf(x * math.sqrt(0.5)))


def _s5_discretise_kernel(lre_ref, lim_ref, ldt_ref, pre_ref, pim_ref, zre_ref, zim_ref):
    lam_re, lam_im = lre_ref[...], lim_ref[...]
    dt = jnp.exp(ldt_ref[...])
    mag = jnp.exp(lam_re * dt)
    ab_re = mag * jnp.cos(lam_im * dt)
    ab_im = mag * jnp.sin(lam_im * dt)
    den = lam_re * lam_re + lam_im * lam_im
    nr, ni = ab_re - 1.0, ab_im
    zre_ref[...] = (nr * lam_re + ni * lam_im) / den
    zim_ref[...] = (ni * lam_re - nr * lam_im) / den
    pr, pi = ab_re, ab_im
    for kk in range(N_POW):
        pre_ref[:, kk, :] = pr
        pim_ref[:, kk, :] = pi
        pr, pi = _cmul(pr, pi, pr, pi)


def _s5_discretise(lam_re, lam_im, log_dt):
    g, n = lam_re.shape
    return pl.pallas_call(
        _s5_discretise_kernel,
        out_shape=[jax.ShapeDtypeStruct((g, N_POW, n), F32), jax.ShapeDtypeStruct((g, N_POW, n), F32),
                   jax.ShapeDtypeStruct((g, n), F32), jax.ShapeDtypeStruct((g, n), F32)],
        name="s5_discretise",
    )(lam_re, lam_im, log_dt)


def _pow_rows(xr, xi, prow_re, prow_im, gi, first, exps, nbits=4):
    for kk in range(nbits):
        ar, ai = prow_re[gi, first + kk:first + kk + 1, :], prow_im[gi, first + kk:first + kk + 1, :]
        yr, yi = _cmul(xr, xi, ar, ai)
        hit = ((exps >> kk) & 1) == 1
        xr, xi = jnp.where(hit, yr, xr), jnp.where(hit, yi, xi)
    return xr, xi


def _s5_weights_kernel(gb, prow_re, prow_im, zre_ref, zim_ref, btre_ref, btim_ref, cre_ref, cim_ref,
                       wt_ref, wx_ref, wc_ref, apw_ref):
    fl = S5_FLAT
    srow = lax.broadcasted_iota(jnp.int32, (fl, 1), 0) // S5_GROUP
    jrow = lax.broadcasted_iota(jnp.int32, (S5_STATE, 1), 0)
    for gi in range(gb):
        zr, zi = zre_ref[gi], zim_ref[gi]
        bbr, bbi = _cmul(zr, zi, btre_ref[gi], btim_ref[gi])
        xr, xi = _pow_rows(jnp.concatenate([bbr] * S5_CHUNK, axis=0), jnp.concatenate([bbi] * S5_CHUNK, axis=0),
                           prow_re, prow_im, gi, 0, (S5_CHUNK - 1) - srow)
        wx_ref[gi] = jnp.concatenate([xr, xi], axis=1).astype(BF16)
        cr, ci = _pow_rows(jnp.concatenate([cre_ref[gi]] * S5_CHUNK, axis=0),
                           jnp.concatenate([cim_ref[gi]] * S5_CHUNK, axis=0), prow_re, prow_im, gi, 0, srow)
        g0t = jnp.concatenate([cr, -ci], axis=1)
        c1r, c1i = _cmul(cr, ci, prow_re[gi, 0:1, :], prow_im[gi, 0:1, :])
        wc_ref[gi] = jnp.concatenate([c1r, -c1i], axis=1).astype(BF16)
        kflat = lax.dot_general(jnp.concatenate([bbr, bbi], axis=1), g0t, (((1,), (1,)), ((), ())),
                                preferred_element_type=F32, precision=lax.Precision.HIGHEST)
        pieces = [kflat]
        for s in range(1, S5_CHUNK):
            pieces.append(jnp.concatenate([jnp.zeros((S5_GROUP, s * S5_GROUP), F32),
                                           kflat[:, :fl - s * S5_GROUP]], axis=1))
        wt_ref[gi] = jnp.concatenate(pieces, axis=0).astype(BF16)
        pr = jnp.ones((S5_STATE, S5_STATE), F32)
        pi = jnp.zeros((S5_STATE, S5_STATE), F32)
        pr, pi = _pow_rows(pr, pi, prow_re, prow_im, gi, 4, jrow, nbits=6)
        apw_ref[gi] = jnp.concatenate([pr, pi], axis=1)


def _s5_weights(prow_re, prow_im, z_re, z_im, bt_re, bt_im, c_re, c_im, gb):
    g = S5_GROUPS
    n, p, fl = S5_STATE, S5_GROUP, S5_FLAT
    b3 = lambda a, b: pl.BlockSpec((gb, a, b), lambda i: (i, 0, 0))
    return pl.pallas_call(
        functools.partial(_s5_weights_kernel, gb),
        grid=(g // gb,),
        in_specs=[b3(N_POW, n), b3(N_POW, n), b3(1, n), b3(1, n), b3(p, n), b3(p, n), b3(p, n), b3(p, n)],
        out_specs=[b3(fl, fl), b3(fl, 2 * n), b3(fl, 2 * n), b3(n, 2 * n)],
        out_shape=[jax.ShapeDtypeStruct((g, fl, fl), BF16), jax.ShapeDtypeStruct((g, fl, 2 * n), BF16),
                   jax.ShapeDtypeStruct((g, fl, 2 * n), BF16), jax.ShapeDtypeStruct((g, n, 2 * n), F32)],
        name="s5_chunk_weights",
    )(prow_re, prow_im, z_re, z_im, bt_re, bt_im, c_re, c_im)


LANE_GROUPS = 128 // S5_GROUP
RELAYOUT_ROWS = 64


def _block_transpose8(xs):
    lane_blk = lax.broadcasted_iota(jnp.int32, (1, 128), 1) // S5_GROUP
    xs = list(xs)
    for d in (4, 2, 1):
        upper = (lane_blk & d) != 0
        nxt = list(xs)
        for a in range(8):
            if a & d:
                continue
            lo, hi = xs[a], xs[a + d]
            nxt[a] = jnp.where(upper, pltpu.roll(hi, S5_GROUP * d, axis=1), lo)
            nxt[a + d] = jnp.where(upper, hi, pltpu.roll(lo, 128 - S5_GROUP * d, axis=1))
        xs = nxt
    return xs


def _s5_prompt_kernel(nseq, u_ref, wt_ref, wx_ref, wc_ref, apw_ref, a16_ref, d_ref, z_ref, hend_ref,
                      tok_scr, flat_scr, x_scr, h_scr):
    n = S5_STATE
    gb = LANE_GROUPS
    rows = u_ref.shape[0] // S5_CHUNK
    slab = 2 * nseq
    nlo = rows // slab
    rb = RELAYOUT_ROWS
    halves = S5_FLAT // 128

    tok_scr[...] = u_ref[...].astype(F32)

    def to_flat(r, carry):
        r0 = pl.multiple_of(r * rb, rb)
        for sh in range(halves):
            xs = [tok_scr[pl.ds(r0 * S5_CHUNK + 8 * sh + s, rb, stride=S5_CHUNK), :] for s in range(8)]
            ys = _block_transpose8(xs)
            for gi in range(gb):
                flat_scr[gi, pl.ds(r0, rb), sh * 128:(sh + 1) * 128] = ys[gi]
        return carry

    lax.fori_loop(0, rows // rb, to_flat, 0)

    odd = (lax.broadcasted_iota(jnp.int32, (slab, 1), 0) % 2) == 1
    for gi in range(gb):
        u = flat_scr[gi]
        u_b = u.astype(BF16)
        x_scr[gi] = _bdot(u_b, wx_ref[gi])
        ar, ai = a16_ref[gi, :, :n], a16_ref[gi, :, n:]
        hr = jnp.zeros((slab, n), F32)
        hi = jnp.zeros((slab, n), F32)
        for j in range(nlo):
            rs = pl.ds(j, slab, stride=nlo)
            h_scr[gi, rs, :] = jnp.concatenate([hr, hi], axis=1)
            xj = x_scr[gi, rs, :]
            nr, ni = _cmul(hr, hi, ar, ai)
            hr = nr + xj[:, :n]
            hi = ni + xj[:, n:]
        mr = jnp.where(odd, pltpu.roll(hr, 1, axis=0), 0.0)
        mi = jnp.where(odd, pltpu.roll(hi, 1, axis=0), 0.0)
        pr, pi = apw_ref[gi, :, :n], apw_ref[gi, :, n:]
        for k in range(1, slab, 2):
            cr, ci = _cmul(pr, pi, mr[k:k + 1], mi[k:k + 1])
            h_scr[gi, k * nlo:(k + 1) * nlo, :n] += cr
            h_scr[gi, k * nlo:(k + 1) * nlo, n:] += ci
        lr, li = _cmul(pr[nlo - 1:nlo], pi[nlo - 1:nlo], ar, ai)
        er, ei = _cmul(mr, mi, lr, li)
        hend_ref[gi] = jnp.concatenate([hr + er, hi + ei], axis=1)
        y = (_bdot(u_b, wt_ref[gi]) + d_ref[gi] * u
             + lax.dot_general(h_scr[gi].astype(BF16), wc_ref[gi], (((1,), (1,)), ((), ())),
                               preferred_element_type=F32))
        flat_scr[gi] = _gelu_exact(y)

    def to_tokens(r, carry):
        r0 = pl.multiple_of(r * rb, rb)
        for sh in range(halves):
            ys = [flat_scr[gi, pl.ds(r0, rb), sh * 128:(sh + 1) * 128] for gi in range(gb)]
            xs = _block_transpose8(ys)
            for s in range(8):
                tok_scr[pl.ds(r0 * S5_CHUNK + 8 * sh + s, rb, stride=S5_CHUNK), :] = xs[s]
        return carry

    lax.fori_loop(0, rows // rb, to_tokens, 0)
    z_ref[...] = tok_scr[...].astype(BF16)


def _s5_prompt(proj, wt, wx, wc, apw, a16, d_flat, nseq):
    m = proj.shape[0]
    g, n, fl, gb = S5_GROUPS, S5_STATE, S5_FLAT, LANE_GROUPS
    rows = m // S5_CHUNK
    b3 = lambda a, b: pl.BlockSpec((gb, a, b), lambda i: (i, 0, 0))
    est = (4 * m * 128 * 2 + m * 128 * 4 + gb * rows * (fl + 4 * n) * 4 + 6 * rows * fl * 4
           + 2 * gb * (fl * fl + 4 * n * fl) * 2)
    return pl.pallas_call(
        functools.partial(_s5_prompt_kernel, nseq),
        grid=(g // gb,),
        in_specs=[pl.BlockSpec((m, 128), lambda i: (0, OFF_U // 128 + i)),
                  b3(fl, fl), b3(fl, 2 * n), b3(fl, 2 * n), b3(n, 2 * n), b3(1, 2 * n), b3(1, fl)],
        out_specs=[pl.BlockSpec((m, 128), lambda i: (0, i)), b3(2 * nseq, 2 * n)],
        out_shape=[jax.ShapeDtypeStruct((m, S5_WIDTH), BF16), jax.ShapeDtypeStruct((g, 2 * nseq, 2 * n), F32)],
        scratch_shapes=[pltpu.VMEM((m, 128), F32), pltpu.VMEM((gb, rows, fl), F32),
                        pltpu.VMEM((gb, rows, 2 * n), F32), pltpu.VMEM((gb, rows, 2 * n), F32)],
        compiler_params=_params(est),
        name="s5_prompt",
    )(proj, wt, wx, wc, apw, a16, d_flat)


def _s5_sample_kernel(gb, u_ref, bst_ref, cre_ref, cim_ref, a1_ref, hre_ref, him_ref, d_ref,
                      z_ref, ore_ref, oim_ref):
    n = S5_STATE
    u = u_ref[...].astype(F32)
    lane_g = lax.broadcasted_iota(jnp.int32, (1, gb * S5_GROUP), 1) // S5_GROUP
    bst = bst_ref[...]
    cst = jnp.concatenate([cre_ref[...], -cim_ref[...]], axis=1)
    row_g = lax.broadcasted_iota(jnp.int32, (gb * S5_GROUP, 1), 0) // S5_GROUP
    y = d_ref[...] * u
    for gi in range(gb):
        mine = lane_g == gi
        bu = _bdot(jnp.where(mine, u, 0.0).astype(BF16), bst)
        ar, ai = a1_ref[gi, :, :n], a1_ref[gi, :, n:]
        nr, ni = _cmul(hre_ref[gi], him_ref[gi], ar, ai)
        hr = nr + bu[:, :n]
        hi = ni + bu[:, n:]
        ore_ref[gi] = hr
        oim_ref[gi] = hi
        cmat = jnp.where(row_g == gi, cst, 0.0).astype(BF16)
        y = y + lax.dot_general(jnp.concatenate([hr, hi], axis=1).astype(BF16), cmat, (((1,), (1,)), ((), ())),
                                preferred_element_type=F32)
    z_ref[...] = _gelu_exact(y).astype(BF16)


def _s5_sample(proj, bstack, c_re, c_im, a1, h_re, h_im, d_row, gb):
    nb = proj.shape[0]
    g, n, p = S5_GROUPS, S5_STATE, S5_GROUP
    lanes = gb * p
    b3 = lambda a, b: pl.BlockSpec((gb, a, b), lambda i: (i, 0, 0))
    st = b3(nb, n)
    return pl.pallas_call(
        functools.partial(_s5_sample_kernel, gb),
        grid=(g // gb,),
        in_specs=[pl.BlockSpec((nb, lanes), lambda i: (0, OFF_U // lanes + i)),
                  pl.BlockSpec((lanes, 2 * n), lambda i: (i, 0)),
                  pl.BlockSpec((lanes, n), lambda i: (i, 0)), pl.BlockSpec((lanes, n), lambda i: (i, 0)),
                  b3(1, 2 * n), st, st,
                  pl.BlockSpec((1, lanes), lambda i: (0, i))],
        out_specs=[pl.BlockSpec((nb, lanes), lambda i: (0, i)), st, st],
        out_shape=[jax.ShapeDtypeStruct((nb, S5_WIDTH), BF16),
                   jax.ShapeDtypeStruct((g, nb, n), F32), jax.ShapeDtypeStruct((g, nb, n), F32)],
        name="s5_sample",
    )(proj, bstack, c_re, c_im, a1, h_re, h_im, d_row)


def _dense_tail(x, proj, o_a, z, w, tm, w_down_bf16=None):
    o_b = _glu(z, w['w_s5_glu'], tm, 512)
    merged = _merge(o_a, o_b, w['w_proj_a'], w['w_proj_b'], proj, tm, 512)
    x1 = _outproj(merged, w['w_out'], x, tm, 512)
    if w_down_bf16 is None:
        act, w_down_bf16 = _ffn_up(x1, w['norm2_g'], w['w_ffn_up'], tm, 256, w_down=w['w_ffn_down'])
    else:
        act, = _ffn_up(x1, w['norm2_g'], w['w_ffn_up'], tm, 256)
    y = _ffn_down(act, w_down_bf16, x1, w['final_norm_g'], min(tm, 1024), 512)
    return y, w_down_bf16


def kernel(x_prompt, x_sample, state_hgrn, state_s5_re, state_s5_im, lb_logits, norm1_g, w_in, hgrn_norm_g,
           s5_lam_re, s5_lam_im, s5_log_dt, s5_B_re, s5_B_im, s5_C_re, s5_C_im, s5_D, w_s5_glu, w_proj_a,
           w_proj_b, w_out, norm2_g, w_ffn_up, w_ffn_down, final_norm_g):
    l = 0
    bp, seq, d = x_prompt.shape
    nb = x_sample.shape[0]
    g, n, p = S5_GROUPS, S5_STATE, S5_GROUP
    w = {'w_s5_glu': w_s5_glu[l], 'w_proj_a': w_proj_a[l], 'w_proj_b': w_proj_b[l], 'w_out': w_out[l],
         'norm2_g': norm2_g[l][None, :], 'w_ffn_up': w_ffn_up[l], 'w_ffn_down': w_ffn_down[l],
         'final_norm_g': final_norm_g[None, :]}
    g1 = norm1_g[l][None, :]
    gn = hgrn_norm_g[l][None, :]

    xp = x_prompt.reshape(bp * seq, d)
    xs = x_sample.reshape(nb, d)
    proj_p = _inproj(xp, g1, w_in[l], 2048, 512)
    proj_s = _inproj(xs, g1, w_in[l], nb, 512)

    lb = _lower_bound(lb_logits)
    oa_p, sh_p = _hgrn_prompt(proj_p, lb, gn, bp, seq, 512)
    q_t = proj_s[:, OFF_Q:OFF_Q + HGRN_WIDTH].astype(F32).T
    f_t = proj_s[:, OFF_F:OFF_F + HGRN_WIDTH].astype(F32).T
    oa_s, sh_s = _hgrn_sample(q_t, f_t, proj_s, lb.reshape(HGRN_WIDTH, 1), gn, state_hgrn[l])

    prow_re, prow_im, z_re, z_im = _s5_discretise(s5_lam_re[l], s5_lam_im[l], s5_log_dt[l][:, None])
    bt_re, bt_im = s5_B_re[l].transpose(0, 2, 1), s5_B_im[l].transpose(0, 2, 1)
    wt, wx, wc, apw = _s5_weights(prow_re, prow_im, z_re[:, None, :], z_im[:, None, :], bt_re, bt_im,
                                  s5_C_re[l], s5_C_im[l], 8)
    d_gp = s5_D[l].reshape(g, 1, p)
    d_flat = jnp.tile(d_gp, (1, 1, S5_CHUNK))
    a16 = jnp.concatenate([prow_re[:, 4:5, :], prow_im[:, 4:5, :]], axis=2)
    a1 = jnp.concatenate([prow_re[:, 0:1, :], prow_im[:, 0:1, :]], axis=2)

    z_p, hend = _s5_prompt(proj_p, wt, wx, wc, apw, a16, d_flat, bp)
    hend = hend[:, 1::2, :]
    s5re_p = hend[:, :, :n].transpose(1, 0, 2)
    s5im_p = hend[:, :, n:].transpose(1, 0, 2)

    bstack = wx[:, S5_FLAT - p:, :].reshape(g * p, 2 * n)
    gbs = 128 // p
    c_rows_re, c_rows_im = s5_C_re[l].reshape(g * p, n), s5_C_im[l].reshape(g * p, n)
    z_s, s5re_s, s5im_s = _s5_sample(proj_s, bstack, c_rows_re, c_rows_im, a1,
                                     state_s5_re[l].transpose(1, 0, 2), state_s5_im[l].transpose(1, 0, 2),
                                     s5_D[l][None, :], gbs)

    y_p, w_down_bf16 = _dense_tail(xp, proj_p, oa_p, z_p, w, 2048)
    y_s, _ = _dense_tail(xs, proj_s, oa_s, z_s, w, nb, w_down_bf16)

    return (y_p.reshape(bp, seq, d), y_s.reshape(nb, 1, d),
            sh_p[None], s5re_p[None], s5im_p[None],
            sh_s[None], s5re_s.transpose(1, 0, 2)[None], s5im_s.transpose(1, 0, 2)[None])
```

```python
import functools
import math

import jax
import jax.numpy as jnp
from jax import lax
from jax.experimental import pallas as pl
from jax.experimental.pallas import tpu as pltpu

F32 = jnp.float32
BF16 = jnp.bfloat16

D_MODEL = 2048
HGRN_WIDTH = 1024
HEAD_DIM = 128
HEADS = 8
S5_WIDTH = 1024
S5_GROUPS = 64
S5_GROUP = 16
S5_STATE = 64
FFN_HIDDEN = 5632
IN_PROJ_WIDTH = 4 * HGRN_WIDTH + S5_WIDTH + 2 * D_MODEL
RMS_EPS = 1e-6

OFF_Q, OFF_F, OFF_I, OFF_OG = 0, HGRN_WIDTH, 2 * HGRN_WIDTH, 3 * HGRN_WIDTH
OFF_U = 4 * HGRN_WIDTH
OFF_GA = OFF_U + S5_WIDTH
OFF_GB = OFF_GA + D_MODEL

HGRN_SUB = 16
HGRN_CHUNK = 128
S5_CHUNK = 16
S5_FLAT = S5_CHUNK * S5_GROUP
N_POW = 10

V7X_VMEM_BYTES = 64 * 1024 * 1024
VMEM_LIMIT_CAP = V7X_VMEM_BYTES - 6 * 1024 * 1024


def _params(est_bytes):
    limit = min(max(int(est_bytes * 1.25) + (4 << 20), 32 << 20), VMEM_LIMIT_CAP)
    return pltpu.CompilerParams(vmem_limit_bytes=limit)


def _bdot(a, b):
    return jnp.dot(a, b, preferred_element_type=F32)


def _rms_rows(x, g):
    ms = jnp.mean(x * x, axis=-1, keepdims=True)
    return x * lax.rsqrt(ms + RMS_EPS) * g


def _inproj_kernel(emit_w, x_ref, g_ref, w_ref, o_ref, *rest):
    if emit_w:
        wb_ref, h_scr = rest
    else:
        h_scr, = rest

    @pl.when(pl.program_id(1) == 0)
    def _():
        h_scr[...] = _rms_rows(x_ref[...], g_ref[...]).astype(BF16)

    w_b = w_ref[...].astype(BF16)
    if emit_w:
        @pl.when(pl.program_id(0) == 0)
        def _():
            wb_ref[...] = w_b
    o_ref[...] = _bdot(h_scr[...], w_b).astype(BF16)


def _inproj(x, g, w, tm, tn, emit_w=False):
    m, k = x.shape
    n = w.shape[1]
    single = pl.Buffered(1) if tm >= 2048 else None
    est = ((1 if single else 2) * tm * k * 4 + 2 * k * tn * 4 + 2 * tm * tn * 2 + tm * k * 2 + k * tn * 2
           + tm * tn * 6)
    out_specs = [pl.BlockSpec((tm, tn), lambda i, j: (i, j))]
    out_shape = [jax.ShapeDtypeStruct((m, n), BF16)]
    if emit_w:
        last = n // tn - 1
        out_specs.append(pl.BlockSpec((k, tn), lambda i, j: (0, jnp.where(i == 0, j, last))))
        out_shape.append(jax.ShapeDtypeStruct((k, n), BF16))
        est += 2 * k * tn * 2
    return pl.pallas_call(
        functools.partial(_inproj_kernel, emit_w),
        grid=(m // tm, n // tn),
        in_specs=[pl.BlockSpec((tm, k), lambda i, j: (i, 0), pipeline_mode=single),
                  pl.BlockSpec((1, k), lambda i, j: (0, 0)),
                  pl.BlockSpec((k, tn), lambda i, j: (0, j))],
        out_specs=out_specs,
        out_shape=out_shape,
        scratch_shapes=[pltpu.VMEM((tm, k), BF16)],
        compiler_params=_params(est),
        name="inproj",
    )(x, g, w)


def _glu_kernel(z_ref, wa_ref, wb_ref, o_ref):
    z = z_ref[...]
    a = _bdot(z, wa_ref[...].astype(BF16))
    b = _bdot(z, wb_ref[...].astype(BF16))
    o_ref[...] = (a * jax.nn.sigmoid(b)).astype(BF16)


def _glu(z, w, tm, tn):
    m, k = z.shape
    n = w.shape[1] // 2
    nj = n // tn
    est = 2 * tm * k * 2 + 4 * k * tn * 4 + 2 * tm * tn * 2 + 2 * k * tn * 2 + 3 * tm * tn * 4
    return pl.pallas_call(
        _glu_kernel,
        grid=(m // tm, nj),
        in_specs=[pl.BlockSpec((tm, k), lambda i, j: (i, 0)),
                  pl.BlockSpec((k, tn), lambda i, j: (0, j)),
                  pl.BlockSpec((k, tn), lambda i, j: (0, j + nj))],
        out_specs=pl.BlockSpec((tm, tn), lambda i, j: (i, j)),
        out_shape=jax.ShapeDtypeStruct((m, n), BF16),
        compiler_params=_params(est),
        name="s5_glu",
    )(z, w, w)


def _merge_kernel(oa_ref, ob_ref, wa_ref, wb_ref, ga_ref, gb_ref, o_ref):
    a = _bdot(oa_ref[...], wa_ref[...].astype(BF16))
    b = _bdot(ob_ref[...], wb_ref[...].astype(BF16))
    ga, gb = ga_ref[...].astype(F32), gb_ref[...].astype(F32)
    o_ref[...] = (jax.nn.sigmoid(ga) * a + jax.nn.sigmoid(gb) * b).astype(BF16)


def _merge(o_a, o_b, w_a, w_b, proj, tm, tn):
    m, k = o_a.shape
    n = w_a.shape[1]
    ja, jb = OFF_GA // tn, OFF_GB // tn
    est = 4 * tm * k * 2 + 4 * k * tn * 4 + 4 * tm * tn * 4 + 2 * tm * tn * 2 + 2 * k * tn * 2 + 3 * tm * tn * 4
    return pl.pallas_call(
        _merge_kernel,
        grid=(m // tm, n // tn),
        in_specs=[pl.BlockSpec((tm, k), lambda i, j: (i, 0)),
                  pl.BlockSpec((tm, k), lambda i, j: (i, 0)),
                  pl.BlockSpec((k, tn), lambda i, j: (0, j)),
                  pl.BlockSpec((k, tn), lambda i, j: (0, j)),
                  pl.BlockSpec((tm, tn), lambda i, j: (i, j + ja)),
                  pl.BlockSpec((tm, tn), lambda i, j: (i, j + jb))],
        out_specs=pl.BlockSpec((tm, tn), lambda i, j: (i, j)),
        out_shape=jax.ShapeDtypeStruct((m, n), BF16),
        compiler_params=_params(est),
        name="gated_merge",
    )(o_a, o_b, w_a, w_b, proj, proj)


def _outproj_kernel(m_ref, w_ref, x_ref, o_ref):
    o_ref[...] = x_ref[...] + _bdot(m_ref[...], w_ref[...].astype(BF16))


def _outproj(merged, w, x, tm, tn):
    m, k = merged.shape
    n = w.shape[1]
    est = 2 * tm * k * 2 + 2 * k * tn * 4 + 4 * tm * tn * 4 + k * tn * 2 + tm * tn * 4
    return pl.pallas_call(
        _outproj_kernel,
        grid=(m // tm, n // tn),
        in_specs=[pl.BlockSpec((tm, k), lambda i, j: (i, 0)),
                  pl.BlockSpec((k, tn), lambda i, j: (0, j)),
                  pl.BlockSpec((tm, tn), lambda i, j: (i, j))],
        out_specs=pl.BlockSpec((tm, tn), lambda i, j: (i, j)),
        out_shape=jax.ShapeDtypeStruct((m, n), F32),
        compiler_params=_params(est),
        name="out_proj",
    )(merged, w, x)


def _ffn_up_kernel(emit_w, x_ref, g_ref, wa_ref, wb_ref, *rest):
    if emit_w:
        wd_ref, o_ref, wab_ref, wbb_ref, wdb_ref, h_scr = rest
        wdb_ref[...] = wd_ref[...].astype(BF16)
    else:
        o_ref, h_scr = rest

    @pl.when(pl.program_id(1) == 0)
    def _():
        h_scr[...] = _rms_rows(x_ref[...], g_ref[...]).astype(BF16)

    h = h_scr[...]
    wa_b, wb_b = wa_ref[...].astype(BF16), wb_ref[...].astype(BF16)
    if emit_w:
        @pl.when(pl.program_id(0) == 0)
        def _():
            wab_ref[...] = wa_b
            wbb_ref[...] = wb_b
    o_ref[...] = (jax.nn.silu(_bdot(h, wa_b)) * _bdot(h, wb_b)).astype(BF16)


def _ffn_up(x, g, w_gate, w_lin, lin_off, tm, tn, w_down=None):
    m, k = x.shape
    n = FFN_HIDDEN
    nj = n // tn
    jl = lin_off // tn
    single = pl.Buffered(1) if tm >= 2048 else None
    est = ((1 if single else 2) * tm * k * 4 + 4 * k * tn * 4 + 2 * tm * tn * 2 + tm * k * 2 + 2 * k * tn * 2
           + 3 * tm * tn * 4)
    in_specs = [pl.BlockSpec((tm, k), lambda i, j: (i, 0), pipeline_mode=single),
                pl.BlockSpec((1, k), lambda i, j: (0, 0)),
                pl.BlockSpec((k, tn), lambda i, j: (0, j)),
                pl.BlockSpec((k, tn), lambda i, j: (0, j + jl))]
    out_specs = [pl.BlockSpec((tm, tn), lambda i, j: (i, j))]
    out_shape = [jax.ShapeDtypeStruct((m, n), BF16)]
    args = [x, g, w_gate, w_lin]
    if w_down is not None:
        rows = w_down.shape[0] // ((m // tm) * nj)
        slice_spec = pl.BlockSpec((rows, w_down.shape[1]), lambda i, j: (i * nj + j, 0))
        tile_spec = pl.BlockSpec((k, tn), lambda i, j: (0, jnp.where(i == 0, j, nj - 1)))
        in_specs.append(slice_spec)
        out_specs += [tile_spec, tile_spec, slice_spec]
        out_shape += [jax.ShapeDtypeStruct((k, n), BF16), jax.ShapeDtypeStruct((k, n), BF16),
                      jax.ShapeDtypeStruct(w_down.shape, BF16)]
        args.append(w_down)
        est += 2 * rows * w_down.shape[1] * 6 + 4 * k * tn * 2
    return pl.pallas_call(
        functools.partial(_ffn_up_kernel, w_down is not None),
        grid=(m // tm, nj),
        in_specs=in_specs,
        out_specs=out_specs,
        out_shape=out_shape,
        scratch_shapes=[pltpu.VMEM((tm, k), BF16)],
        compiler_params=_params(est),
        name="ffn_up",
    )(*args)


FFN_DOWN_COLS = 512


def _ffn_down_kernel(a_ref, w_ref, x_ref, g_ref, o_ref):
    kk = pl.program_id(1)

    @pl.when(kk == 0)
    def _():
        o_ref[...] = x_ref[...]

    a = a_ref[...]
    for n0 in range(0, o_ref.shape[1], FFN_DOWN_COLS):
        cols = slice(n0, n0 + FFN_DOWN_COLS)
        o_ref[:, cols] += _bdot(a, w_ref[:, cols])

    @pl.when(kk == pl.num_programs(1) - 1)
    def _():
        o_ref[...] = _rms_rows(o_ref[...], g_ref[...])


def _ffn_down(act, w, x, g, tm, tk):
    m, k = act.shape
    n = w.shape[1]
    est = 2 * tm * tk * 2 + 2 * tk * n * 2 + 4 * tm * n * 4 + 2 * tm * n * 4
    return pl.pallas_call(
        _ffn_down_kernel,
        grid=(m // tm, k // tk),
        in_specs=[pl.BlockSpec((tm, tk), lambda i, kk: (i, kk)),
                  pl.BlockSpec((tk, n), lambda i, kk: (kk, 0)),
                  pl.BlockSpec((tm, n), lambda i, kk: (i, 0)),
                  pl.BlockSpec((1, n), lambda i, kk: (0, 0))],
        out_specs=pl.BlockSpec((tm, n), lambda i, kk: (i, 0)),
        out_shape=jax.ShapeDtypeStruct((m, n), F32),
        compiler_params=_params(est),
        name="ffn_down_final_norm",
    )(act, w, x, g)


def _lower_bound_kernel(lb_ref, o_ref):
    x = lb_ref[...]
    mx = jnp.max(x, axis=0, keepdims=True)
    e = jnp.exp(x - mx)
    o_ref[...] = e[0:1, :] / jnp.sum(e, axis=0, keepdims=True)


def _lower_bound(lb_logits):
    return pl.pallas_call(
        _lower_bound_kernel,
        out_shape=jax.ShapeDtypeStruct((1, HGRN_WIDTH), F32),
        name="hgrn_lower_bound",
    )(lb_logits)


def _split3(x):
    hi = x.astype(BF16)
    r = x - hi.astype(F32)
    mid = r.astype(BF16)
    lo = (r - mid.astype(F32)).astype(BF16)
    return hi, mid, lo


def _rows_ref(b, block, pick):
    parts = []
    for s0 in range(0, HGRN_CHUNK, block):
        parts.append(jnp.broadcast_to(b[s0 + pick:s0 + pick + 1, :], (block, b.shape[1])))
    return parts[0] if len(parts) == 1 else jnp.concatenate(parts, axis=0)


def _hgrn_prompt_kernel(n_chunks, q_ref, f_ref, i_ref, og_ref, lb_ref, gn_ref, oa_ref, s_ref, st_scr):
    t = pl.program_id(1)

    @pl.when(t == 0)
    def _():
        st_scr[...] = jnp.zeros(st_scr.shape, F32)

    c_ = HGRN_CHUNK
    row = lax.broadcasted_iota(jnp.int32, (c_, c_), 0)
    col = lax.broadcasted_iota(jnp.int32, (c_, c_), 1)
    tri = (col <= row).astype(BF16)
    m_diag = (row // HGRN_SUB == col // HGRN_SUB) & (col <= row)
    levels = (128, 64, 32)
    m_lvl = [(row // bs == col // bs) & (row % bs >= bs // 2) & (col % bs < bs // 2) for bs in levels]
    rcol = lax.broadcasted_iota(jnp.int32, (c_, 1), 0)
    second = [(rcol % bs) >= bs // 2 for bs in levels]

    lb = lb_ref[...]
    gn = gn_ref[...]

    def chunk(c, carry):
        r0 = pl.multiple_of(c * c_, c_)
        rows = pl.ds(r0, c_)
        f = lb + (1.0 - lb) * jax.nn.sigmoid(f_ref[rows, :].astype(F32))
        g = jnp.log(f)
        k_all = 1.0 - f
        g_hi, g_mid, g_lo = _split3(g)
        b = _bdot(tri, g_hi) + _bdot(tri, g_mid) + _bdot(tri, g_lo)
        b_last = b[c_ - 1:c_, :]
        e_in = jnp.exp(b)
        e_upd = jnp.exp(b_last - b)
        dec_all = jnp.exp(b_last)
        b_mid = _rows_ref(b, HGRN_SUB, HGRN_SUB // 2)
        e_dq = jnp.exp(b - b_mid)
        e_dk = jnp.exp(b_mid - b)
        e_lvl = []
        for li, bs in enumerate(levels):
            d = b - _rows_ref(b, bs, bs // 2 - 1)
            e_lvl.append(jnp.exp(jnp.where(second[li], d, -d)))
        q_all = q_ref[rows, :].astype(F32)
        v_all = i_ref[rows, :].astype(F32)
        og = og_ref[rows, :].astype(F32)
        for h in range(HEADS):
            ls = slice(h * HEAD_DIM, (h + 1) * HEAD_DIM)
            q, k, v = q_all[:, ls], k_all[:, ls], v_all[:, ls]
            nt = (((1,), (1,)), ((), ()))
            sc = lax.dot_general((q * e_dq[:, ls]).astype(BF16), (k * e_dk[:, ls]).astype(BF16), nt,
                                 preferred_element_type=F32)
            scores = jnp.where(m_diag, sc, 0.0)
            for li in range(len(levels)):
                e = e_lvl[li][:, ls]
                sc = lax.dot_general((q * e).astype(BF16), (k * e).astype(BF16), nt, preferred_element_type=F32)
                scores = jnp.where(m_lvl[li], sc, scores)
            v_b = v.astype(BF16)
            st = st_scr[h]
            o = _bdot(scores.astype(BF16), v_b)
            o = o + lax.dot_general((q * e_in[:, ls]).astype(BF16), st.astype(BF16), nt, preferred_element_type=F32)
            upd = lax.dot_general(v_b, (k * e_upd[:, ls]).astype(BF16), (((0,), (0,)), ((), ())),
                                  preferred_element_type=F32)
            st_scr[h] = dec_all[:, ls] * st + upd
            o = o * lax.rsqrt(jnp.mean(o * o, axis=-1, keepdims=True) + RMS_EPS) * gn[:, ls]
            oa_ref[rows, ls] = (o * jax.nn.silu(og[:, ls])).astype(BF16)
        return carry

    lax.fori_loop(0, n_chunks, chunk, 0)

    @pl.when(t == pl.num_programs(1) - 1)
    def _():
        for h in range(HEADS):
            s_ref[0, h] = st_scr[h].T


def _hgrn_prompt(proj, lb, gn, batch, seq, tt):
    m = batch * seq
    w = HGRN_WIDTH
    nt = seq // tt
    blk = lambda off: pl.BlockSpec((tt, w), lambda b, t, off=off: (b * nt + t, off // w))
    est = 8 * tt * w * 4 + 2 * tt * w * 2 + 3 * HEADS * HEAD_DIM * HEAD_DIM * 4 + 24 * HGRN_CHUNK * w * 4
    return pl.pallas_call(
        functools.partial(_hgrn_prompt_kernel, tt // HGRN_CHUNK),
        grid=(batch, nt),
        in_specs=[blk(OFF_Q), blk(OFF_F), blk(OFF_I), blk(OFF_OG),
                  pl.BlockSpec((1, w), lambda b, t: (0, 0)),
                  pl.BlockSpec((1, w), lambda b, t: (0, 0))],
        out_specs=[pl.BlockSpec((tt, w), lambda b, t: (b * nt + t, 0)),
                   pl.BlockSpec((1, HEADS, HEAD_DIM, HEAD_DIM), lambda b, t: (b, 0, 0, 0))],
        out_shape=[jax.ShapeDtypeStruct((m, w), BF16),
                   jax.ShapeDtypeStruct((batch, HEADS, HEAD_DIM, HEAD_DIM), F32)],
        scratch_shapes=[pltpu.VMEM((HEADS, HEAD_DIM, HEAD_DIM), F32)],
        compiler_params=_params(est),
        name="hgrn2_prompt",
    )(proj, proj, proj, proj, lb, gn)


def _hgrn_sample_kernel(nb, qt_ref, ft_ref, v_ref, og_ref, lbc_ref, gn_ref, s0_ref, oa_ref, s_ref, o_scr):
    lbc = lbc_ref[...]
    f_t = lbc + (1.0 - lbc) * jax.nn.sigmoid(ft_ref[...])
    q_t = qt_ref[...]
    v = v_ref[...].astype(F32)
    for b in range(nb):
        v_b = v[b:b + 1, :]
        s_new = v_b + f_t[:, b:b + 1] * (s0_ref[b, 0] - v_b)
        s_ref[b, 0] = s_new
        o_scr[b:b + 1, :] = jnp.sum(q_t[:, b:b + 1] * s_new, axis=0, keepdims=True)
    o = o_scr[...]
    o = o * lax.rsqrt(jnp.mean(o * o, axis=-1, keepdims=True) + RMS_EPS) * gn_ref[...]
    oa_ref[...] = (o * jax.nn.silu(og_ref[...].astype(F32))).astype(BF16)


def _hgrn_sample(q_t, f_t, proj, lb_col, gn, s0):
    nb = s0.shape[0]
    hd = HEAD_DIM
    est = 4 * nb * hd * hd * 4 + 16 * nb * hd * 4
    return pl.pallas_call(
        functools.partial(_hgrn_sample_kernel, nb),
        grid=(HEADS,),
        in_specs=[pl.BlockSpec((hd, nb), lambda h: (h, 0)),
                  pl.BlockSpec((hd, nb), lambda h: (h, 0)),
                  pl.BlockSpec((nb, hd), lambda h: (0, OFF_I // hd + h)),
                  pl.BlockSpec((nb, hd), lambda h: (0, OFF_OG // hd + h)),
                  pl.BlockSpec((hd, 1), lambda h: (h, 0)),
                  pl.BlockSpec((1, hd), lambda h: (0, h)),
                  pl.BlockSpec((nb, 1, hd, hd), lambda h: (0, h, 0, 0))],
        out_specs=[pl.BlockSpec((nb, hd), lambda h: (0, h)),
                   pl.BlockSpec((nb, 1, hd, hd), lambda h: (0, h, 0, 0))],
        out_shape=[jax.ShapeDtypeStruct((nb, HGRN_WIDTH), BF16),
                   jax.ShapeDtypeStruct(s0.shape, F32)],
        scratch_shapes=[pltpu.VMEM((nb, hd), F32)],
        compiler_params=_params(est),
        name="hgrn2_sample",
    )(q_t, f_t, proj, proj, lb_col, gn, s0)


def _cmul(ar, ai, br, bi):
    return ar * br - ai * bi, ar * bi + ai * br


def _gelu_exact(x):
    return 0.5 * x * (1.0 + lax.erf(x * math.sqrt(0.5)))


def _s5_discretise_kernel(lre_ref, lim_ref, ldt_ref, pre_ref, pim_ref, zre_ref, zim_ref):
    lam_re, lam_im = lre_ref[...], lim_ref[...]
    dt = jnp.exp(ldt_ref[...])
    mag = jnp.exp(lam_re * dt)
    ab_re = mag * jnp.cos(lam_im * dt)
    ab_im = mag * jnp.sin(lam_im * dt)
    den = lam_re * lam_re + lam_im * lam_im
    nr, ni = ab_re - 1.0, ab_im
    zre_ref[...] = (nr * lam_re + ni * lam_im) / den
    zim_ref[...] = (ni * lam_re - nr * lam_im) / den
    pr, pi = ab_re, ab_im
    for kk in range(N_POW):
        pre_ref[:, kk, :] = pr
        pim_ref[:, kk, :] = pi
        pr, pi = _cmul(pr, pi, pr, pi)


def _s5_discretise(lam_re, lam_im, log_dt):
    g, n = lam_re.shape
    return pl.pallas_call(
        _s5_discretise_kernel,
        out_shape=[jax.ShapeDtypeStruct((g, N_POW, n), F32), jax.ShapeDtypeStruct((g, N_POW, n), F32),
                   jax.ShapeDtypeStruct((g, n), F32), jax.ShapeDtypeStruct((g, n), F32)],
        name="s5_discretise",
    )(lam_re, lam_im, log_dt)


def _pow_rows(xr, xi, prow_re, prow_im, gi, first, exps, nbits=4):
    for kk in range(nbits):
        ar, ai = prow_re[gi, first + kk:first + kk + 1, :], prow_im[gi, first + kk:first + kk + 1, :]
        yr, yi = _cmul(xr, xi, ar, ai)
        hit = ((exps >> kk) & 1) == 1
        xr, xi = jnp.where(hit, yr, xr), jnp.where(hit, yi, xi)
    return xr, xi


def _s5_weights_kernel(gb, prow_re, prow_im, zre_ref, zim_ref, btre_ref, btim_ref, cre_ref, cim_ref,
                       wt_ref, wx_ref, wc_ref, apw_ref):
    fl = S5_FLAT
    srow = lax.broadcasted_iota(jnp.int32, (fl, 1), 0) // S5_GROUP
    jrow = lax.broadcasted_iota(jnp.int32, (S5_STATE, 1), 0)
    for gi in range(gb):
        zr, zi = zre_ref[gi], zim_ref[gi]
        bbr, bbi = _cmul(zr, zi, btre_ref[gi], btim_ref[gi])
        xr, xi = _pow_rows(jnp.concatenate([bbr] * S5_CHUNK, axis=0), jnp.concatenate([bbi] * S5_CHUNK, axis=0),
                           prow_re, prow_im, gi, 0, (S5_CHUNK - 1) - srow)
        wx_ref[gi] = jnp.concatenate([xr, xi], axis=1).astype(BF16)
        cr, ci = _pow_rows(jnp.concatenate([cre_ref[gi]] * S5_CHUNK, axis=0),
                           jnp.concatenate([cim_ref[gi]] * S5_CHUNK, axis=0), prow_re, prow_im, gi, 0, srow)
        g0t = jnp.concatenate([cr, -ci], axis=1)
        c1r, c1i = _cmul(cr, ci, prow_re[gi, 0:1, :], prow_im[gi, 0:1, :])
        wc_ref[gi] = jnp.concatenate([c1r, -c1i], axis=1).astype(BF16)
        kflat = lax.dot_general(jnp.concatenate([bbr, bbi], axis=1), g0t, (((1,), (1,)), ((), ())),
                                preferred_element_type=F32, precision=lax.Precision.HIGHEST)
        pieces = [kflat]
        for s in range(1, S5_CHUNK):
            pieces.append(jnp.concatenate([jnp.zeros((S5_GROUP, s * S5_GROUP), F32),
                                           kflat[:, :fl - s * S5_GROUP]], axis=1))
        wt_ref[gi] = jnp.concatenate(pieces, axis=0).astype(BF16)
        pr = jnp.ones((S5_STATE, S5_STATE), F32)
        pi = jnp.zeros((S5_STATE, S5_STATE), F32)
        pr, pi = _pow_rows(pr, pi, prow_re, prow_im, gi, 4, jrow, nbits=6)
        apw_ref[gi] = jnp.concatenate([pr, pi], axis=1)


def _s5_weights(prow_re, prow_im, z_re, z_im, bt_re, bt_im, c_re, c_im, gb):
    g = S5_GROUPS
    n, p, fl = S5_STATE, S5_GROUP, S5_FLAT
    b3 = lambda a, b: pl.BlockSpec((gb, a, b), lambda i: (i, 0, 0))
    return pl.pallas_call(
        functools.partial(_s5_weights_kernel, gb),
        grid=(g // gb,),
        in_specs=[b3(N_POW, n), b3(N_POW, n), b3(1, n), b3(1, n), b3(p, n), b3(p, n), b3(p, n), b3(p, n)],
        out_specs=[b3(fl, fl), b3(fl, 2 * n), b3(fl, 2 * n), b3(n, 2 * n)],
        out_shape=[jax.ShapeDtypeStruct((g, fl, fl), BF16), jax.ShapeDtypeStruct((g, fl, 2 * n), BF16),
                   jax.ShapeDtypeStruct((g, fl, 2 * n), BF16), jax.ShapeDtypeStruct((g, n, 2 * n), F32)],
        name="s5_chunk_weights",
    )(prow_re, prow_im, z_re, z_im, bt_re, bt_im, c_re, c_im)


LANE_GROUPS = 128 // S5_GROUP
RELAYOUT_ROWS = 64


def _block_transpose8(xs):
    lane_blk = lax.broadcasted_iota(jnp.int32, (1, 128), 1) // S5_GROUP
    xs = list(xs)
    for d in (4, 2, 1):
        upper = (lane_blk & d) != 0
        nxt = list(xs)
        for a in range(8):
            if a & d:
                continue
            lo, hi = xs[a], xs[a + d]
            nxt[a] = jnp.where(upper, pltpu.roll(hi, S5_GROUP * d, axis=1), lo)
            nxt[a + d] = jnp.where(upper, hi, pltpu.roll(lo, 128 - S5_GROUP * d, axis=1))
        xs = nxt
    return xs


def _s5_prompt_kernel(nseq, u_ref, wt_ref, wx_ref, wc_ref, apw_ref, a16_ref, d_ref, z_ref, hend_ref,
                      tok_scr, flat_scr, x_scr, h_scr):
    n = S5_STATE
    gb = LANE_GROUPS
    rows = u_ref.shape[0] // S5_CHUNK
    slab = 2 * nseq
    nlo = rows // slab
    rb = RELAYOUT_ROWS
    halves = S5_FLAT // 128

    tok_scr[...] = u_ref[...].astype(F32)

    def to_flat(r, carry):
        r0 = pl.multiple_of(r * rb, rb)
        for sh in range(halves):
            xs = [tok_scr[pl.ds(r0 * S5_CHUNK + 8 * sh + s, rb, stride=S5_CHUNK), :] for s in range(8)]
            ys = _block_transpose8(xs)
            for gi in range(gb):
                flat_scr[gi, pl.ds(r0, rb), sh * 128:(sh + 1) * 128] = ys[gi]
        return carry

    lax.fori_loop(0, rows // rb, to_flat, 0)

    odd = (lax.broadcasted_iota(jnp.int32, (slab, 1), 0) % 2) == 1
    for gi in range(gb):
        u = flat_scr[gi]
        u_b = u.astype(BF16)
        x_scr[gi] = _bdot(u_b, wx_ref[gi])
        ar, ai = a16_ref[gi, :, :n], a16_ref[gi, :, n:]
        hr = jnp.zeros((slab, n), F32)
        hi = jnp.zeros((slab, n), F32)
        for j in range(nlo):
            rs = pl.ds(j, slab, stride=nlo)
            h_scr[gi, rs, :] = jnp.concatenate([hr, hi], axis=1)
            xj = x_scr[gi, rs, :]
            nr, ni = _cmul(hr, hi, ar, ai)
            hr = nr + xj[:, :n]
            hi = ni + xj[:, n:]
        mr = jnp.where(odd, pltpu.roll(hr, 1, axis=0), 0.0)
        mi = jnp.where(odd, pltpu.roll(hi, 1, axis=0), 0.0)
        pr, pi = apw_ref[gi, :, :n], apw_ref[gi, :, n:]
        for k in range(1, slab, 2):
            cr, ci = _cmul(pr, pi, mr[k:k + 1], mi[k:k + 1])
            h_scr[gi, k * nlo:(k + 1) * nlo, :n] += cr
            h_scr[gi, k * nlo:(k + 1) * nlo, n:] += ci
        lr, li = _cmul(pr[nlo - 1:nlo], pi[nlo - 1:nlo], ar, ai)
        er, ei = _cmul(mr, mi, lr, li)
        hend_ref[gi] = jnp.concatenate([hr + er, hi + ei], axis=1)
        y = (_bdot(u_b, wt_ref[gi]) + d_ref[gi] * u
             + lax.dot_general(h_scr[gi].astype(BF16), wc_ref[gi], (((1,), (1,)), ((), ())),
                               preferred_element_type=F32))
        flat_scr[gi] = _gelu_exact(y)

    def to_tokens(r, carry):
        r0 = pl.multiple_of(r * rb, rb)
        for sh in range(halves):
            ys = [flat_scr[gi, pl.ds(r0, rb), sh * 128:(sh + 1) * 128] for gi in range(gb)]
            xs = _block_transpose8(ys)
            for s in range(8):
                tok_scr[pl.ds(r0 * S5_CHUNK + 8 * sh + s, rb, stride=S5_CHUNK), :] = xs[s]
        return carry

    lax.fori_loop(0, rows // rb, to_tokens, 0)
    z_ref[...] = tok_scr[...].astype(BF16)


def _s5_prompt(proj, wt, wx, wc, apw, a16, d_flat, nseq):
    m = proj.shape[0]
    g, n, fl, gb = S5_GROUPS, S5_STATE, S5_FLAT, LANE_GROUPS
    rows = m // S5_CHUNK
    b3 = lambda a, b: pl.BlockSpec((gb, a, b), lambda i: (i, 0, 0))
    est = (4 * m * 128 * 2 + m * 128 * 4 + gb * rows * (fl + 4 * n) * 4 + 6 * rows * fl * 4
           + 2 * gb * (fl * fl + 4 * n * fl) * 2)
    return pl.pallas_call(
        functools.partial(_s5_prompt_kernel, nseq),
        grid=(g // gb,),
        in_specs=[pl.BlockSpec((m, 128), lambda i: (0, OFF_U // 128 + i)),
                  b3(fl, fl), b3(fl, 2 * n), b3(fl, 2 * n), b3(n, 2 * n), b3(1, 2 * n), b3(1, fl)],
        out_specs=[pl.BlockSpec((m, 128), lambda i: (0, i)), b3(2 * nseq, 2 * n)],
        out_shape=[jax.ShapeDtypeStruct((m, S5_WIDTH), BF16), jax.ShapeDtypeStruct((g, 2 * nseq, 2 * n), F32)],
        scratch_shapes=[pltpu.VMEM((m, 128), F32), pltpu.VMEM((gb, rows, fl), F32),
                        pltpu.VMEM((gb, rows, 2 * n), F32), pltpu.VMEM((gb, rows, 2 * n), F32)],
        compiler_params=_params(est),
        name="s5_prompt",
    )(proj, wt, wx, wc, apw, a16, d_flat)


def _s5_sample_kernel(gb, u_ref, bst_ref, cre_ref, cim_ref, a1_ref, hre_ref, him_ref, d_ref,
                      z_ref, ore_ref, oim_ref):
    n = S5_STATE
    u = u_ref[...].astype(F32)
    lane_g = lax.broadcasted_iota(jnp.int32, (1, gb * S5_GROUP), 1) // S5_GROUP
    bst = bst_ref[...]
    cst = jnp.concatenate([cre_ref[...], -cim_ref[...]], axis=1)
    row_g = lax.broadcasted_iota(jnp.int32, (gb * S5_GROUP, 1), 0) // S5_GROUP
    y = d_ref[...] * u
    for gi in range(gb):
        mine = lane_g == gi
        bu = _bdot(jnp.where(mine, u, 0.0).astype(BF16), bst)
        ar, ai = a1_ref[gi, :, :n], a1_ref[gi, :, n:]
        nr, ni = _cmul(hre_ref[gi], him_ref[gi], ar, ai)
        hr = nr + bu[:, :n]
        hi = ni + bu[:, n:]
        ore_ref[gi] = hr
        oim_ref[gi] = hi
        cmat = jnp.where(row_g == gi, cst, 0.0).astype(BF16)
        y = y + lax.dot_general(jnp.concatenate([hr, hi], axis=1).astype(BF16), cmat, (((1,), (1,)), ((), ())),
                                preferred_element_type=F32)
    z_ref[...] = _gelu_exact(y).astype(BF16)


def _s5_sample(proj, bstack, c_re, c_im, a1, h_re, h_im, d_row, gb):
    nb = proj.shape[0]
    g, n, p = S5_GROUPS, S5_STATE, S5_GROUP
    lanes = gb * p
    b3 = lambda a, b: pl.BlockSpec((gb, a, b), lambda i: (i, 0, 0))
    st = b3(nb, n)
    return pl.pallas_call(
        functools.partial(_s5_sample_kernel, gb),
        grid=(g // gb,),
        in_specs=[pl.BlockSpec((nb, lanes), lambda i: (0, OFF_U // lanes + i)),
                  pl.BlockSpec((lanes, 2 * n), lambda i: (i, 0)),
                  pl.BlockSpec((lanes, n), lambda i: (i, 0)), pl.BlockSpec((lanes, n), lambda i: (i, 0)),
                  b3(1, 2 * n), st, st,
                  pl.BlockSpec((1, lanes), lambda i: (0, i))],
        out_specs=[pl.BlockSpec((nb, lanes), lambda i: (0, i)), st, st],
        out_shape=[jax.ShapeDtypeStruct((nb, S5_WIDTH), BF16),
                   jax.ShapeDtypeStruct((g, nb, n), F32), jax.ShapeDtypeStruct((g, nb, n), F32)],
        name="s5_sample",
    )(proj, bstack, c_re, c_im, a1, h_re, h_im, d_row)


def _dense_tail(x, proj, o_a, z, w, tm, ffn_bf16=None):
    o_b = _glu(z, w['w_s5_glu'], tm, 512)
    merged = _merge(o_a, o_b, w['w_proj_a'], w['w_proj_b'], proj, tm, 512)
    x1 = _outproj(merged, w['w_out'], x, tm, 512)
    if ffn_bf16 is None:
        act, *ffn_bf16 = _ffn_up(x1, w['norm2_g'], w['w_ffn_up'], w['w_ffn_up'], FFN_HIDDEN, tm, 256,
                                 w_down=w['w_ffn_down'])
    else:
        act, = _ffn_up(x1, w['norm2_g'], ffn_bf16[0], ffn_bf16[1], 0, tm, 256)
    y = _ffn_down(act, ffn_bf16[2], x1, w['final_norm_g'], min(tm, 1024), 512)
    return y, ffn_bf16


def kernel(x_prompt, x_sample, state_hgrn, state_s5_re, state_s5_im, lb_logits, norm1_g, w_in, hgrn_norm_g,
           s5_lam_re, s5_lam_im, s5_log_dt, s5_B_re, s5_B_im, s5_C_re, s5_C_im, s5_D, w_s5_glu, w_proj_a,
           w_proj_b, w_out, norm2_g, w_ffn_up, w_ffn_down, final_norm_g):
    l = 0
    bp, seq, d = x_prompt.shape
    nb = x_sample.shape[0]
    g, n, p = S5_GROUPS, S5_STATE, S5_GROUP
    w = {'w_s5_glu': w_s5_glu[l], 'w_proj_a': w_proj_a[l], 'w_proj_b': w_proj_b[l], 'w_out': w_out[l],
         'norm2_g': norm2_g[l][None, :], 'w_ffn_up': w_ffn_up[l], 'w_ffn_down': w_ffn_down[l],
         'final_norm_g': final_norm_g[None, :]}
    g1 = norm1_g[l][None, :]
    gn = hgrn_norm_g[l][None, :]

    xp = x_prompt.reshape(bp * seq, d)
    xs = x_sample.reshape(nb, d)
    proj_p, w_in_bf16 = _inproj(xp, g1, w_in[l], 2048, 512, emit_w=True)
    proj_s, = _inproj(xs, g1, w_in_bf16, nb, 512)

    lb = _lower_bound(lb_logits)
    oa_p, sh_p = _hgrn_prompt(proj_p, lb, gn, bp, seq, 512)
    q_t = proj_s[:, OFF_Q:OFF_Q + HGRN_WIDTH].astype(F32).T
    f_t = proj_s[:, OFF_F:OFF_F + HGRN_WIDTH].astype(F32).T
    oa_s, sh_s = _hgrn_sample(q_t, f_t, proj_s, lb.reshape(HGRN_WIDTH, 1), gn, state_hgrn[l])

    prow_re, prow_im, z_re, z_im = _s5_discretise(s5_lam_re[l], s5_lam_im[l], s5_log_dt[l][:, None])
    bt_re, bt_im = s5_B_re[l].transpose(0, 2, 1), s5_B_im[l].transpose(0, 2, 1)
    wt, wx, wc, apw = _s5_weights(prow_re, prow_im, z_re[:, None, :], z_im[:, None, :], bt_re, bt_im,
                                  s5_C_re[l], s5_C_im[l], 8)
    d_gp = s5_D[l].reshape(g, 1, p)
    d_flat = jnp.tile(d_gp, (1, 1, S5_CHUNK))
    a16 = jnp.concatenate([prow_re[:, 4:5, :], prow_im[:, 4:5, :]], axis=2)
    a1 = jnp.concatenate([prow_re[:, 0:1, :], prow_im[:, 0:1, :]], axis=2)

    z_p, hend = _s5_prompt(proj_p, wt, wx, wc, apw, a16, d_flat, bp)
    hend = hend[:, 1::2, :]
    s5re_p = hend[:, :, :n].transpose(1, 0, 2)
    s5im_p = hend[:, :, n:].transpose(1, 0, 2)

    bstack = wx[:, S5_FLAT - p:, :].reshape(g * p, 2 * n)
    gbs = 128 // p
    c_rows_re, c_rows_im = s5_C_re[l].reshape(g * p, n), s5_C_im[l].reshape(g * p, n)
    z_s, s5re_s, s5im_s = _s5_sample(proj_s, bstack, c_rows_re, c_rows_im, a1,
                                     state_s5_re[l].transpose(1, 0, 2), state_s5_im[l].transpose(1, 0, 2),
                                     s5_D[l][None, :], gbs)

    y_p, ffn_bf16 = _dense_tail(xp, proj_p, oa_p, z_p, w, 2048)
    y_s, _ = _dense_tail(xs, proj_s, oa_s, z_s, w, nb, ffn_bf16)

    return (y_p.reshape(bp, seq, d), y_s.reshape(nb, 1, d),
            sh_p[None], s5re_p[None], s5im_p[None],
            sh_s[None], s5re_s.transpose(1, 0, 2)[None], s5im_s.transpose(1, 0, 2)[None])
```

```python
import functools
import math

import jax
import jax.numpy as jnp
from jax import lax
from jax.experimental import pallas as pl
from jax.experimental.pallas import tpu as pltpu

F32 = jnp.float32
BF16 = jnp.bfloat16

D_MODEL = 2048
HGRN_WIDTH = 1024
HEAD_DIM = 128
HEADS = 8
S5_WIDTH = 1024
S5_GROUPS = 64
S5_GROUP = 16
S5_STATE = 64
FFN_HIDDEN = 5632
IN_PROJ_WIDTH = 4 * HGRN_WIDTH + S5_WIDTH + 2 * D_MODEL
RMS_EPS = 1e-6

OFF_Q, OFF_F, OFF_I, OFF_OG = 0, HGRN_WIDTH, 2 * HGRN_WIDTH, 3 * HGRN_WIDTH
OFF_U = 4 * HGRN_WIDTH
OFF_GA = OFF_U + S5_WIDTH
OFF_GB = OFF_GA + D_MODEL

HGRN_SUB = 16
HGRN_CHUNK = 128
S5_CHUNK = 16
S5_FLAT = S5_CHUNK * S5_GROUP
N_POW = 10

V7X_VMEM_BYTES = 64 * 1024 * 1024
VMEM_LIMIT_CAP = V7X_VMEM_BYTES - 6 * 1024 * 1024


def _params(est_bytes):
    limit = min(max(int(est_bytes * 1.25) + (4 << 20), 32 << 20), VMEM_LIMIT_CAP)
    return pltpu.CompilerParams(vmem_limit_bytes=limit)


def _bdot(a, b):
    return jnp.dot(a, b, preferred_element_type=F32)


def _rms_rows(x, g):
    ms = jnp.mean(x * x, axis=-1, keepdims=True)
    return x * lax.rsqrt(ms + RMS_EPS) * g


def _inproj_kernel(emit_w, x_ref, g_ref, w_ref, *rest):
    if emit_w:
        ws_ref, o_ref, wb_ref, h_scr = rest
        wb_ref[...] = ws_ref[...].astype(BF16)
    else:
        o_ref, h_scr = rest

    @pl.when(pl.program_id(1) == 0)
    def _():
        h_scr[...] = _rms_rows(x_ref[...], g_ref[...]).astype(BF16)

    o_ref[...] = _bdot(h_scr[...], w_ref[...].astype(BF16)).astype(BF16)


def _inproj(x, g, w, tm, tn, emit_w=False):
    m, k = x.shape
    n = w.shape[1]
    single = pl.Buffered(1) if tm >= 2048 else None
    est = ((1 if single else 2) * tm * k * 4 + 2 * k * tn * 4 + 2 * tm * tn * 2 + tm * k * 2 + k * tn * 2
           + tm * tn * 6)
    nj = n // tn
    in_specs = [pl.BlockSpec((tm, k), lambda i, j: (i, 0), pipeline_mode=single),
                pl.BlockSpec((1, k), lambda i, j: (0, 0)),
                pl.BlockSpec((k, tn), lambda i, j: (0, j))]
    out_specs = [pl.BlockSpec((tm, tn), lambda i, j: (i, j))]
    out_shape = [jax.ShapeDtypeStruct((m, n), BF16)]
    args = [x, g, w]
    if emit_w:
        cols = n // ((m // tm) * nj)
        slice_spec = pl.BlockSpec((k, cols), lambda i, j: (0, i * nj + j))
        in_specs.append(slice_spec)
        out_specs.append(slice_spec)
        out_shape.append(jax.ShapeDtypeStruct((k, n), BF16))
        args.append(w)
        est += 2 * k * cols * 6
    return pl.pallas_call(
        functools.partial(_inproj_kernel, emit_w),
        grid=(m // tm, nj),
        in_specs=in_specs,
        out_specs=out_specs,
        out_shape=out_shape,
        scratch_shapes=[pltpu.VMEM((tm, k), BF16)],
        compiler_params=_params(est),
        name="inproj",
    )(*args)


def _glu_kernel(z_ref, wa_ref, wb_ref, o_ref):
    z = z_ref[...]
    a = _bdot(z, wa_ref[...].astype(BF16))
    b = _bdot(z, wb_ref[...].astype(BF16))
    o_ref[...] = (a * jax.nn.sigmoid(b)).astype(BF16)


def _glu(z, w, tm, tn):
    m, k = z.shape
    n = w.shape[1] // 2
    nj = n // tn
    est = 2 * tm * k * 2 + 4 * k * tn * 4 + 2 * tm * tn * 2 + 2 * k * tn * 2 + 3 * tm * tn * 4
    return pl.pallas_call(
        _glu_kernel,
        grid=(m // tm, nj),
        in_specs=[pl.BlockSpec((tm, k), lambda i, j: (i, 0)),
                  pl.BlockSpec((k, tn), lambda i, j: (0, j)),
                  pl.BlockSpec((k, tn), lambda i, j: (0, j + nj))],
        out_specs=pl.BlockSpec((tm, tn), lambda i, j: (i, j)),
        out_shape=jax.ShapeDtypeStruct((m, n), BF16),
        compiler_params=_params(est),
        name="s5_glu",
    )(z, w, w)


def _merge_kernel(oa_ref, ob_ref, wa_ref, wb_ref, ga_ref, gb_ref, o_ref):
    a = _bdot(oa_ref[...], wa_ref[...].astype(BF16))
    b = _bdot(ob_ref[...], wb_ref[...].astype(BF16))
    ga, gb = ga_ref[...].astype(F32), gb_ref[...].astype(F32)
    o_ref[...] = (jax.nn.sigmoid(ga) * a + jax.nn.sigmoid(gb) * b).astype(BF16)


def _merge(o_a, o_b, w_a, w_b, proj, tm, tn):
    m, k = o_a.shape
    n = w_a.shape[1]
    ja, jb = OFF_GA // tn, OFF_GB // tn
    est = 4 * tm * k * 2 + 4 * k * tn * 4 + 4 * tm * tn * 4 + 2 * tm * tn * 2 + 2 * k * tn * 2 + 3 * tm * tn * 4
    return pl.pallas_call(
        _merge_kernel,
        grid=(m // tm, n // tn),
        in_specs=[pl.BlockSpec((tm, k), lambda i, j: (i, 0)),
                  pl.BlockSpec((tm, k), lambda i, j: (i, 0)),
                  pl.BlockSpec((k, tn), lambda i, j: (0, j)),
                  pl.BlockSpec((k, tn), lambda i, j: (0, j)),
                  pl.BlockSpec((tm, tn), lambda i, j: (i, j + ja)),
                  pl.BlockSpec((tm, tn), lambda i, j: (i, j + jb))],
        out_specs=pl.BlockSpec((tm, tn), lambda i, j: (i, j)),
        out_shape=jax.ShapeDtypeStruct((m, n), BF16),
        compiler_params=_params(est),
        name="gated_merge",
    )(o_a, o_b, w_a, w_b, proj, proj)


def _outproj_kernel(m_ref, w_ref, x_ref, o_ref):
    o_ref[...] = x_ref[...] + _bdot(m_ref[...], w_ref[...].astype(BF16))


def _outproj(merged, w, x, tm, tn):
    m, k = merged.shape
    n = w.shape[1]
    est = 2 * tm * k * 2 + 2 * k * tn * 4 + 4 * tm * tn * 4 + k * tn * 2 + tm * tn * 4
    return pl.pallas_call(
        _outproj_kernel,
        grid=(m // tm, n // tn),
        in_specs=[pl.BlockSpec((tm, k), lambda i, j: (i, 0)),
                  pl.BlockSpec((k, tn), lambda i, j: (0, j)),
                  pl.BlockSpec((tm, tn), lambda i, j: (i, j))],
        out_specs=pl.BlockSpec((tm, tn), lambda i, j: (i, j)),
        out_shape=jax.ShapeDtypeStruct((m, n), F32),
        compiler_params=_params(est),
        name="out_proj",
    )(merged, w, x)


def _ffn_up_kernel(emit_w, x_ref, g_ref, wa_ref, wb_ref, *rest):
    if emit_w:
        wu_ref, wd_ref, o_ref, wub_ref, wdb_ref, h_scr = rest
        wub_ref[...] = wu_ref[...].astype(BF16)
        wdb_ref[...] = wd_ref[...].astype(BF16)
    else:
        o_ref, h_scr = rest

    @pl.when(pl.program_id(1) == 0)
    def _():
        h_scr[...] = _rms_rows(x_ref[...], g_ref[...]).astype(BF16)

    h = h_scr[...]
    a = _bdot(h, wa_ref[...].astype(BF16))
    b = _bdot(h, wb_ref[...].astype(BF16))
    o_ref[...] = (jax.nn.silu(a) * b).astype(BF16)


def _ffn_up(x, g, w_gate, w_lin, lin_off, tm, tn, w_down=None):
    m, k = x.shape
    n = FFN_HIDDEN
    nj = n // tn
    jl = lin_off // tn
    single = pl.Buffered(1) if tm >= 2048 else None
    est = ((1 if single else 2) * tm * k * 4 + 4 * k * tn * 4 + 2 * tm * tn * 2 + tm * k * 2 + 2 * k * tn * 2
           + 3 * tm * tn * 4)
    in_specs = [pl.BlockSpec((tm, k), lambda i, j: (i, 0), pipeline_mode=single),
                pl.BlockSpec((1, k), lambda i, j: (0, 0)),
                pl.BlockSpec((k, tn), lambda i, j: (0, j)),
                pl.BlockSpec((k, tn), lambda i, j: (0, j + jl))]
    out_specs = [pl.BlockSpec((tm, tn), lambda i, j: (i, j))]
    out_shape = [jax.ShapeDtypeStruct((m, n), BF16)]
    args = [x, g, w_gate, w_lin]
    if w_down is not None:
        steps = (m // tm) * nj
        rows, cols = w_down.shape[0] // steps, w_gate.shape[1] // steps
        down_spec = pl.BlockSpec((rows, w_down.shape[1]), lambda i, j: (i * nj + j, 0))
        up_spec = pl.BlockSpec((k, cols), lambda i, j: (0, i * nj + j))
        in_specs += [up_spec, down_spec]
        out_specs += [up_spec, down_spec]
        out_shape += [jax.ShapeDtypeStruct(w_gate.shape, BF16), jax.ShapeDtypeStruct(w_down.shape, BF16)]
        args += [w_gate, w_down]
        est += 2 * (rows * w_down.shape[1] + k * cols) * 6
    return pl.pallas_call(
        functools.partial(_ffn_up_kernel, w_down is not None),
        grid=(m // tm, nj),
        in_specs=in_specs,
        out_specs=out_specs,
        out_shape=out_shape,
        scratch_shapes=[pltpu.VMEM((tm, k), BF16)],
        compiler_params=_params(est),
        name="ffn_up",
    )(*args)


FFN_DOWN_COLS = 512


def _ffn_down_kernel(a_ref, w_ref, x_ref, g_ref, o_ref):
    kk = pl.program_id(1)

    @pl.when(kk == 0)
    def _():
        o_ref[...] = x_ref[...]

    a = a_ref[...]
    for n0 in range(0, o_ref.shape[1], FFN_DOWN_COLS):
        cols = slice(n0, n0 + FFN_DOWN_COLS)
        o_ref[:, cols] += _bdot(a, w_ref[:, cols])

    @pl.when(kk == pl.num_programs(1) - 1)
    def _():
        o_ref[...] = _rms_rows(o_ref[...], g_ref[...])


def _ffn_down(act, w, x, g, tm, tk):
    m, k = act.shape
    n = w.shape[1]
    est = 2 * tm * tk * 2 + 2 * tk * n * 2 + 4 * tm * n * 4 + 2 * tm * n * 4
    return pl.pallas_call(
        _ffn_down_kernel,
        grid=(m // tm, k // tk),
        in_specs=[pl.BlockSpec((tm, tk), lambda i, kk: (i, kk)),
                  pl.BlockSpec((tk, n), lambda i, kk: (kk, 0)),
                  pl.BlockSpec((tm, n), lambda i, kk: (i, 0)),
                  pl.BlockSpec((1, n), lambda i, kk: (0, 0))],
        out_specs=pl.BlockSpec((tm, n), lambda i, kk: (i, 0)),
        out_shape=jax.ShapeDtypeStruct((m, n), F32),
        compiler_params=_params(est),
        name="ffn_down_final_norm",
    )(act, w, x, g)


def _lower_bound_kernel(lb_ref, o_ref):
    x = lb_ref[...]
    mx = jnp.max(x, axis=0, keepdims=True)
    e = jnp.exp(x - mx)
    o_ref[...] = e[0:1, :] / jnp.sum(e, axis=0, keepdims=True)


def _lower_bound(lb_logits):
    return pl.pallas_call(
        _lower_bound_kernel,
        out_shape=jax.ShapeDtypeStruct((1, HGRN_WIDTH), F32),
        name="hgrn_lower_bound",
    )(lb_logits)


def _split3(x):
    hi = x.astype(BF16)
    r = x - hi.astype(F32)
    mid = r.astype(BF16)
    lo = (r - mid.astype(F32)).astype(BF16)
    return hi, mid, lo


def _rows_ref(b, block, pick):
    parts = []
    for s0 in range(0, HGRN_CHUNK, block):
        parts.append(jnp.broadcast_to(b[s0 + pick:s0 + pick + 1, :], (block, b.shape[1])))
    return parts[0] if len(parts) == 1 else jnp.concatenate(parts, axis=0)


def _hgrn_prompt_kernel(n_chunks, q_ref, f_ref, i_ref, og_ref, lb_ref, gn_ref, oa_ref, s_ref, st_scr):
    t = pl.program_id(1)

    @pl.when(t == 0)
    def _():
        st_scr[...] = jnp.zeros(st_scr.shape, F32)

    c_ = HGRN_CHUNK
    row = lax.broadcasted_iota(jnp.int32, (c_, c_), 0)
    col = lax.broadcasted_iota(jnp.int32, (c_, c_), 1)
    tri = (col <= row).astype(BF16)
    m_diag = (row // HGRN_SUB == col // HGRN_SUB) & (col <= row)
    levels = (128, 64, 32)
    m_lvl = [(row // bs == col // bs) & (row % bs >= bs // 2) & (col % bs < bs // 2) for bs in levels]
    rcol = lax.broadcasted_iota(jnp.int32, (c_, 1), 0)
    second = [(rcol % bs) >= bs // 2 for bs in levels]

    lb = lb_ref[...]
    gn = gn_ref[...]

    def chunk(c, carry):
        r0 = pl.multiple_of(c * c_, c_)
        rows = pl.ds(r0, c_)
        f = lb + (1.0 - lb) * jax.nn.sigmoid(f_ref[rows, :].astype(F32))
        g = jnp.log(f)
        k_all = 1.0 - f
        g_hi, g_mid, g_lo = _split3(g)
        b = _bdot(tri, g_hi) + _bdot(tri, g_mid) + _bdot(tri, g_lo)
        b_last = b[c_ - 1:c_, :]
        e_in = jnp.exp(b)
        e_upd = jnp.exp(b_last - b)
        dec_all = jnp.exp(b_last)
        b_mid = _rows_ref(b, HGRN_SUB, HGRN_SUB // 2)
        e_dq = jnp.exp(b - b_mid)
        e_dk = jnp.exp(b_mid - b)
        e_lvl = []
        for li, bs in enumerate(levels):
            d = b - _rows_ref(b, bs, bs // 2 - 1)
            e_lvl.append(jnp.exp(jnp.where(second[li], d, -d)))
        q_all = q_ref[rows, :].astype(F32)
        v_all = i_ref[rows, :].astype(F32)
        og = og_ref[rows, :].astype(F32)
        for h in range(HEADS):
            ls = slice(h * HEAD_DIM, (h + 1) * HEAD_DIM)
            q, k, v = q_all[:, ls], k_all[:, ls], v_all[:, ls]
            nt = (((1,), (1,)), ((), ()))
            sc = lax.dot_general((q * e_dq[:, ls]).astype(BF16), (k * e_dk[:, ls]).astype(BF16), nt,
                                 preferred_element_type=F32)
            scores = jnp.where(m_diag, sc, 0.0)
            for li in range(len(levels)):
                e = e_lvl[li][:, ls]
                sc = lax.dot_general((q * e).astype(BF16), (k * e).astype(BF16), nt, preferred_element_type=F32)
                scores = jnp.where(m_lvl[li], sc, scores)
            v_b = v.astype(BF16)
            st = st_scr[h]
            o = _bdot(scores.astype(BF16), v_b)
            o = o + lax.dot_general((q * e_in[:, ls]).astype(BF16), st.astype(BF16), nt, preferred_element_type=F32)
            upd = lax.dot_general(v_b, (k * e_upd[:, ls]).astype(BF16), (((0,), (0,)), ((), ())),
                                  preferred_element_type=F32)
            st_scr[h] = dec_all[:, ls] * st + upd
            o = o * lax.rsqrt(jnp.mean(o * o, axis=-1, keepdims=True) + RMS_EPS) * gn[:, ls]
            oa_ref[rows, ls] = (o * jax.nn.silu(og[:, ls])).astype(BF16)
        return carry

    lax.fori_loop(0, n_chunks, chunk, 0)

    @pl.when(t == pl.num_programs(1) - 1)
    def _():
        for h in range(HEADS):
            s_ref[0, h] = st_scr[h].T


def _hgrn_prompt(proj, lb, gn, batch, seq, tt):
    m = batch * seq
    w = HGRN_WIDTH
    nt = seq // tt
    blk = lambda off: pl.BlockSpec((tt, w), lambda b, t, off=off: (b * nt + t, off // w))
    est = 8 * tt * w * 4 + 2 * tt * w * 2 + 3 * HEADS * HEAD_DIM * HEAD_DIM * 4 + 24 * HGRN_CHUNK * w * 4
    return pl.pallas_call(
        functools.partial(_hgrn_prompt_kernel, tt // HGRN_CHUNK),
        grid=(batch, nt),
        in_specs=[blk(OFF_Q), blk(OFF_F), blk(OFF_I), blk(OFF_OG),
                  pl.BlockSpec((1, w), lambda b, t: (0, 0)),
                  pl.BlockSpec((1, w), lambda b, t: (0, 0))],
        out_specs=[pl.BlockSpec((tt, w), lambda b, t: (b * nt + t, 0)),
                   pl.BlockSpec((1, HEADS, HEAD_DIM, HEAD_DIM), lambda b, t: (b, 0, 0, 0))],
        out_shape=[jax.ShapeDtypeStruct((m, w), BF16),
                   jax.ShapeDtypeStruct((batch, HEADS, HEAD_DIM, HEAD_DIM), F32)],
        scratch_shapes=[pltpu.VMEM((HEADS, HEAD_DIM, HEAD_DIM), F32)],
        compiler_params=_params(est),
        name="hgrn2_prompt",
    )(proj, proj, proj, proj, lb, gn)


def _hgrn_sample_kernel(nb, qt_ref, ft_ref, v_ref, og_ref, lbc_ref, gn_ref, s0_ref, oa_ref, s_ref, o_scr):
    lbc = lbc_ref[...]
    f_t = lbc + (1.0 - lbc) * jax.nn.sigmoid(ft_ref[...])
    q_t = qt_ref[...]
    v = v_ref[...].astype(F32)
    for b in range(nb):
        v_b = v[b:b + 1, :]
        s_new = v_b + f_t[:, b:b + 1] * (s0_ref[b, 0] - v_b)
        s_ref[b, 0] = s_new
        o_scr[b:b + 1, :] = jnp.sum(q_t[:, b:b + 1] * s_new, axis=0, keepdims=True)
    o = o_scr[...]
    o = o * lax.rsqrt(jnp.mean(o * o, axis=-1, keepdims=True) + RMS_EPS) * gn_ref[...]
    oa_ref[...] = (o * jax.nn.silu(og_ref[...].astype(F32))).astype(BF16)


def _hgrn_sample(q_t, f_t, proj, lb_col, gn, s0):
    nb = s0.shape[0]
    hd = HEAD_DIM
    est = 4 * nb * hd * hd * 4 + 16 * nb * hd * 4
    return pl.pallas_call(
        functools.partial(_hgrn_sample_kernel, nb),
        grid=(HEADS,),
        in_specs=[pl.BlockSpec((hd, nb), lambda h: (h, 0)),
                  pl.BlockSpec((hd, nb), lambda h: (h, 0)),
                  pl.BlockSpec((nb, hd), lambda h: (0, OFF_I // hd + h)),
                  pl.BlockSpec((nb, hd), lambda h: (0, OFF_OG // hd + h)),
                  pl.BlockSpec((hd, 1), lambda h: (h, 0)),
                  pl.BlockSpec((1, hd), lambda h: (0, h)),
                  pl.BlockSpec((nb, 1, hd, hd), lambda h: (0, h, 0, 0))],
        out_specs=[pl.BlockSpec((nb, hd), lambda h: (0, h)),
                   pl.BlockSpec((nb, 1, hd, hd), lambda h: (0, h, 0, 0))],
        out_shape=[jax.ShapeDtypeStruct((nb, HGRN_WIDTH), BF16),
                   jax.ShapeDtypeStruct(s0.shape, F32)],
        scratch_shapes=[pltpu.VMEM((nb, hd), F32)],
        compiler_params=_params(est),
        name="hgrn2_sample",
    )(q_t, f_t, proj, proj, lb_col, gn, s0)


def _cmul(ar, ai, br, bi):
    return ar * br - ai * bi, ar * bi + ai * br


def _gelu_exact(x):
    return 0.5 * x * (1.0 + lax.erf(x * math.sqrt(0.5)))


def _s5_discretise_kernel(lre_ref, lim_ref, ldt_ref, pre_ref, pim_ref, zre_ref, zim_ref):
    lam_re, lam_im = lre_ref[...], lim_ref[...]
    dt = jnp.exp(ldt_ref[...])
    mag = jnp.exp(lam_re * dt)
    ab_re = mag * jnp.cos(lam_im * dt)
    ab_im = mag * jnp.sin(lam_im * dt)
    den = lam_re * lam_re + lam_im * lam_im
    nr, ni = ab_re - 1.0, ab_im
    zre_ref[...] = (nr * lam_re + ni * lam_im) / den
    zim_ref[...] = (ni * lam_re - nr * lam_im) / den
    pr, pi = ab_re, ab_im
    for kk in range(N_POW):
        pre_ref[:, kk, :] = pr
        pim_ref[:, kk, :] = pi
        pr, pi = _cmul(pr, pi, pr, pi)


def _s5_discretise(lam_re, lam_im, log_dt):
    g, n = lam_re.shape
    return pl.pallas_call(
        _s5_discretise_kernel,
        out_shape=[jax.ShapeDtypeStruct((g, N_POW, n), F32), jax.ShapeDtypeStruct((g, N_POW, n), F32),
                   jax.ShapeDtypeStruct((g, n), F32), jax.ShapeDtypeStruct((g, n), F32)],
        name="s5_discretise",
    )(lam_re, lam_im, log_dt)


def _pow_rows(xr, xi, prow_re, prow_im, gi, first, exps, nbits=4):
    for kk in range(nbits):
        ar, ai = prow_re[gi, first + kk:first + kk + 1, :], prow_im[gi, first + kk:first + kk + 1, :]
        yr, yi = _cmul(xr, xi, ar, ai)
        hit = ((exps >> kk) & 1) == 1
        xr, xi = jnp.where(hit, yr, xr), jnp.where(hit, yi, xi)
    return xr, xi


def _s5_weights_kernel(gb, prow_re, prow_im, zre_ref, zim_ref, btre_ref, btim_ref, cre_ref, cim_ref,
                       wt_ref, wx_ref, wc_ref, apw_ref):
    fl = S5_FLAT
    srow = lax.broadcasted_iota(jnp.int32, (fl, 1), 0) // S5_GROUP
    jrow = lax.broadcasted_iota(jnp.int32, (S5_STATE, 1), 0)
    for gi in range(gb):
        zr, zi = zre_ref[gi], zim_ref[gi]
        bbr, bbi = _cmul(zr, zi, btre_ref[gi], btim_ref[gi])
        xr, xi = _pow_rows(jnp.concatenate([bbr] * S5_CHUNK, axis=0), jnp.concatenate([bbi] * S5_CHUNK, axis=0),
                           prow_re, prow_im, gi, 0, (S5_CHUNK - 1) - srow)
        wx_ref[gi] = jnp.concatenate([xr, xi], axis=1).astype(BF16)
        cr, ci = _pow_rows(jnp.concatenate([cre_ref[gi]] * S5_CHUNK, axis=0),
                           jnp.concatenate([cim_ref[gi]] * S5_CHUNK, axis=0), prow_re, prow_im, gi, 0, srow)
        g0t = jnp.concatenate([cr, -ci], axis=1)
        c1r, c1i = _cmul(cr, ci, prow_re[gi, 0:1, :], prow_im[gi, 0:1, :])
        wc_ref[gi] = jnp.concatenate([c1r, -c1i], axis=1).astype(BF16)
        kflat = lax.dot_general(jnp.concatenate([bbr, bbi], axis=1), g0t, (((1,), (1,)), ((), ())),
                                preferred_element_type=F32, precision=lax.Precision.HIGHEST)
        pieces = [kflat]
        for s in range(1, S5_CHUNK):
            pieces.append(jnp.concatenate([jnp.zeros((S5_GROUP, s * S5_GROUP), F32),
                                           kflat[:, :fl - s * S5_GROUP]], axis=1))
        wt_ref[gi] = jnp.concatenate(pieces, axis=0).astype(BF16)
        pr = jnp.ones((S5_STATE, S5_STATE), F32)
        pi = jnp.zeros((S5_STATE, S5_STATE), F32)
        pr, pi = _pow_rows(pr, pi, prow_re, prow_im, gi, 4, jrow, nbits=6)
        apw_ref[gi] = jnp.concatenate([pr, pi], axis=1)


def _s5_weights(prow_re, prow_im, z_re, z_im, bt_re, bt_im, c_re, c_im, gb):
    g = S5_GROUPS
    n, p, fl = S5_STATE, S5_GROUP, S5_FLAT
    b3 = lambda a, b: pl.BlockSpec((gb, a, b), lambda i: (i, 0, 0))
    return pl.pallas_call(
        functools.partial(_s5_weights_kernel, gb),
        grid=(g // gb,),
        in_specs=[b3(N_POW, n), b3(N_POW, n), b3(1, n), b3(1, n), b3(p, n), b3(p, n), b3(p, n), b3(p, n)],
        out_specs=[b3(fl, fl), b3(fl, 2 * n), b3(fl, 2 * n), b3(n, 2 * n)],
        out_shape=[jax.ShapeDtypeStruct((g, fl, fl), BF16), jax.ShapeDtypeStruct((g, fl, 2 * n), BF16),
                   jax.ShapeDtypeStruct((g, fl, 2 * n), BF16), jax.ShapeDtypeStruct((g, n, 2 * n), F32)],
        name="s5_chunk_weights",
    )(prow_re, prow_im, z_re, z_im, bt_re, bt_im, c_re, c_im)


LANE_GROUPS = 128 // S5_GROUP
RELAYOUT_ROWS = 64


def _block_transpose8(xs):
    lane_blk = lax.broadcasted_iota(jnp.int32, (1, 128), 1) // S5_GROUP
    xs = list(xs)
    for d in (4, 2, 1):
        upper = (lane_blk & d) != 0
        nxt = list(xs)
        for a in range(8):
            if a & d:
                continue
            lo, hi = xs[a], xs[a + d]
            nxt[a] = jnp.where(upper, pltpu.roll(hi, S5_GROUP * d, axis=1), lo)
            nxt[a + d] = jnp.where(upper, hi, pltpu.roll(lo, 128 - S5_GROUP * d, axis=1))
        xs = nxt
    return xs


def _s5_prompt_kernel(nseq, u_ref, wt_ref, wx_ref, wc_ref, apw_ref, a16_ref, d_ref, z_ref, hend_ref,
                      tok_scr, flat_scr, x_scr, h_scr):
    n = S5_STATE
    gb = LANE_GROUPS
    rows = u_ref.shape[0] // S5_CHUNK
    slab = 2 * nseq
    nlo = rows // slab
    rb = RELAYOUT_ROWS
    halves = S5_FLAT // 128

    tok_scr[...] = u_ref[...].astype(F32)

    def to_flat(r, carry):
        r0 = pl.multiple_of(r * rb, rb)
        for sh in range(halves):
            xs = [tok_scr[pl.ds(r0 * S5_CHUNK + 8 * sh + s, rb, stride=S5_CHUNK), :] for s in range(8)]
            ys = _block_transpose8(xs)
            for gi in range(gb):
                flat_scr[gi, pl.ds(r0, rb), sh * 128:(sh + 1) * 128] = ys[gi]
        return carry

    lax.fori_loop(0, rows // rb, to_flat, 0)

    odd = (lax.broadcasted_iota(jnp.int32, (slab, 1), 0) % 2) == 1
    for gi in range(gb):
        u = flat_scr[gi]
        u_b = u.astype(BF16)
        x_scr[gi] = _bdot(u_b, wx_ref[gi])
        ar, ai = a16_ref[gi, :, :n], a16_ref[gi, :, n:]
        hr = jnp.zeros((slab, n), F32)
        hi = jnp.zeros((slab, n), F32)
        for j in range(nlo):
            rs = pl.ds(j, slab, stride=nlo)
            h_scr[gi, rs, :] = jnp.concatenate([hr, hi], axis=1)
            xj = x_scr[gi, rs, :]
            nr, ni = _cmul(hr, hi, ar, ai)
            hr = nr + xj[:, :n]
            hi = ni + xj[:, n:]
        mr = jnp.where(odd, pltpu.roll(hr, 1, axis=0), 0.0)
        mi = jnp.where(odd, pltpu.roll(hi, 1, axis=0), 0.0)
        pr, pi = apw_ref[gi, :, :n], apw_ref[gi, :, n:]
        for k in range(1, slab, 2):
            cr, ci = _cmul(pr, pi, mr[k:k + 1], mi[k:k + 1])
            h_scr[gi, k * nlo:(k + 1) * nlo, :n] += cr
            h_scr[gi, k * nlo:(k + 1) * nlo, n:] += ci
        lr, li = _cmul(pr[nlo - 1:nlo], pi[nlo - 1:nlo], ar, ai)
        er, ei = _cmul(mr, mi, lr, li)
        hend_ref[gi] = jnp.concatenate([hr + er, hi + ei], axis=1)
        y = (_bdot(u_b, wt_ref[gi]) + d_ref[gi] * u
             + lax.dot_general(h_scr[gi].astype(BF16), wc_ref[gi], (((1,), (1,)), ((), ())),
                               preferred_element_type=F32))
        flat_scr[gi] = _gelu_exact(y)

    def to_tokens(r, carry):
        r0 = pl.multiple_of(r * rb, rb)
        for sh in range(halves):
            ys = [flat_scr[gi, pl.ds(r0, rb), sh * 128:(sh + 1) * 128] for gi in range(gb)]
            xs = _block_transpose8(ys)
            for s in range(8):
                tok_scr[pl.ds(r0 * S5_CHUNK + 8 * sh + s, rb, stride=S5_CHUNK), :] = xs[s]
        return carry

    lax.fori_loop(0, rows // rb, to_tokens, 0)
    z_ref[...] = tok_scr[...].astype(BF16)


def _s5_prompt(proj, wt, wx, wc, apw, a16, d_flat, nseq):
    m = proj.shape[0]
    g, n, fl, gb = S5_GROUPS, S5_STATE, S5_FLAT, LANE_GROUPS
    rows = m // S5_CHUNK
    b3 = lambda a, b: pl.BlockSpec((gb, a, b), lambda i: (i, 0, 0))
    est = (4 * m * 128 * 2 + m * 128 * 4 + gb * rows * (fl + 4 * n) * 4 + 6 * rows * fl * 4
           + 2 * gb * (fl * fl + 4 * n * fl) * 2)
    return pl.pallas_call(
        functools.partial(_s5_prompt_kernel, nseq),
        grid=(g // gb,),
        in_specs=[pl.BlockSpec((m, 128), lambda i: (0, OFF_U // 128 + i)),
                  b3(fl, fl), b3(fl, 2 * n), b3(fl, 2 * n), b3(n, 2 * n), b3(1, 2 * n), b3(1, fl)],
        out_specs=[pl.BlockSpec((m, 128), lambda i: (0, i)), b3(2 * nseq, 2 * n)],
        out_shape=[jax.ShapeDtypeStruct((m, S5_WIDTH), BF16), jax.ShapeDtypeStruct((g, 2 * nseq, 2 * n), F32)],
        scratch_shapes=[pltpu.VMEM((m, 128), F32), pltpu.VMEM((gb, rows, fl), F32),
                        pltpu.VMEM((gb, rows, 2 * n), F32), pltpu.VMEM((gb, rows, 2 * n), F32)],
        compiler_params=_params(est),
        name="s5_prompt",
    )(proj, wt, wx, wc, apw, a16, d_flat)


def _s5_sample_kernel(gb, u_ref, bst_ref, cre_ref, cim_ref, a1_ref, hre_ref, him_ref, d_ref,
                      z_ref, ore_ref, oim_ref):
    n = S5_STATE
    u = u_ref[...].astype(F32)
    lane_g = lax.broadcasted_iota(jnp.int32, (1, gb * S5_GROUP), 1) // S5_GROUP
    bst = bst_ref[...]
    cst = jnp.concatenate([cre_ref[...], -cim_ref[...]], axis=1)
    row_g = lax.broadcasted_iota(jnp.int32, (gb * S5_GROUP, 1), 0) // S5_GROUP
    y = d_ref[...] * u
    for gi in range(gb):
        mine = lane_g == gi
        bu = _bdot(jnp.where(mine, u, 0.0).astype(BF16), bst)
        ar, ai = a1_ref[gi, :, :n], a1_ref[gi, :, n:]
        nr, ni = _cmul(hre_ref[gi], him_ref[gi], ar, ai)
        hr = nr + bu[:, :n]
        hi = ni + bu[:, n:]
        ore_ref[gi] = hr
        oim_ref[gi] = hi
        cmat = jnp.where(row_g == gi, cst, 0.0).astype(BF16)
        y = y + lax.dot_general(jnp.concatenate([hr, hi], axis=1).astype(BF16), cmat, (((1,), (1,)), ((), ())),
                                preferred_element_type=F32)
    z_ref[...] = _gelu_exact(y).astype(BF16)


def _s5_sample(proj, bstack, c_re, c_im, a1, h_re, h_im, d_row, gb):
    nb = proj.shape[0]
    g, n, p = S5_GROUPS, S5_STATE, S5_GROUP
    lanes = gb * p
    b3 = lambda a, b: pl.BlockSpec((gb, a, b), lambda i: (i, 0, 0))
    st = b3(nb, n)
    return pl.pallas_call(
        functools.partial(_s5_sample_kernel, gb),
        grid=(g // gb,),
        in_specs=[pl.BlockSpec((nb, lanes), lambda i: (0, OFF_U // lanes + i)),
                  pl.BlockSpec((lanes, 2 * n), lambda i: (i, 0)),
                  pl.BlockSpec((lanes, n), lambda i: (i, 0)), pl.BlockSpec((lanes, n), lambda i: (i, 0)),
                  b3(1, 2 * n), st, st,
                  pl.BlockSpec((1, lanes), lambda i: (0, i))],
        out_specs=[pl.BlockSpec((nb, lanes), lambda i: (0, i)), st, st],
        out_shape=[jax.ShapeDtypeStruct((nb, S5_WIDTH), BF16),
                   jax.ShapeDtypeStruct((g, nb, n), F32), jax.ShapeDtypeStruct((g, nb, n), F32)],
        name="s5_sample",
    )(proj, bstack, c_re, c_im, a1, h_re, h_im, d_row)


def _dense_tail(x, proj, o_a, z, w, tm, ffn_bf16=None):
    o_b = _glu(z, w['w_s5_glu'], tm, 512)
    merged = _merge(o_a, o_b, w['w_proj_a'], w['w_proj_b'], proj, tm, 512)
    x1 = _outproj(merged, w['w_out'], x, tm, 512)
    if ffn_bf16 is None:
        act, *ffn_bf16 = _ffn_up(x1, w['norm2_g'], w['w_ffn_up'], w['w_ffn_up'], FFN_HIDDEN, tm, 256,
                                 w_down=w['w_ffn_down'])
    else:
        act, = _ffn_up(x1, w['norm2_g'], ffn_bf16[0], ffn_bf16[0], FFN_HIDDEN, tm, 256)
    y = _ffn_down(act, ffn_bf16[1], x1, w['final_norm_g'], min(tm, 1024), 512)
    return y, ffn_bf16


def kernel(x_prompt, x_sample, state_hgrn, state_s5_re, state_s5_im, lb_logits, norm1_g, w_in, hgrn_norm_g,
           s5_lam_re, s5_lam_im, s5_log_dt, s5_B_re, s5_B_im, s5_C_re, s5_C_im, s5_D, w_s5_glu, w_proj_a,
           w_proj_b, w_out, norm2_g, w_ffn_up, w_ffn_down, final_norm_g):
    l = 0
    bp, seq, d = x_prompt.shape
    nb = x_sample.shape[0]
    g, n, p = S5_GROUPS, S5_STATE, S5_GROUP
    w = {'w_s5_glu': w_s5_glu[l], 'w_proj_a': w_proj_a[l], 'w_proj_b': w_proj_b[l], 'w_out': w_out[l],
         'norm2_g': norm2_g[l][None, :], 'w_ffn_up': w_ffn_up[l], 'w_ffn_down': w_ffn_down[l],
         'final_norm_g': final_norm_g[None, :]}
    g1 = norm1_g[l][None, :]
    gn = hgrn_norm_g[l][None, :]

    xp = x_prompt.reshape(bp * seq, d)
    xs = x_sample.reshape(nb, d)
    proj_p, w_in_bf16 = _inproj(xp, g1, w_in[l], 2048, 512, emit_w=True)
    proj_s, = _inproj(xs, g1, w_in_bf16, nb, 512)

    lb = _lower_bound(lb_logits)
    oa_p, sh_p = _hgrn_prompt(proj_p, lb, gn, bp, seq, 512)
    q_t = proj_s[:, OFF_Q:OFF_Q + HGRN_WIDTH].astype(F32).T
    f_t = proj_s[:, OFF_F:OFF_F + HGRN_WIDTH].astype(F32).T
    oa_s, sh_s = _hgrn_sample(q_t, f_t, proj_s, lb.reshape(HGRN_WIDTH, 1), gn, state_hgrn[l])

    prow_re, prow_im, z_re, z_im = _s5_discretise(s5_lam_re[l], s5_lam_im[l], s5_log_dt[l][:, None])
    bt_re, bt_im = s5_B_re[l].transpose(0, 2, 1), s5_B_im[l].transpose(0, 2, 1)
    wt, wx, wc, apw = _s5_weights(prow_re, prow_im, z_re[:, None, :], z_im[:, None, :], bt_re, bt_im,
                                  s5_C_re[l], s5_C_im[l], 8)
    d_gp = s5_D[l].reshape(g, 1, p)
    d_flat = jnp.tile(d_gp, (1, 1, S5_CHUNK))
    a16 = jnp.concatenate([prow_re[:, 4:5, :], prow_im[:, 4:5, :]], axis=2)
    a1 = jnp.concatenate([prow_re[:, 0:1, :], prow_im[:, 0:1, :]], axis=2)

    z_p, hend = _s5_prompt(proj_p, wt, wx, wc, apw, a16, d_flat, bp)
    hend = hend[:, 1::2, :]
    s5re_p = hend[:, :, :n].transpose(1, 0, 2)
    s5im_p = hend[:, :, n:].transpose(1, 0, 2)

    bstack = wx[:, S5_FLAT - p:, :].reshape(g * p, 2 * n)
    gbs = 128 // p
    c_rows_re, c_rows_im = s5_C_re[l].reshape(g * p, n), s5_C_im[l].reshape(g * p, n)
    z_s, s5re_s, s5im_s = _s5_sample(proj_s, bstack, c_rows_re, c_rows_im, a1,
                                     state_s5_re[l].transpose(1, 0, 2), state_s5_im[l].transpose(1, 0, 2),
                                     s5_D[l][None, :], gbs)

    y_p, ffn_bf16 = _dense_tail(xp, proj_p, oa_p, z_p, w, 2048)
    y_s, _ = _dense_tail(xs, proj_s, oa_s, z_s, w, nb, ffn_bf16)

    return (y_p.reshape(bp, seq, d), y_s.reshape(nb, 1, d),
            sh_p[None], s5re_p[None], s5im_p[None],
            sh_s[None], s5re_s.transpose(1, 0, 2)[None], s5im_s.transpose(1, 0, 2)[None])
```

```python
import functools
import math

import jax
import jax.numpy as jnp
from jax import lax
from jax.experimental import pallas as pl
from jax.experimental.pallas import tpu as pltpu

F32 = jnp.float32
BF16 = jnp.bfloat16

D_MODEL = 2048
HGRN_WIDTH = 1024
HEAD_DIM = 128
HEADS = 8
S5_WIDTH = 1024
S5_GROUPS = 64
S5_GROUP = 16
S5_STATE = 64
FFN_HIDDEN = 5632
IN_PROJ_WIDTH = 4 * HGRN_WIDTH + S5_WIDTH + 2 * D_MODEL
RMS_EPS = 1e-6

OFF_Q, OFF_F, OFF_I, OFF_OG = 0, HGRN_WIDTH, 2 * HGRN_WIDTH, 3 * HGRN_WIDTH
OFF_U = 4 * HGRN_WIDTH
OFF_GA = OFF_U + S5_WIDTH
OFF_GB = OFF_GA + D_MODEL

HGRN_SUB = 16
HGRN_CHUNK = 128
S5_CHUNK = 16
S5_FLAT = S5_CHUNK * S5_GROUP
N_POW = 10

V7X_VMEM_BYTES = 64 * 1024 * 1024
VMEM_LIMIT_CAP = V7X_VMEM_BYTES - 6 * 1024 * 1024


def _params(est_bytes):
    limit = min(max(int(est_bytes * 1.25) + (4 << 20), 32 << 20), VMEM_LIMIT_CAP)
    return pltpu.CompilerParams(vmem_limit_bytes=limit)


def _bdot(a, b):
    return jnp.dot(a, b, preferred_element_type=F32)


def _rms_rows(x, g):
    ms = jnp.mean(x * x, axis=-1, keepdims=True)
    return x * lax.rsqrt(ms + RMS_EPS) * g


def _inproj_kernel(emit_w, x_ref, g_ref, w_ref, *rest):
    if emit_w:
        ws_ref, o_ref, wb_ref, h_scr = rest
        wb_ref[...] = ws_ref[...].astype(BF16)
    else:
        o_ref, h_scr = rest

    @pl.when(pl.program_id(1) == 0)
    def _():
        h_scr[...] = _rms_rows(x_ref[...], g_ref[...]).astype(BF16)

    o_ref[...] = _bdot(h_scr[...], w_ref[...].astype(BF16)).astype(BF16)


def _inproj(x, g, w, tm, tn, emit_w=False):
    m, k = x.shape
    n = w.shape[1]
    single = pl.Buffered(1) if tm >= 2048 else None
    est = ((1 if single else 2) * tm * k * 4 + 2 * k * tn * 4 + 2 * tm * tn * 2 + tm * k * 2 + k * tn * 2
           + tm * tn * 6)
    nj = n // tn
    in_specs = [pl.BlockSpec((tm, k), lambda i, j: (i, 0), pipeline_mode=single),
                pl.BlockSpec((1, k), lambda i, j: (0, 0)),
                pl.BlockSpec((k, tn), lambda i, j: (0, j))]
    out_specs = [pl.BlockSpec((tm, tn), lambda i, j: (i, j))]
    out_shape = [jax.ShapeDtypeStruct((m, n), BF16)]
    args = [x, g, w]
    if emit_w:
        cols = n // ((m // tm) * nj)
        slice_spec = pl.BlockSpec((k, cols), lambda i, j: (0, i * nj + j))
        in_specs.append(slice_spec)
        out_specs.append(slice_spec)
        out_shape.append(jax.ShapeDtypeStruct((k, n), BF16))
        args.append(w)
        est += 2 * k * cols * 6
    return pl.pallas_call(
        functools.partial(_inproj_kernel, emit_w),
        grid=(m // tm, nj),
        in_specs=in_specs,
        out_specs=out_specs,
        out_shape=out_shape,
        scratch_shapes=[pltpu.VMEM((tm, k), BF16)],
        compiler_params=_params(est),
        name="inproj",
    )(*args)


def _glu_kernel(z_ref, wa_ref, wb_ref, o_ref):
    z = z_ref[...]
    a = _bdot(z, wa_ref[...].astype(BF16))
    b = _bdot(z, wb_ref[...].astype(BF16))
    o_ref[...] = (a * jax.nn.sigmoid(b)).astype(BF16)


def _glu(z, w, tm, tn):
    m, k = z.shape
    n = w.shape[1] // 2
    nj = n // tn
    est = 2 * tm * k * 2 + 4 * k * tn * 4 + 2 * tm * tn * 2 + 2 * k * tn * 2 + 3 * tm * tn * 4
    return pl.pallas_call(
        _glu_kernel,
        grid=(m // tm, nj),
        in_specs=[pl.BlockSpec((tm, k), lambda i, j: (i, 0)),
                  pl.BlockSpec((k, tn), lambda i, j: (0, j)),
                  pl.BlockSpec((k, tn), lambda i, j: (0, j + nj))],
        out_specs=pl.BlockSpec((tm, tn), lambda i, j: (i, j)),
        out_shape=jax.ShapeDtypeStruct((m, n), BF16),
        compiler_params=_params(est),
        name="s5_glu",
    )(z, w, w)


def _merge_kernel(oa_ref, ob_ref, wa_ref, wb_ref, ga_ref, gb_ref, o_ref):
    a = _bdot(oa_ref[...], wa_ref[...].astype(BF16))
    b = _bdot(ob_ref[...], wb_ref[...].astype(BF16))
    ga, gb = ga_ref[...].astype(F32), gb_ref[...].astype(F32)
    o_ref[...] = (jax.nn.sigmoid(ga) * a + jax.nn.sigmoid(gb) * b).astype(BF16)


def _merge(o_a, o_b, w_a, w_b, proj, tm, tn):
    m, k = o_a.shape
    n = w_a.shape[1]
    ja, jb = OFF_GA // tn, OFF_GB // tn
    est = 4 * tm * k * 2 + 4 * k * tn * 4 + 4 * tm * tn * 4 + 2 * tm * tn * 2 + 2 * k * tn * 2 + 3 * tm * tn * 4
    return pl.pallas_call(
        _merge_kernel,
        grid=(m // tm, n // tn),
        in_specs=[pl.BlockSpec((tm, k), lambda i, j: (i, 0)),
                  pl.BlockSpec((tm, k), lambda i, j: (i, 0)),
                  pl.BlockSpec((k, tn), lambda i, j: (0, j)),
                  pl.BlockSpec((k, tn), lambda i, j: (0, j)),
                  pl.BlockSpec((tm, tn), lambda i, j: (i, j + ja)),
                  pl.BlockSpec((tm, tn), lambda i, j: (i, j + jb))],
        out_specs=pl.BlockSpec((tm, tn), lambda i, j: (i, j)),
        out_shape=jax.ShapeDtypeStruct((m, n), BF16),
        compiler_params=_params(est),
        name="gated_merge",
    )(o_a, o_b, w_a, w_b, proj, proj)


def _outproj_kernel(m_ref, w_ref, x_ref, o_ref):
    o_ref[...] = x_ref[...] + _bdot(m_ref[...], w_ref[...].astype(BF16))


def _outproj(merged, w, x, tm, tn):
    m, k = merged.shape
    n = w.shape[1]
    est = 2 * tm * k * 2 + 2 * k * tn * 4 + 4 * tm * tn * 4 + k * tn * 2 + tm * tn * 4
    return pl.pallas_call(
        _outproj_kernel,
        grid=(m // tm, n // tn),
        in_specs=[pl.BlockSpec((tm, k), lambda i, j: (i, 0)),
                  pl.BlockSpec((k, tn), lambda i, j: (0, j)),
                  pl.BlockSpec((tm, tn), lambda i, j: (i, j))],
        out_specs=pl.BlockSpec((tm, tn), lambda i, j: (i, j)),
        out_shape=jax.ShapeDtypeStruct((m, n), F32),
        compiler_params=_params(est),
        name="out_proj",
    )(merged, w, x)


def _ffn_up_kernel(emit_w, x_ref, g_ref, wa_ref, wb_ref, *rest):
    if emit_w:
        wu_ref, wd_ref, o_ref, wub_ref, wdb_ref, h_scr = rest
        wub_ref[...] = wu_ref[...].astype(BF16)
        wdb_ref[...] = wd_ref[...].astype(BF16)
    else:
        o_ref, h_scr = rest

    @pl.when(pl.program_id(1) == 0)
    def _():
        h_scr[...] = _rms_rows(x_ref[...], g_ref[...]).astype(BF16)

    h = h_scr[...]
    a = _bdot(h, wa_ref[...].astype(BF16))
    b = _bdot(h, wb_ref[...].astype(BF16))
    o_ref[...] = (jax.nn.silu(a) * b).astype(BF16)


def _ffn_up(x, g, w_gate, w_lin, lin_off, tm, tn, w_down=None):
    m, k = x.shape
    n = FFN_HIDDEN
    nj = n // tn
    jl = lin_off // tn
    single = pl.Buffered(1) if tm >= 2048 else None
    est = ((1 if single else 2) * tm * k * 4 + 4 * k * tn * 4 + 2 * tm * tn * 2 + tm * k * 2 + 2 * k * tn * 2
           + 3 * tm * tn * 4)
    in_specs = [pl.BlockSpec((tm, k), lambda i, j: (i, 0), pipeline_mode=single),
                pl.BlockSpec((1, k), lambda i, j: (0, 0)),
                pl.BlockSpec((k, tn), lambda i, j: (0, j)),
                pl.BlockSpec((k, tn), lambda i, j: (0, j + jl))]
    out_specs = [pl.BlockSpec((tm, tn), lambda i, j: (i, j))]
    out_shape = [jax.ShapeDtypeStruct((m, n), BF16)]
    args = [x, g, w_gate, w_lin]
    if w_down is not None:
        steps = (m // tm) * nj
        rows, cols = w_down.shape[0] // steps, w_gate.shape[1] // steps
        down_spec = pl.BlockSpec((rows, w_down.shape[1]), lambda i, j: (i * nj + j, 0))
        up_spec = pl.BlockSpec((k, cols), lambda i, j: (0, i * nj + j))
        in_specs += [up_spec, down_spec]
        out_specs += [up_spec, down_spec]
        out_shape += [jax.ShapeDtypeStruct(w_gate.shape, BF16), jax.ShapeDtypeStruct(w_down.shape, BF16)]
        args += [w_gate, w_down]
        est += 2 * (rows * w_down.shape[1] + k * cols) * 6
    return pl.pallas_call(
        functools.partial(_ffn_up_kernel, w_down is not None),
        grid=(m // tm, nj),
        in_specs=in_specs,
        out_specs=out_specs,
        out_shape=out_shape,
        scratch_shapes=[pltpu.VMEM((tm, k), BF16)],
        compiler_params=_params(est),
        name="ffn_up",
    )(*args)


FFN_DOWN_COLS = 512


def _ffn_down_kernel(a_ref, w_ref, x_ref, g_ref, o_ref):
    kk = pl.program_id(1)

    @pl.when(kk == 0)
    def _():
        o_ref[...] = x_ref[...]

    a = a_ref[...]
    for n0 in range(0, o_ref.shape[1], FFN_DOWN_COLS):
        cols = slice(n0, n0 + FFN_DOWN_COLS)
        o_ref[:, cols] += _bdot(a, w_ref[:, cols])

    @pl.when(kk == pl.num_programs(1) - 1)
    def _():
        o_ref[...] = _rms_rows(o_ref[...], g_ref[...])


def _ffn_down(act, w, x, g, tm, tk):
    m, k = act.shape
    n = w.shape[1]
    est = 2 * tm * tk * 2 + 2 * tk * n * 2 + 4 * tm * n * 4 + 2 * tm * n * 4
    return pl.pallas_call(
        _ffn_down_kernel,
        grid=(m // tm, k // tk),
        in_specs=[pl.BlockSpec((tm, tk), lambda i, kk: (i, kk)),
                  pl.BlockSpec((tk, n), lambda i, kk: (kk, 0)),
                  pl.BlockSpec((tm, n), lambda i, kk: (i, 0)),
                  pl.BlockSpec((1, n), lambda i, kk: (0, 0))],
        out_specs=pl.BlockSpec((tm, n), lambda i, kk: (i, 0)),
        out_shape=jax.ShapeDtypeStruct((m, n), F32),
        compiler_params=_params(est),
        name="ffn_down_final_norm",
    )(act, w, x, g)


def _lower_bound_kernel(lb_ref, o_ref):
    x = lb_ref[...]
    mx = jnp.max(x, axis=0, keepdims=True)
    e = jnp.exp(x - mx)
    o_ref[...] = e[0:1, :] / jnp.sum(e, axis=0, keepdims=True)


def _lower_bound(lb_logits):
    return pl.pallas_call(
        _lower_bound_kernel,
        out_shape=jax.ShapeDtypeStruct((1, HGRN_WIDTH), F32),
        name="hgrn_lower_bound",
    )(lb_logits)


def _split3(x):
    hi = x.astype(BF16)
    r = x - hi.astype(F32)
    mid = r.astype(BF16)
    lo = (r - mid.astype(F32)).astype(BF16)
    return hi, mid, lo


def _rows_ref(b, block, pick):
    parts = []
    for s0 in range(0, HGRN_CHUNK, block):
        parts.append(jnp.broadcast_to(b[s0 + pick:s0 + pick + 1, :], (block, b.shape[1])))
    return parts[0] if len(parts) == 1 else jnp.concatenate(parts, axis=0)


def _hgrn_prompt_kernel(n_chunks, q_ref, f_ref, i_ref, og_ref, lb_ref, gn_ref, oa_ref, s_ref, st_scr):
    t = pl.program_id(1)

    @pl.when(t == 0)
    def _():
        st_scr[...] = jnp.zeros(st_scr.shape, F32)

    c_ = HGRN_CHUNK
    row = lax.broadcasted_iota(jnp.int32, (c_, c_), 0)
    col = lax.broadcasted_iota(jnp.int32, (c_, c_), 1)
    tri = (col <= row).astype(BF16)
    m_diag = (row // HGRN_SUB == col // HGRN_SUB) & (col <= row)
    levels = (128, 64, 32)
    m_lvl = [(row // bs == col // bs) & (row % bs >= bs // 2) & (col % bs < bs // 2) for bs in levels]
    rcol = lax.broadcasted_iota(jnp.int32, (c_, 1), 0)
    second = [(rcol % bs) >= bs // 2 for bs in levels]

    lb = lb_ref[...]
    gn = gn_ref[...]

    def chunk(c, carry):
        r0 = pl.multiple_of(c * c_, c_)
        rows = pl.ds(r0, c_)
        f = lb + (1.0 - lb) * jax.nn.sigmoid(f_ref[rows, :].astype(F32))
        g = jnp.log(f)
        k_all = 1.0 - f
        g_hi, g_mid, g_lo = _split3(g)
        b = _bdot(tri, g_hi) + _bdot(tri, g_mid) + _bdot(tri, g_lo)
        b_last = b[c_ - 1:c_, :]
        e_in = jnp.exp(b)
        e_upd = jnp.exp(b_last - b)
        dec_all = jnp.exp(b_last)
        b_mid = _rows_ref(b, HGRN_SUB, HGRN_SUB // 2)
        e_dq = jnp.exp(b - b_mid)
        e_dk = jnp.exp(b_mid - b)
        e_lvl = []
        for li, bs in enumerate(levels):
            d = b - _rows_ref(b, bs, bs // 2 - 1)
            e_lvl.append(jnp.exp(jnp.where(second[li], d, -d)))
        q_all = q_ref[rows, :].astype(F32)
        v_all = i_ref[rows, :].astype(F32)
        og = og_ref[rows, :].astype(F32)
        for h in range(HEADS):
            ls = slice(h * HEAD_DIM, (h + 1) * HEAD_DIM)
            q, k, v = q_all[:, ls], k_all[:, ls], v_all[:, ls]
            nt = (((1,), (1,)), ((), ()))
            sc = lax.dot_general((q * e_dq[:, ls]).astype(BF16), (k * e_dk[:, ls]).astype(BF16), nt,
                                 preferred_element_type=F32)
            scores = jnp.where(m_diag, sc, 0.0)
            for li in range(len(levels)):
                e = e_lvl[li][:, ls]
                sc = lax.dot_general((q * e).astype(BF16), (k * e).astype(BF16), nt, preferred_element_type=F32)
                scores = jnp.where(m_lvl[li], sc, scores)
            v_b = v.astype(BF16)
            st = st_scr[h]
            o = _bdot(scores.astype(BF16), v_b)
            o = o + lax.dot_general((q * e_in[:, ls]).astype(BF16), st.astype(BF16), nt, preferred_element_type=F32)
            upd = lax.dot_general(v_b, (k * e_upd[:, ls]).astype(BF16), (((0,), (0,)), ((), ())),
                                  preferred_element_type=F32)
            st_scr[h] = dec_all[:, ls] * st + upd
            o = o * lax.rsqrt(jnp.mean(o * o, axis=-1, keepdims=True) + RMS_EPS) * gn[:, ls]
            oa_ref[rows, ls] = (o * jax.nn.silu(og[:, ls])).astype(BF16)
        return carry

    lax.fori_loop(0, n_chunks, chunk, 0)

    @pl.when(t == pl.num_programs(1) - 1)
    def _():
        for h in range(HEADS):
            s_ref[0, h] = st_scr[h].T


def _hgrn_prompt(proj, lb, gn, batch, seq, tt):
    m = batch * seq
    w = HGRN_WIDTH
    nt = seq // tt
    blk = lambda off: pl.BlockSpec((tt, w), lambda b, t, off=off: (b * nt + t, off // w))
    est = 8 * tt * w * 4 + 2 * tt * w * 2 + 3 * HEADS * HEAD_DIM * HEAD_DIM * 4 + 24 * HGRN_CHUNK * w * 4
    return pl.pallas_call(
        functools.partial(_hgrn_prompt_kernel, tt // HGRN_CHUNK),
        grid=(batch, nt),
        in_specs=[blk(OFF_Q), blk(OFF_F), blk(OFF_I), blk(OFF_OG),
                  pl.BlockSpec((1, w), lambda b, t: (0, 0)),
                  pl.BlockSpec((1, w), lambda b, t: (0, 0))],
        out_specs=[pl.BlockSpec((tt, w), lambda b, t: (b * nt + t, 0)),
                   pl.BlockSpec((1, HEADS, HEAD_DIM, HEAD_DIM), lambda b, t: (b, 0, 0, 0))],
        out_shape=[jax.ShapeDtypeStruct((m, w), BF16),
                   jax.ShapeDtypeStruct((batch, HEADS, HEAD_DIM, HEAD_DIM), F32)],
        scratch_shapes=[pltpu.VMEM((HEADS, HEAD_DIM, HEAD_DIM), F32)],
        compiler_params=_params(est),
        name="hgrn2_prompt",
    )(proj, proj, proj, proj, lb, gn)


def _hgrn_sample_kernel(nb, qt_ref, ft_ref, v_ref, og_ref, lbc_ref, gn_ref, s0_ref, oa_ref, s_ref, o_scr):
    lbc = lbc_ref[...]
    f_t = lbc + (1.0 - lbc) * jax.nn.sigmoid(ft_ref[...])
    q_t = qt_ref[...]
    v = v_ref[...].astype(F32)
    for b in range(nb):
        v_b = v[b:b + 1, :]
        s_new = v_b + f_t[:, b:b + 1] * (s0_ref[b, 0] - v_b)
        s_ref[b, 0] = s_new
        o_scr[b:b + 1, :] = jnp.sum(q_t[:, b:b + 1] * s_new, axis=0, keepdims=True)
    o = o_scr[...]
    o = o * lax.rsqrt(jnp.mean(o * o, axis=-1, keepdims=True) + RMS_EPS) * gn_ref[...]
    oa_ref[...] = (o * jax.nn.silu(og_ref[...].astype(F32))).astype(BF16)


def _hgrn_sample(q_t, f_t, proj, lb_col, gn, s0):
    nb = s0.shape[0]
    hd = HEAD_DIM
    est = 4 * nb * hd * hd * 4 + 16 * nb * hd * 4
    return pl.pallas_call(
        functools.partial(_hgrn_sample_kernel, nb),
        grid=(HEADS,),
        in_specs=[pl.BlockSpec((hd, nb), lambda h: (h, 0)),
                  pl.BlockSpec((hd, nb), lambda h: (h, 0)),
                  pl.BlockSpec((nb, hd), lambda h: (0, OFF_I // hd + h)),
                  pl.BlockSpec((nb, hd), lambda h: (0, OFF_OG // hd + h)),
                  pl.BlockSpec((hd, 1), lambda h: (h, 0)),
                  pl.BlockSpec((1, hd), lambda h: (0, h)),
                  pl.BlockSpec((nb, 1, hd, hd), lambda h: (0, h, 0, 0))],
        out_specs=[pl.BlockSpec((nb, hd), lambda h: (0, h)),
                   pl.BlockSpec((nb, 1, hd, hd), lambda h: (0, h, 0, 0))],
        out_shape=[jax.ShapeDtypeStruct((nb, HGRN_WIDTH), BF16),
                   jax.ShapeDtypeStruct(s0.shape, F32)],
        scratch_shapes=[pltpu.VMEM((nb, hd), F32)],
        compiler_params=_params(est),
        name="hgrn2_sample",
    )(q_t, f_t, proj, proj, lb_col, gn, s0)


def _cmul(ar, ai, br, bi):
    return ar * br - ai * bi, ar * bi + ai * br


def _gelu_exact(x):
    return 0.5 * x * (1.0 + lax.erf(x * math.sqrt(0.5)))


def _s5_discretise_kernel(lre_ref, lim_ref, ldt_ref, pre_ref, pim_ref, zre_ref, zim_ref):
    lam_re, lam_im = lre_ref[...], lim_ref[...]
    dt = jnp.exp(ldt_ref[...])
    mag = jnp.exp(lam_re * dt)
    ab_re = mag * jnp.cos(lam_im * dt)
    ab_im = mag * jnp.sin(lam_im * dt)
    den = lam_re * lam_re + lam_im * lam_im
    nr, ni = ab_re - 1.0, ab_im
    zre_ref[...] = (nr * lam_re + ni * lam_im) / den
    zim_ref[...] = (ni * lam_re - nr * lam_im) / den
    pr, pi = ab_re, ab_im
    for kk in range(N_POW):
        pre_ref[:, kk, :] = pr
        pim_ref[:, kk, :] = pi
        pr, pi = _cmul(pr, pi, pr, pi)


def _s5_discretise(lam_re, lam_im, log_dt):
    g, n = lam_re.shape
    return pl.pallas_call(
        _s5_discretise_kernel,
        out_shape=[jax.ShapeDtypeStruct((g, N_POW, n), F32), jax.ShapeDtypeStruct((g, N_POW, n), F32),
                   jax.ShapeDtypeStruct((g, n), F32), jax.ShapeDtypeStruct((g, n), F32)],
        name="s5_discretise",
    )(lam_re, lam_im, log_dt)


def _pow_rows(xr, xi, prow_re, prow_im, gi, first, exps, nbits=4):
    for kk in range(nbits):
        ar, ai = prow_re[gi, first + kk:first + kk + 1, :], prow_im[gi, first + kk:first + kk + 1, :]
        yr, yi = _cmul(xr, xi, ar, ai)
        hit = ((exps >> kk) & 1) == 1
        xr, xi = jnp.where(hit, yr, xr), jnp.where(hit, yi, xi)
    return xr, xi


def _s5_weights_kernel(gb, prow_re, prow_im, zre_ref, zim_ref, btre_ref, btim_ref, cre_ref, cim_ref,
                       wt_ref, wx_ref, wc_ref, apw_ref):
    fl = S5_FLAT
    srow = lax.broadcasted_iota(jnp.int32, (fl, 1), 0) // S5_GROUP
    jrow = lax.broadcasted_iota(jnp.int32, (S5_STATE, 1), 0)
    for gi in range(gb):
        zr, zi = zre_ref[gi], zim_ref[gi]
        bbr, bbi = _cmul(zr, zi, btre_ref[gi], btim_ref[gi])
        xr, xi = _pow_rows(jnp.concatenate([bbr] * S5_CHUNK, axis=0), jnp.concatenate([bbi] * S5_CHUNK, axis=0),
                           prow_re, prow_im, gi, 0, (S5_CHUNK - 1) - srow)
        wx_ref[gi] = jnp.concatenate([xr, xi], axis=1).astype(BF16)
        cr, ci = _pow_rows(jnp.concatenate([cre_ref[gi]] * S5_CHUNK, axis=0),
                           jnp.concatenate([cim_ref[gi]] * S5_CHUNK, axis=0), prow_re, prow_im, gi, 0, srow)
        g0t = jnp.concatenate([cr, -ci], axis=1)
        c1r, c1i = _cmul(cr, ci, prow_re[gi, 0:1, :], prow_im[gi, 0:1, :])
        wc_ref[gi] = jnp.concatenate([c1r, -c1i], axis=1).astype(BF16)
        kflat = lax.dot_general(jnp.concatenate([bbr, bbi], axis=1), g0t, (((1,), (1,)), ((), ())),
                                preferred_element_type=F32, precision=lax.Precision.HIGHEST)
        pieces = [kflat]
        for s in range(1, S5_CHUNK):
            pieces.append(jnp.concatenate([jnp.zeros((S5_GROUP, s * S5_GROUP), F32),
                                           kflat[:, :fl - s * S5_GROUP]], axis=1))
        wt_ref[gi] = jnp.concatenate(pieces, axis=0).astype(BF16)
        pr = jnp.ones((S5_STATE, S5_STATE), F32)
        pi = jnp.zeros((S5_STATE, S5_STATE), F32)
        pr, pi = _pow_rows(pr, pi, prow_re, prow_im, gi, 4, jrow, nbits=6)
        apw_ref[gi] = jnp.concatenate([pr, pi], axis=1)


def _s5_weights(prow_re, prow_im, z_re, z_im, bt_re, bt_im, c_re, c_im, gb):
    g = S5_GROUPS
    n, p, fl = S5_STATE, S5_GROUP, S5_FLAT
    b3 = lambda a, b: pl.BlockSpec((gb, a, b), lambda i: (i, 0, 0))
    return pl.pallas_call(
        functools.partial(_s5_weights_kernel, gb),
        grid=(g // gb,),
        in_specs=[b3(N_POW, n), b3(N_POW, n), b3(1, n), b3(1, n), b3(p, n), b3(p, n), b3(p, n), b3(p, n)],
        out_specs=[b3(fl, fl), b3(fl, 2 * n), b3(fl, 2 * n), b3(n, 2 * n)],
        out_shape=[jax.ShapeDtypeStruct((g, fl, fl), BF16), jax.ShapeDtypeStruct((g, fl, 2 * n), BF16),
                   jax.ShapeDtypeStruct((g, fl, 2 * n), BF16), jax.ShapeDtypeStruct((g, n, 2 * n), F32)],
        name="s5_chunk_weights",
    )(prow_re, prow_im, z_re, z_im, bt_re, bt_im, c_re, c_im)


LANE_GROUPS = 128 // S5_GROUP
RELAYOUT_ROWS = 64


def _block_transpose8(xs):
    lane_blk = lax.broadcasted_iota(jnp.int32, (1, 128), 1) // S5_GROUP
    xs = list(xs)
    for d in (4, 2, 1):
        upper = (lane_blk & d) != 0
        nxt = list(xs)
        for a in range(8):
            if a & d:
                continue
            lo, hi = xs[a], xs[a + d]
            nxt[a] = jnp.where(upper, pltpu.roll(hi, S5_GROUP * d, axis=1), lo)
            nxt[a + d] = jnp.where(upper, hi, pltpu.roll(lo, 128 - S5_GROUP * d, axis=1))
        xs = nxt
    return xs


def _s5_prompt_kernel(nseq, u_ref, wt_ref, wx_ref, wc_ref, apw_ref, a16_ref, d_ref, z_ref, hend_ref,
                      tok_scr, flat_scr, x_scr, h_scr):
    n = S5_STATE
    gb = LANE_GROUPS
    rows = u_ref.shape[0] // S5_CHUNK
    slab = 2 * nseq
    nlo = rows // slab
    rb = RELAYOUT_ROWS
    halves = S5_FLAT // 128

    tok_scr[...] = u_ref[...].astype(F32)

    def to_flat(r, carry):
        r0 = pl.multiple_of(r * rb, rb)
        for sh in range(halves):
            xs = [tok_scr[pl.ds(r0 * S5_CHUNK + 8 * sh + s, rb, stride=S5_CHUNK), :] for s in range(8)]
            ys = _block_transpose8(xs)
            for gi in range(gb):
                flat_scr[gi, pl.ds(r0, rb), sh * 128:(sh + 1) * 128] = ys[gi]
        return carry

    lax.fori_loop(0, rows // rb, to_flat, 0)

    odd = (lax.broadcasted_iota(jnp.int32, (slab, 1), 0) % 2) == 1
    for gi in range(gb):
        u = flat_scr[gi]
        u_b = u.astype(BF16)
        x_scr[gi] = _bdot(u_b, wx_ref[gi])
        ar, ai = a16_ref[gi, :, :n], a16_ref[gi, :, n:]
        hr = jnp.zeros((slab, n), F32)
        hi = jnp.zeros((slab, n), F32)
        for j in range(nlo):
            rs = pl.ds(j, slab, stride=nlo)
            h_scr[gi, rs, :] = jnp.concatenate([hr, hi], axis=1)
            xj = x_scr[gi, rs, :]
            nr, ni = _cmul(hr, hi, ar, ai)
            hr = nr + xj[:, :n]
            hi = ni + xj[:, n:]
        mr = jnp.where(odd, pltpu.roll(hr, 1, axis=0), 0.0)
        mi = jnp.where(odd, pltpu.roll(hi, 1, axis=0), 0.0)
        pr, pi = apw_ref[gi, :, :n], apw_ref[gi, :, n:]
        for k in range(1, slab, 2):
            cr, ci = _cmul(pr, pi, mr[k:k + 1], mi[k:k + 1])
            h_scr[gi, k * nlo:(k + 1) * nlo, :n] += cr
            h_scr[gi, k * nlo:(k + 1) * nlo, n:] += ci
        lr, li = _cmul(pr[nlo - 1:nlo], pi[nlo - 1:nlo], ar, ai)
        er, ei = _cmul(mr, mi, lr, li)
        hend_ref[gi] = jnp.concatenate([hr + er, hi + ei], axis=1)
        y = (_bdot(u_b, wt_ref[gi]) + d_ref[gi] * u
             + lax.dot_general(h_scr[gi].astype(BF16), wc_ref[gi], (((1,), (1,)), ((), ())),
                               preferred_element_type=F32))
        flat_scr[gi] = _gelu_exact(y)

    def to_tokens(r, carry):
        r0 = pl.multiple_of(r * rb, rb)
        for sh in range(halves):
            ys = [flat_scr[gi, pl.ds(r0, rb), sh * 128:(sh + 1) * 128] for gi in range(gb)]
            xs = _block_transpose8(ys)
            for s in range(8):
                tok_scr[pl.ds(r0 * S5_CHUNK + 8 * sh + s, rb, stride=S5_CHUNK), :] = xs[s]
        return carry

    lax.fori_loop(0, rows // rb, to_tokens, 0)
    z_ref[...] = tok_scr[...].astype(BF16)


def _s5_prompt(proj, wt, wx, wc, apw, a16, d_flat, nseq):
    m = proj.shape[0]
    g, n, fl, gb = S5_GROUPS, S5_STATE, S5_FLAT, LANE_GROUPS
    rows = m // S5_CHUNK
    b3 = lambda a, b: pl.BlockSpec((gb, a, b), lambda i: (i, 0, 0))
    est = (4 * m * 128 * 2 + m * 128 * 4 + gb * rows * (fl + 4 * n) * 4 + 6 * rows * fl * 4
           + 2 * gb * (fl * fl + 4 * n * fl) * 2)
    return pl.pallas_call(
        functools.partial(_s5_prompt_kernel, nseq),
        grid=(g // gb,),
        in_specs=[pl.BlockSpec((m, 128), lambda i: (0, OFF_U // 128 + i)),
                  b3(fl, fl), b3(fl, 2 * n), b3(fl, 2 * n), b3(n, 2 * n), b3(1, 2 * n), b3(1, fl)],
        out_specs=[pl.BlockSpec((m, 128), lambda i: (0, i)), b3(2 * nseq, 2 * n)],
        out_shape=[jax.ShapeDtypeStruct((m, S5_WIDTH), BF16), jax.ShapeDtypeStruct((g, 2 * nseq, 2 * n), F32)],
        scratch_shapes=[pltpu.VMEM((m, 128), F32), pltpu.VMEM((gb, rows, fl), F32),
                        pltpu.VMEM((gb, rows, 2 * n), F32), pltpu.VMEM((gb, rows, 2 * n), F32)],
        compiler_params=_params(est),
        name="s5_prompt",
    )(proj, wt, wx, wc, apw, a16, d_flat)


def _s5_sample_kernel(gb, u_ref, bst_ref, cre_ref, cim_ref, a1_ref, hre_ref, him_ref, d_ref,
                      z_ref, ore_ref, oim_ref):
    n = S5_STATE
    u = u_ref[...].astype(F32)
    lane_g = lax.broadcasted_iota(jnp.int32, (1, gb * S5_GROUP), 1) // S5_GROUP
    bst = bst_ref[...]
    cst = jnp.concatenate([cre_ref[...], -cim_ref[...]], axis=1)
    row_g = lax.broadcasted_iota(jnp.int32, (gb * S5_GROUP, 1), 0) // S5_GROUP
    y = d_ref[...] * u
    for gi in range(gb):
        mine = lane_g == gi
        bu = _bdot(jnp.where(mine, u, 0.0).astype(BF16), bst)
        ar, ai = a1_ref[gi, :, :n], a1_ref[gi, :, n:]
        nr, ni = _cmul(hre_ref[gi], him_ref[gi], ar, ai)
        hr = nr + bu[:, :n]
        hi = ni + bu[:, n:]
        ore_ref[gi] = hr
        oim_ref[gi] = hi
        cmat = jnp.where(row_g == gi, cst, 0.0).astype(BF16)
        y = y + lax.dot_general(jnp.concatenate([hr, hi], axis=1).astype(BF16), cmat, (((1,), (1,)), ((), ())),
                                preferred_element_type=F32)
    z_ref[...] = _gelu_exact(y).astype(BF16)


def _s5_sample(proj, bstack, c_re, c_im, a1, h_re, h_im, d_row, gb):
    nb = proj.shape[0]
    g, n, p = S5_GROUPS, S5_STATE, S5_GROUP
    lanes = gb * p
    b3 = lambda a, b: pl.BlockSpec((gb, a, b), lambda i: (i, 0, 0))
    st = b3(nb, n)
    return pl.pallas_call(
        functools.partial(_s5_sample_kernel, gb),
        grid=(g // gb,),
        in_specs=[pl.BlockSpec((nb, lanes), lambda i: (0, OFF_U // lanes + i)),
                  pl.BlockSpec((lanes, 2 * n), lambda i: (i, 0)),
                  pl.BlockSpec((lanes, n), lambda i: (i, 0)), pl.BlockSpec((lanes, n), lambda i: (i, 0)),
                  b3(1, 2 * n), st, st,
                  pl.BlockSpec((1, lanes), lambda i: (0, i))],
        out_specs=[pl.BlockSpec((nb, lanes), lambda i: (0, i)), st, st],
        out_shape=[jax.ShapeDtypeStruct((nb, S5_WIDTH), BF16),
                   jax.ShapeDtypeStruct((g, nb, n), F32), jax.ShapeDtypeStruct((g, nb, n), F32)],
        name="s5_sample",
    )(proj, bstack, c_re, c_im, a1, h_re, h_im, d_row)


def _dense_tail(x, proj, o_a, z, w, tm, ffn_bf16=None):
    o_b = _glu(z, w['w_s5_glu'], tm, 512)
    merged = _merge(o_a, o_b, w['w_proj_a'], w['w_proj_b'], proj, tm, 512)
    x1 = _outproj(merged, w['w_out'], x, tm, 512)
    if ffn_bf16 is None:
        act, *ffn_bf16 = _ffn_up(x1, w['norm2_g'], w['w_ffn_up'], w['w_ffn_up'], FFN_HIDDEN, tm, 256,
                                 w_down=w['w_ffn_down'])
    else:
        act, = _ffn_up(x1, w['norm2_g'], ffn_bf16[0], ffn_bf16[0], FFN_HIDDEN, tm, 256)
    y = _ffn_down(act, ffn_bf16[1], x1, w['final_norm_g'], min(tm, 1024), FFN_HIDDEN // 4)
    return y, ffn_bf16


def kernel(x_prompt, x_sample, state_hgrn, state_s5_re, state_s5_im, lb_logits, norm1_g, w_in, hgrn_norm_g,
           s5_lam_re, s5_lam_im, s5_log_dt, s5_B_re, s5_B_im, s5_C_re, s5_C_im, s5_D, w_s5_glu, w_proj_a,
           w_proj_b, w_out, norm2_g, w_ffn_up, w_ffn_down, final_norm_g):
    l = 0
    bp, seq, d = x_prompt.shape
    nb = x_sample.shape[0]
    g, n, p = S5_GROUPS, S5_STATE, S5_GROUP
    w = {'w_s5_glu': w_s5_glu[l], 'w_proj_a': w_proj_a[l], 'w_proj_b': w_proj_b[l], 'w_out': w_out[l],
         'norm2_g': norm2_g[l][None, :], 'w_ffn_up': w_ffn_up[l], 'w_ffn_down': w_ffn_down[l],
         'final_norm_g': final_norm_g[None, :]}
    g1 = norm1_g[l][None, :]
    gn = hgrn_norm_g[l][None, :]

    xp = x_prompt.reshape(bp * seq, d)
    xs = x_sample.reshape(nb, d)
    proj_p, w_in_bf16 = _inproj(xp, g1, w_in[l], 2048, 512, emit_w=True)
    proj_s, = _inproj(xs, g1, w_in_bf16, nb, 512)

    lb = _lower_bound(lb_logits)
    oa_p, sh_p = _hgrn_prompt(proj_p, lb, gn, bp, seq, 512)
    q_t = proj_s[:, OFF_Q:OFF_Q + HGRN_WIDTH].astype(F32).T
    f_t = proj_s[:, OFF_F:OFF_F + HGRN_WIDTH].astype(F32).T
    oa_s, sh_s = _hgrn_sample(q_t, f_t, proj_s, lb.reshape(HGRN_WIDTH, 1), gn, state_hgrn[l])

    prow_re, prow_im, z_re, z_im = _s5_discretise(s5_lam_re[l], s5_lam_im[l], s5_log_dt[l][:, None])
    bt_re, bt_im = s5_B_re[l].transpose(0, 2, 1), s5_B_im[l].transpose(0, 2, 1)
    wt, wx, wc, apw = _s5_weights(prow_re, prow_im, z_re[:, None, :], z_im[:, None, :], bt_re, bt_im,
                                  s5_C_re[l], s5_C_im[l], 8)
    d_gp = s5_D[l].reshape(g, 1, p)
    d_flat = jnp.tile(d_gp, (1, 1, S5_CHUNK))
    a16 = jnp.concatenate([prow_re[:, 4:5, :], prow_im[:, 4:5, :]], axis=2)
    a1 = jnp.concatenate([prow_re[:, 0:1, :], prow_im[:, 0:1, :]], axis=2)

    z_p, hend = _s5_prompt(proj_p, wt, wx, wc, apw, a16, d_flat, bp)
    hend = hend[:, 1::2, :]
    s5re_p = hend[:, :, :n].transpose(1, 0, 2)
    s5im_p = hend[:, :, n:].transpose(1, 0, 2)

    bstack = wx[:, S5_FLAT - p:, :].reshape(g * p, 2 * n)
    gbs = 128 // p
    c_rows_re, c_rows_im = s5_C_re[l].reshape(g * p, n), s5_C_im[l].reshape(g * p, n)
    z_s, s5re_s, s5im_s = _s5_sample(proj_s, bstack, c_rows_re, c_rows_im, a1,
                                     state_s5_re[l].transpose(1, 0, 2), state_s5_im[l].transpose(1, 0, 2),
                                     s5_D[l][None, :], gbs)

    y_p, ffn_bf16 = _dense_tail(xp, proj_p, oa_p, z_p, w, 2048)
    y_s, _ = _dense_tail(xs, proj_s, oa_s, z_s, w, nb, ffn_bf16)

    return (y_p.reshape(bp, seq, d), y_s.reshape(nb, 1, d),
            sh_p[None], s5re_p[None], s5im_p[None],
            sh_s[None], s5re_s.transpose(1, 0, 2)[None], s5im_s.transpose(1, 0, 2)[None])
```
